```python
import math
import jax
import jax.numpy as jnp
from jax import lax
import numpy as np

D_MODEL = 2048
BATCH = 1
SEQ = 8192
DEPTH = 1

N_META = 16
GDN_HEADS = 8
GDN_DIM = 128
GDN_WIDTH = GDN_HEADS * GDN_DIM
CONV_WIDTH = 4
CHUNK = 64
DIFF_HEADS = 8
DIFF_DIM = 64
DIFF_VDIM = 2 * DIFF_DIM
DIFF_QK_WIDTH = DIFF_HEADS * 2 * DIFF_DIM
DIFF_WIDTH = DIFF_HEADS * DIFF_VDIM
Q_BLOCK = 128
ROPE_THETA = 10000.0
MIX_WIDTH = GDN_WIDTH + DIFF_WIDTH
PROJ_SIZES = (GDN_WIDTH, GDN_WIDTH, GDN_WIDTH, GDN_WIDTH, GDN_HEADS, GDN_HEADS,
              DIFF_QK_WIDTH, DIFF_QK_WIDTH, DIFF_WIDTH)
PROJ_WIDTH = 4 * GDN_WIDTH + 2 * GDN_HEADS + 2 * DIFF_QK_WIDTH + DIFF_WIDTH
D_FF = (8 * D_MODEL + 3 * 256 - 1) // (3 * 256) * 256
NORM_EPS = 1e-6
MASK_VALUE = -1e30

kernel_name = "hymba_gdn_diffattn_swiglu"


def rms_norm(x, gain):
    xf = x.astype(jnp.float32)
    y = xf * lax.rsqrt(jnp.mean(xf * xf, axis=-1, keepdims=True) + NORM_EPS) * gain.astype(jnp.float32)
    return y.astype(x.dtype)


def l2_normalize(x):
    return x * lax.rsqrt(jnp.sum(x * x, axis=-1, keepdims=True) + NORM_EPS)


def causal_conv(x, w):
    L = x.shape[1]
    xp = jnp.pad(x, ((0, 0), (CONV_WIDTH - 1, 0), (0, 0)))
    y = xp[:, 0:L] * w[0]
    for j in range(1, CONV_WIDTH):
        y = y + xp[:, j:j + L] * w[j]
    return y


def rotary(x, positions):
    half = DIFF_DIM // 2
    inv_freq = ROPE_THETA ** (-jnp.arange(half, dtype=jnp.float32) / half)
    ang = positions.astype(jnp.float32)[:, None] * inv_freq[None, :]
    cos = jnp.cos(ang)[None, :, None, None, :]
    sin = jnp.sin(ang)[None, :, None, None, :]
    xf = x.astype(jnp.float32)
    x1, x2 = xf[..., :half], xf[..., half:]
    return jnp.concatenate([x1 * cos - x2 * sin, x2 * cos + x1 * sin], axis=-1).astype(x.dtype)


def gated_deltanet(q, k, v, z, b, a, conv_w, a_log, dt_bias, o_gain):
    B, L, _ = q.shape
    out_dtype = q.dtype
    qkv = jax.nn.silu(causal_conv(jnp.concatenate([q, k, v], axis=-1), conv_w))
    q, k, v = jnp.split(qkv, 3, axis=-1)
    heads = lambda t: t.reshape(B, L, GDN_HEADS, GDN_DIM).astype(jnp.float32)
    q = l2_normalize(heads(q)) * (GDN_DIM ** -0.5)
    k = l2_normalize(heads(k))
    v = heads(v)
    beta = jax.nn.sigmoid(b.astype(jnp.float32))
    g = -jnp.exp(a_log.astype(jnp.float32)) * jax.nn.softplus(
        a.astype(jnp.float32) + dt_bias.astype(jnp.float32))

    pad = (-L) % CHUNK
    n_chunks = (L + pad) // CHUNK

    def to_chunks(t):
        t = jnp.pad(t, [(0, 0), (pad, 0)] + [(0, 0)] * (t.ndim - 2))
        t = t.reshape((B, n_chunks, CHUNK) + t.shape[2:])
        return jnp.moveaxis(t, 3, 1)

    q, k, v, beta, g = to_chunks(q), to_chunks(k), to_chunks(v), to_chunks(beta), to_chunks(g)
    gc = jnp.cumsum(g, axis=-1)
    idx = jnp.arange(CHUNK)
    incl = idx[:, None] >= idx[None, :]
    strict = idx[:, None] > idx[None, :]
    decay = jnp.exp(jnp.where(incl, gc[..., :, None] - gc[..., None, :], MASK_VALUE))

    k_beta = k * beta[..., None]
    v_beta = v * beta[..., None]
    lmat = jnp.where(strict, jnp.einsum('bhncd,bhnsd->bhncs', k_beta, k) * decay, 0.0)
    eye = jnp.broadcast_to(jnp.eye(CHUNK, dtype=jnp.float32), lmat.shape)
    t_inv = lax.linalg.triangular_solve(lmat, eye, left_side=True, lower=True,
                                        unit_diagonal=True)
    u = jnp.einsum('bhncs,bhnsd->bhncd', t_inv, v_beta)
    w = jnp.einsum('bhncs,bhnsd->bhncd', t_inv, k_beta * jnp.exp(gc)[..., None])
    attn = jnp.einsum('bhncd,bhnsd->bhncs', q, k) * decay

    def step(state, inp):
        q_i, k_i, u_i, w_i, a_i, gc_i = inp
        v_new = u_i - jnp.einsum('bhck,bhkv->bhcv', w_i, state)
        o_i = (jnp.einsum('bhck,bhkv->bhcv', q_i * jnp.exp(gc_i)[..., None], state)
               + jnp.einsum('bhcs,bhsv->bhcv', a_i, v_new))
        g_last = gc_i[..., -1]
        k_dec = k_i * jnp.exp(g_last[..., None] - gc_i)[..., None]
        state = state * jnp.exp(g_last)[..., None, None] + jnp.einsum('bhck,bhcv->bhkv', k_dec, v_new)
        return state, o_i

    xs = tuple(jnp.moveaxis(t, 2, 0) for t in (q, k, u, w, attn, gc))
    state0 = jnp.zeros((B, GDN_HEADS, GDN_DIM, GDN_DIM), jnp.float32)
    _, o = lax.scan(step, state0, xs)
    o = jnp.transpose(o, (1, 0, 3, 2, 4)).reshape(B, n_chunks * CHUNK, GDN_HEADS, GDN_DIM)[:, pad:]
    zf = z.reshape(B, L, GDN_HEADS, GDN_DIM).astype(jnp.float32)
    o = rms_norm(o, o_gain) * jax.nn.silu(zf)
    return o.reshape(B, L, GDN_WIDTH).astype(out_dtype)


def diff_attention(q, k, v, q_gain, k_gain, lq1, lk1, lq2, lk2, sub_gain, positions, lambda_init):
    B, L, _ = q.shape
    out_dtype = q.dtype
    q = q.reshape(B, L, DIFF_HEADS, 2, DIFF_DIM)
    k = k.reshape(B, L, DIFF_HEADS, 2, DIFF_DIM)
    v = v.reshape(B, L, DIFF_HEADS, DIFF_VDIM)
    q = rotary(rms_norm(q, q_gain), positions)
    k = rotary(rms_norm(k, k_gain), positions)
    f32 = lambda t: t.astype(jnp.float32)
    lam = (jnp.exp(jnp.sum(f32(lq1) * f32(lk1))) - jnp.exp(jnp.sum(f32(lq2) * f32(lk2)))
           + lambda_init)
    scale = DIFF_DIM ** -0.5

    pad = (-L) % Q_BLOCK
    lp = L + pad
    n_blocks = lp // Q_BLOCK
    pad_seq = lambda t: jnp.pad(t, [(0, 0), (pad, 0)] + [(0, 0)] * (t.ndim - 2))
    kp = f32(pad_seq(k))
    vp = f32(pad_seq(v))
    qb = jnp.moveaxis(pad_seq(q).reshape(B, n_blocks, Q_BLOCK, DIFF_HEADS, 2, DIFF_DIM), 1, 0)
    kpos = jnp.arange(lp)

    def block(args):
        q_blk, blk = args
        qpos = blk * Q_BLOCK + jnp.arange(Q_BLOCK)
        s = jnp.einsum('bqhmd,bkhmd->bhmqk', f32(q_blk), kp) * scale
        mask = (kpos[None, :] <= qpos[:, None]) & (kpos[None, :] >= pad)
        p = jax.nn.softmax(jnp.where(mask, s, MASK_VALUE), axis=-1)
        wts = p[:, :, 0] - lam * p[:, :, 1]
        return jnp.einsum('bhqk,bkhe->bqhe', wts, vp)

    o = lax.map(block, (qb, jnp.arange(n_blocks)))
    o = jnp.moveaxis(o, 0, 1).reshape(B, lp, DIFF_HEADS, DIFF_VDIM)[:, pad:]
    o = rms_norm(o, sub_gain) * (1.0 - lambda_init)
    return o.reshape(B, L, DIFF_WIDTH).astype(out_dtype)


def setup_inputs(seed: int = 0) -> dict:
    key = jax.random.key(seed)
    ks = jax.random.split(key, 20)
    nrm = lambda k, shape, s: jax.random.normal(k, shape, jnp.float32) * s
    gain = lambda k, n: 1.0 + nrm(k, (DEPTH, n), 0.02)
    dt = jnp.exp(jax.random.uniform(ks[5], (DEPTH, GDN_HEADS), jnp.float32,
                                    math.log(1e-3), math.log(1e-1)))
    return {
        "x": nrm(ks[0], (BATCH, SEQ, D_MODEL), 1.0),
        "meta_tokens": nrm(ks[1], (N_META, D_MODEL), 1.0),
        "attn_norm": gain(ks[2], D_MODEL),
        "w_in": nrm(ks[3], (DEPTH, D_MODEL, PROJ_WIDTH), D_MODEL ** -0.5),
        "conv_w": nrm(ks[4], (DEPTH, CONV_WIDTH, 3 * GDN_WIDTH), CONV_WIDTH ** -0.5),
        "a_log": jnp.log(jax.random.uniform(ks[6], (DEPTH, GDN_HEADS), jnp.float32, 1.0, 16.0)),
        "dt_bias": dt + jnp.log(-jnp.expm1(-dt)),
        "gdn_norm": gain(ks[7], GDN_DIM),
        "q_norm": gain(ks[8], DIFF_DIM),
        "k_norm": gain(ks[9], DIFF_DIM),
        "lambda_q1": nrm(ks[10], (DEPTH, DIFF_DIM), 0.1),
        "lambda_k1": nrm(ks[11], (DEPTH, DIFF_DIM), 0.1),
        "lambda_q2": nrm(ks[12], (DEPTH, DIFF_DIM), 0.1),
        "lambda_k2": nrm(ks[13], (DEPTH, DIFF_DIM), 0.1),
        "diff_norm": gain(ks[14], DIFF_VDIM),
        "w_out": nrm(ks[15], (DEPTH, MIX_WIDTH, D_MODEL), MIX_WIDTH ** -0.5),
        "ffn_norm": gain(ks[16], D_MODEL),
        "w_gate_up": nrm(ks[17], (DEPTH, D_MODEL, 2 * D_FF), D_MODEL ** -0.5),
        "w_down": nrm(ks[18], (DEPTH, D_FF, D_MODEL), D_FF ** -0.5),
    }


def reference(x, meta_tokens, attn_norm, w_in, conv_w, a_log, dt_bias, gdn_norm, q_norm, k_norm,
              lambda_q1, lambda_k1, lambda_q2, lambda_k2, diff_norm, w_out, ffn_norm,
              w_gate_up, w_down):
    B = x.shape[0]
    meta = jnp.broadcast_to(meta_tokens[None].astype(x.dtype), (B, N_META, D_MODEL))
    h = jnp.concatenate([meta, x], axis=1)
    L = h.shape[1]
    positions = jnp.arange(L)
    split_at = [int(s) for s in np.cumsum(PROJ_SIZES)[:-1]]
    for layer in range(DEPTH):
        lambda_init = 0.8 - 0.6 * math.exp(-0.3 * layer)
        n = rms_norm(h, attn_norm[layer])
        proj = jnp.einsum('bld,dp->blp', n, w_in[layer])
        gq, gk, gv, gz, gb, ga, dq, dk, dv = jnp.split(proj, split_at, axis=-1)
        o_gdn = gated_deltanet(gq, gk, gv, gz, gb, ga, conv_w[layer], a_log[layer],
                               dt_bias[layer], gdn_norm[layer])
        o_diff = diff_attention(dq, dk, dv, q_norm[layer], k_norm[layer], lambda_q1[layer],
                                lambda_k1[layer], lambda_q2[layer], lambda_k2[layer],
                                diff_norm[layer], positions, lambda_init)
        mix = jnp.concatenate([o_gdn, o_diff], axis=-1)
        h = h + jnp.einsum('blm,md->bld', mix, w_out[layer])
        n = rms_norm(h, ffn_norm[layer])
        gate, up = jnp.split(jnp.einsum('bld,df->blf', n, w_gate_up[layer]), 2, axis=-1)
        h = h + jnp.einsum('blf,fd->bld', jax.nn.silu(gate) * up, w_down[layer])
    return h[:, N_META:]
```

```python
import functools
import math

import jax
import jax.numpy as jnp
import numpy as np
from jax import lax
from jax.experimental import pallas as pl
from jax.experimental.pallas import tpu as pltpu

F32 = jnp.float32
BF16 = jnp.bfloat16

D_MODEL = 2048
N_META = 16
GDN_HEADS = 8
GDN_DIM = 128
GDN_WIDTH = GDN_HEADS * GDN_DIM
CONV_WIDTH = 4
CHUNK = 64
DIFF_HEADS = 8
DIFF_DIM = 64
DIFF_VDIM = 2 * DIFF_DIM
DIFF_WIDTH = DIFF_HEADS * DIFF_VDIM
ROPE_THETA = 10000.0
D_FF = 5632
NORM_EPS = 1e-6
MASK_VALUE = -1e30
LAMBDA_INIT = 0.8 - 0.6 * math.exp(-0.3 * 0)

LANES = 128
META_BLOCK = 128
META_PAD = META_BLOCK - N_META
GATE_LANES = 128
VMEM_LIMIT = 56 * 1024 * 1024


def _pick(n, candidates):
    for c in candidates:
        if n % c == 0:
            return c
    raise ValueError(f"no tile in {candidates} divides {n}")


def _params(sem, vmem=VMEM_LIMIT):
    return pltpu.CompilerParams(dimension_semantics=sem, vmem_limit_bytes=vmem)


def _dot(a, b):
    return jnp.dot(a, b, preferred_element_type=F32)


def _dot_nt(a, b):
    return lax.dot_general(a, b, (((1,), (1,)), ((), ())), preferred_element_type=F32)


def _dot_tn(a, b):
    return lax.dot_general(a, b, (((0,), (0,)), ((), ())), preferred_element_type=F32)


def _softplus(x):
    return jnp.maximum(x, 0.0) + jnp.log1p(jnp.exp(-jnp.abs(x)))


def _silu(x):
    return x * jax.nn.sigmoid(x)


def _prenorm_gate_kernel(h_ref, gain_ref, wba_ref, alog_ref, dtb_ref, n_ref, gcol_ref, grow_ref):
    h = h_ref[...]
    ms = jnp.mean(h * h, axis=-1, keepdims=True)
    n = (h * lax.rsqrt(ms + NORM_EPS) * gain_ref[...]).astype(BF16)
    n_ref[...] = n
    ba = _dot(n, wba_ref[...])
    beta = jax.nn.sigmoid(ba)
    g = -jnp.exp(alog_ref[...]) * _softplus(ba + dtb_ref[...])
    row = lax.broadcasted_iota(jnp.int32, ba.shape, 0) % CHUNK
    gc = g
    for d in (1, 2, 4, 8, 16, 32):
        gc = gc + jnp.where(row >= d, pltpu.roll(gc, d, axis=0), 0.0)
    lane = lax.broadcasted_iota(jnp.int32, ba.shape, 1)
    out = jnp.where(lane < GDN_HEADS, beta, gc)
    gcol_ref[...] = out
    grow_ref[...] = out.T[: 2 * GDN_HEADS]


def _prenorm_gate(h, gain, wba, alog, dtb):
    rows = h.shape[0]
    tm = _pick(rows, (640, 128))
    return pl.pallas_call(
        _prenorm_gate_kernel,
        grid=(rows // tm,),
        in_specs=[
            pl.BlockSpec((tm, D_MODEL), lambda i: (i, 0)),
            pl.BlockSpec((1, D_MODEL), lambda i: (0, 0)),
            pl.BlockSpec((D_MODEL, GATE_LANES), lambda i: (0, 0)),
            pl.BlockSpec((1, GATE_LANES), lambda i: (0, 0)),
            pl.BlockSpec((1, GATE_LANES), lambda i: (0, 0)),
        ],
        out_specs=[
            pl.BlockSpec((tm, D_MODEL), lambda i: (i, 0)),
            pl.BlockSpec((tm, GATE_LANES), lambda i: (i, 0)),
            pl.BlockSpec((2 * GDN_HEADS, tm), lambda i: (0, i)),
        ],
        out_shape=[
            jax.ShapeDtypeStruct((rows, D_MODEL), BF16),
            jax.ShapeDtypeStruct((rows, GATE_LANES), F32),
            jax.ShapeDtypeStruct((2 * GDN_HEADS, rows), F32),
        ],
        compiler_params=_params(("parallel",)),
        name="prenorm_gate",
    )(h, gain, wba, alog, dtb)


def _matmul_kernel(a_ref, b_ref, o_ref):
    o_ref[...] = _dot(a_ref[...], b_ref[...]).astype(o_ref.dtype)


def _matmul(a, b, out_dtype, tm, tn, name):
    m, k = a.shape
    n = b.shape[1]
    return pl.pallas_call(
        _matmul_kernel,
        grid=(n // tn, m // tm),
        in_specs=[
            pl.BlockSpec((tm, k), lambda j, i: (i, 0)),
            pl.BlockSpec((k, tn), lambda j, i: (0, j)),
        ],
        out_specs=pl.BlockSpec((tm, tn), lambda j, i: (i, j)),
        out_shape=jax.ShapeDtypeStruct((m, n), out_dtype),
        compiler_params=_params(("parallel", "parallel")),
        name=name,
    )(a, b)


GDN_LOCAL_CHUNKS = 2


def _causal_conv_silu(x, prev, w):
    r8 = lax.broadcasted_iota(jnp.int32, prev.shape, 0)
    y = x * w[CONV_WIDTH - 1:CONV_WIDTH]
    for d in range(1, CONV_WIDTH):
        shifted = pltpu.roll(x, d, axis=0)
        top = jnp.where(r8 < d, pltpu.roll(prev, d, axis=0), shifted[:8])
        shifted = jnp.concatenate([top, shifted[8:]], axis=0)
        y = y + shifted * w[CONV_WIDTH - 1 - d:CONV_WIDTH - d]
    return _silu(y)


def _unit_lower_inverse(lm, ii, jj):
    xor = ii ^ jj
    eye = jnp.where(ii == jj, 1.0, 0.0)
    x = eye - jnp.where(xor == 1, lm, 0.0)
    level = 1
    while (2 << level) <= CHUNK:
        c = jnp.where((xor >> level) == 1, lm, 0.0).astype(BF16)
        y = _dot(c, x.astype(BF16))
        x = x - _dot(x.astype(BF16), y.astype(BF16))
        level += 1
    return x


def _gdn_local_kernel(q_ref, k_ref, v_ref, qp_ref, kp_ref, vp_ref, cw_ref, gcol_ref, grow_ref,
                      w_ref, u_ref, qg_ref, kd_ref, attn_ref, egl_ref):
    cw = cw_ref[...]
    q_all = _causal_conv_silu(q_ref[...], qp_ref[...], cw[:, 0:GDN_WIDTH])
    k_all = _causal_conv_silu(k_ref[...], kp_ref[...], cw[:, GDN_WIDTH:2 * GDN_WIDTH])
    v_all = _causal_conv_silu(v_ref[...], vp_ref[...], cw[:, 2 * GDN_WIDTH:3 * GDN_WIDTH])
    gcol = gcol_ref[...]
    grow = grow_ref[...]
    ii = lax.broadcasted_iota(jnp.int32, (CHUNK, CHUNK), 0)
    jj = lax.broadcasted_iota(jnp.int32, (CHUNK, CHUNK), 1)
    for c in range(GDN_LOCAL_CHUNKS):
        rs = slice(c * CHUNK, (c + 1) * CHUNK)
        for h in range(GDN_HEADS):
            ls = slice(h * GDN_DIM, (h + 1) * GDN_DIM)
            q = q_all[rs, ls]
            k = k_all[rs, ls]
            v = v_all[rs, ls]
            q = q * lax.rsqrt(jnp.sum(q * q, axis=-1, keepdims=True) + NORM_EPS) * (GDN_DIM ** -0.5)
            k = k * lax.rsqrt(jnp.sum(k * k, axis=-1, keepdims=True) + NORM_EPS)
            beta_c = gcol[rs, h:h + 1]
            gc_c = gcol[rs, GDN_HEADS + h:GDN_HEADS + h + 1]
            gc_r = grow[GDN_HEADS + h:GDN_HEADS + h + 1, rs]
            kb = k * beta_c
            vb = v * beta_c
            egc = jnp.exp(gc_c)
            k16 = k.astype(BF16)
            kq = _dot_nt(jnp.concatenate([kb.astype(BF16), q.astype(BF16)], axis=0), k16)
            decay = jnp.exp(jnp.where(ii >= jj, gc_c - gc_r, MASK_VALUE))
            lm = jnp.where(ii > jj, kq[:CHUNK] * decay, 0.0)
            attn = kq[CHUNK:] * decay
            t_inv = _unit_lower_inverse(lm, ii, jj)
            rhs = jnp.concatenate([vb.astype(BF16), (kb * egc).astype(BF16)], axis=1)
            uw = _dot(t_inv.astype(BF16), rhs)
            u_ref[rs, ls] = uw[:, :GDN_DIM]
            w_ref[rs, ls] = uw[:, GDN_DIM:].astype(BF16)
            qg_ref[rs, ls] = (q * egc).astype(BF16)
            gc_last = gc_c[CHUNK - 1:CHUNK]
            kd_ref[rs, ls] = (k * jnp.exp(gc_last - gc_c)).astype(BF16)
            attn_ref[h, rs, :] = attn.astype(BF16)
            egl_ref[c, h:h + 1, :] = jnp.broadcast_to(jnp.exp(gc_last), (1, GDN_DIM))


def _gdn_local(proj, conv_w, gcol, grow):
    rows = proj.shape[0]
    rb = GDN_LOCAL_CHUNKS * CHUNK
    nsteps = rows // rb
    n8 = rows // 8
    blk = lambda col: pl.BlockSpec((rb, GDN_WIDTH), lambda i, col=col: (i, col))
    prev = lambda col: pl.BlockSpec((8, GDN_WIDTH), lambda i, col=col: ((i * (rb // 8) + n8 - 1) % n8, col))
    row_out = lambda dt: jax.ShapeDtypeStruct((rows, GDN_WIDTH), dt)
    return pl.pallas_call(
        _gdn_local_kernel,
        grid=(nsteps,),
        in_specs=[
            blk(0), blk(1), blk(2), prev(0), prev(1), prev(2),
            pl.BlockSpec((CONV_WIDTH, 3 * GDN_WIDTH), lambda i: (0, 0)),
            pl.BlockSpec((rb, GATE_LANES), lambda i: (i, 0)),
            pl.BlockSpec((2 * GDN_HEADS, rb), lambda i: (0, i)),
        ],
        out_specs=[
            pl.BlockSpec((rb, GDN_WIDTH), lambda i: (i, 0)),
            pl.BlockSpec((rb, GDN_WIDTH), lambda i: (i, 0)),
            pl.BlockSpec((rb, GDN_WIDTH), lambda i: (i, 0)),
            pl.BlockSpec((rb, GDN_WIDTH), lambda i: (i, 0)),
            pl.BlockSpec((GDN_HEADS, rb, CHUNK), lambda i: (0, i, 0)),
            pl.BlockSpec((GDN_LOCAL_CHUNKS, GDN_HEADS, GDN_DIM), lambda i: (i, 0, 0)),
        ],
        out_shape=[
            row_out(BF16),
            row_out(F32),
            row_out(BF16),
            row_out(BF16),
            jax.ShapeDtypeStruct((GDN_HEADS, rows, CHUNK), BF16),
            jax.ShapeDtypeStruct((rows // CHUNK, GDN_HEADS, GDN_DIM), F32),
        ],
        compiler_params=_params(("parallel",)),
        name="gdn_local",
    )(proj, proj, proj, proj, proj, proj, conv_w, gcol, grow)


def _gdn_state_kernel(w_ref, u_ref, qg_ref, kd_ref, attn_ref, egl_ref, z_ref, gain_ref, o_ref, s_ref):
    @pl.when(pl.program_id(0) == 0)
    def _():
        s_ref[...] = jnp.zeros_like(s_ref)

    gain = gain_ref[...]
    for h in range(GDN_HEADS):
        ls = slice(h * GDN_DIM, (h + 1) * GDN_DIM)
        s = s_ref[h]
        r = _dot(jnp.concatenate([w_ref[:, ls], qg_ref[:, ls]], axis=0), s.astype(BF16))
        v_new = (u_ref[:, ls] - r[:CHUNK]).astype(BF16)
        o = r[CHUNK:] + _dot(attn_ref[h], v_new)
        s_ref[h] = s * egl_ref[0, h:h + 1, :] + _dot_tn(kd_ref[:, ls], v_new)
        on = o * lax.rsqrt(jnp.mean(o * o, axis=-1, keepdims=True) + NORM_EPS) * gain
        o_ref[:, ls] = (on * _silu(z_ref[:, ls])).astype(BF16)


def _gdn_state(w, u, qg, kd, attn, egl, proj, gain):
    rows = w.shape[0]
    nchunks = rows // CHUNK
    nseq = nchunks - 1
    phys = lambda c: (c + nchunks - 1) % nchunks
    rowblk = lambda col: pl.BlockSpec((CHUNK, GDN_WIDTH), lambda c, col=col: (phys(c), col))
    return pl.pallas_call(
        _gdn_state_kernel,
        grid=(nseq,),
        in_specs=[
            rowblk(0), rowblk(0), rowblk(0), rowblk(0),
            pl.BlockSpec((GDN_HEADS, CHUNK, CHUNK), lambda c: (0, phys(c), 0)),
            pl.BlockSpec((1, GDN_HEADS, GDN_DIM), lambda c: (phys(c), 0, 0)),
            rowblk(3),
            pl.BlockSpec((1, GDN_DIM), lambda c: (0, 0)),
        ],
        out_specs=pl.BlockSpec((CHUNK, GDN_WIDTH), lambda c: (jnp.maximum(c - 1, 0), 0)),
        out_shape=jax.ShapeDtypeStruct((rows - META_BLOCK, GDN_WIDTH), BF16),
        scratch_shapes=[pltpu.VMEM((GDN_HEADS, GDN_DIM, GDN_DIM), F32)],
        compiler_params=_params(("arbitrary",)),
        name="gdn_state",
    )(w, u, qg, kd, attn, egl, proj, gain)


def _attn_prep_kernel(q_ref, k_ref, v_ref, cos_ref, sin_ref, qg_ref, kg_ref, gsum_ref,
                      q2_ref, kr_ref, vb_ref):
    cos = cos_ref[...]
    sin = sin_ref[...]
    gsum = gsum_ref[...]
    lane = lax.broadcasted_iota(jnp.int32, cos.shape, 1)
    first_half = (lane % DIFF_DIM) < (DIFF_DIM // 2)
    low_map = lane < DIFF_DIM

    def norm_rope(x, gain):
        ms = _dot((x * x).astype(BF16), gsum) * (1.0 / DIFF_DIM)
        xn = x * lax.rsqrt(ms + NORM_EPS) * gain
        rot = jnp.where(first_half, pltpu.roll(xn, LANES - DIFF_DIM // 2, axis=1),
                        pltpu.roll(xn, DIFF_DIM // 2, axis=1))
        return xn * cos + rot * sin

    for h in range(DIFF_HEADS):
        ls = slice(h * DIFF_VDIM, (h + 1) * DIFF_VDIM)
        q = norm_rope(q_ref[:, ls], qg_ref[...]) * (DIFF_DIM ** -0.5)
        q2_ref[0, :, ls] = jnp.where(low_map, q, 0.0).astype(BF16)
        q2_ref[1, :, ls] = jnp.where(low_map, 0.0, q).astype(BF16)
        kr_ref[:, ls] = norm_rope(k_ref[:, ls], kg_ref[...]).astype(BF16)
    vb_ref[...] = v_ref[...].astype(BF16)


def _attn_prep(proj, cos, sin, qgain, kgain, gsum):
    rows = proj.shape[0]
    tm = _pick(rows, (640, 128))
    col = lambda c: pl.BlockSpec((tm, DIFF_WIDTH), lambda i, c=c: (i, c))
    small = lambda shape: pl.BlockSpec(shape, lambda i: (0,) * len(shape))
    return pl.pallas_call(
        _attn_prep_kernel,
        grid=(rows // tm,),
        in_specs=[
            col(4), col(5), col(6),
            pl.BlockSpec((tm, LANES), lambda i: (i, 0)),
            pl.BlockSpec((tm, LANES), lambda i: (i, 0)),
            small((1, LANES)), small((1, LANES)), small((LANES, LANES)),
        ],
        out_specs=[
            pl.BlockSpec((2, tm, DIFF_WIDTH), lambda i: (0, i, 0)),
            pl.BlockSpec((tm, DIFF_WIDTH), lambda i: (i, 0)),
            pl.BlockSpec((tm, DIFF_WIDTH), lambda i: (i, 0)),
        ],
        out_shape=[
            jax.ShapeDtypeStruct((2, rows, DIFF_WIDTH), BF16),
            jax.ShapeDtypeStruct((rows, DIFF_WIDTH), BF16),
            jax.ShapeDtypeStruct((rows, DIFF_WIDTH), BF16),
        ],
        compiler_params=_params(("parallel",)),
        name="attn_prep",
    )(proj, proj, proj, cos, sin, qgain, kgain, gsum)


ATTN_BLOCK = 512


def _diff_attn_kernel(q_ref, k_ref, v_ref, lam_ref, gain_ref, o_ref, m_ref, l_ref, acc_ref, *, seq):
    i = pl.program_id(1)
    bq = q_ref.shape[1]
    q = q_ref[...].reshape(2 * bq, DIFF_VDIM)

    s = _dot_nt(q, k_ref[seq:seq + META_BLOCK, :])
    col = lax.broadcasted_iota(jnp.int32, s.shape, 1)
    s = jnp.where(col >= META_PAD, s, MASK_VALUE)
    m = jnp.max(s, axis=-1, keepdims=True)
    p = jnp.exp(s - m)
    m_ref[...] = m
    l_ref[...] = jnp.sum(p, axis=-1, keepdims=True)
    acc_ref[...] = _dot(p.astype(BF16), v_ref[seq:seq + META_BLOCK, :])

    def kv_step(j, causal):
        start = pl.multiple_of(j * bq, bq)
        s = _dot_nt(q, k_ref[pl.ds(start, bq), :])
        if causal:
            r = lax.broadcasted_iota(jnp.int32, s.shape, 0) % bq
            c = lax.broadcasted_iota(jnp.int32, s.shape, 1)
            s = jnp.where(c <= r, s, MASK_VALUE)
        m_prev = m_ref[...]
        m_new = jnp.maximum(m_prev, jnp.max(s, axis=-1, keepdims=True))
        alpha = jnp.exp(m_prev - m_new)
        p = jnp.exp(s - m_new)
        l_ref[...] = alpha * l_ref[...] + jnp.sum(p, axis=-1, keepdims=True)
        acc_ref[...] = alpha * acc_ref[...] + _dot(p.astype(BF16), v_ref[pl.ds(start, bq), :])
        m_ref[...] = m_new

    def body(j, carry):
        kv_step(j, False)
        return carry

    lax.fori_loop(0, i, body, 0)
    kv_step(i, True)

    lp = lam_ref[...]
    lam = (jnp.exp(jnp.sum(lp[0:1] * lp[1:2], axis=-1, keepdims=True))
           - jnp.exp(jnp.sum(lp[2:3] * lp[3:4], axis=-1, keepdims=True)) + LAMBDA_INIT)
    o = acc_ref[...] / l_ref[...]
    o = o[:bq] - lam * o[bq:]
    o = o * lax.rsqrt(jnp.mean(o * o, axis=-1, keepdims=True) + NORM_EPS) * gain_ref[...]
    o_ref[...] = (o * (1.0 - LAMBDA_INIT)).astype(BF16)


def _diff_attn(q2, kr, vb, lam_params, gain, seq):
    rows = kr.shape[0]
    bq = _pick(seq, (ATTN_BLOCK, 128))
    return pl.pallas_call(
        functools.partial(_diff_attn_kernel, seq=seq),
        grid=(DIFF_HEADS, seq // bq),
        in_specs=[
            pl.BlockSpec((2, bq, DIFF_VDIM), lambda h, i: (0, i, h)),
            pl.BlockSpec((rows, DIFF_VDIM), lambda h, i: (0, h)),
            pl.BlockSpec((rows, DIFF_VDIM), lambda h, i: (0, h)),
            pl.BlockSpec((4, DIFF_DIM), lambda h, i: (0, 0)),
            pl.BlockSpec((1, DIFF_VDIM), lambda h, i: (0, 0)),
        ],
        out_specs=pl.BlockSpec((bq, DIFF_VDIM), lambda h, i: (i, h)),
        out_shape=jax.ShapeDtypeStruct((seq, DIFF_WIDTH), BF16),
        scratch_shapes=[
            pltpu.VMEM((2 * bq, 1), F32),
            pltpu.VMEM((2 * bq, 1), F32),
            pltpu.VMEM((2 * bq, DIFF_VDIM), F32),
        ],
        compiler_params=_params(("parallel", "arbitrary")),
        name="diff_attn",
    )(q2, kr, vb, lam_params, gain)


def _out_proj_kernel(mg_ref, md_ref, wg_ref, wd_ref, h_ref, gain_ref, h2_ref, n2_ref):
    h2 = h_ref[...] + _dot(mg_ref[...], wg_ref[...]) + _dot(md_ref[...], wd_ref[...])
    h2_ref[...] = h2
    ms = jnp.mean(h2 * h2, axis=-1, keepdims=True)
    n2_ref[...] = (h2 * lax.rsqrt(ms + NORM_EPS) * gain_ref[...]).astype(BF16)


def _out_proj(mix_g, mix_d, w_out, h, gain, seq):
    tm = _pick(seq, (512, 128))
    return pl.pallas_call(
        _out_proj_kernel,
        grid=(seq // tm,),
        in_specs=[
            pl.BlockSpec((tm, GDN_WIDTH), lambda i: (i, 0)),
            pl.BlockSpec((tm, DIFF_WIDTH), lambda i: (i, 0)),
            pl.BlockSpec((GDN_WIDTH, D_MODEL), lambda i: (0, 0)),
            pl.BlockSpec((DIFF_WIDTH, D_MODEL), lambda i: (1, 0)),
            pl.BlockSpec((tm, D_MODEL), lambda i: (i, 0)),
            pl.BlockSpec((1, D_MODEL), lambda i: (0, 0)),
        ],
        out_specs=[
            pl.BlockSpec((tm, D_MODEL), lambda i: (i, 0)),
            pl.BlockSpec((tm, D_MODEL), lambda i: (i, 0)),
        ],
        out_shape=[
            jax.ShapeDtypeStruct((seq, D_MODEL), F32),
            jax.ShapeDtypeStruct((seq, D_MODEL), BF16),
        ],
        compiler_params=_params(("parallel",)),
        name="out_proj",
    )(mix_g, mix_d, w_out, w_out, h, gain)


def _gate_up_kernel(n_ref, wg_ref, wu_ref, a_ref):
    n = n_ref[...]
    g = _dot(n, wg_ref[...])
    u = _dot(n, wu_ref[...])
    a_ref[...] = (_silu(g) * u).astype(BF16)


def _gate_up(n2, w_gu):
    seq = n2.shape[0]
    tm = _pick(seq, (1024, 128))
    tn = 512
    nt = D_FF // tn
    return pl.pallas_call(
        _gate_up_kernel,
        grid=(nt, seq // tm),
        in_specs=[
            pl.BlockSpec((tm, D_MODEL), lambda j, i: (i, 0)),
            pl.BlockSpec((D_MODEL, tn), lambda j, i: (0, j)),
            pl.BlockSpec((D_MODEL, tn), lambda j, i: (0, j + nt)),
        ],
        out_specs=pl.BlockSpec((tm, tn), lambda j, i: (i, j)),
        out_shape=jax.ShapeDtypeStruct((seq, D_FF), BF16),
        compiler_params=_params(("parallel", "parallel")),
        name="ffn_gate_up",
    )(n2, w_gu, w_gu)


def _down_kernel(a_ref, w_ref, h_ref, o_ref):
    o_ref[...] = h_ref[...] + _dot(a_ref[...], w_ref[...])


def _down(act, w_down, h2):
    seq = act.shape[0]
    tm = _pick(seq, (512, 128))
    tn = 1024
    return pl.pallas_call(
        _down_kernel,
        grid=(D_MODEL // tn, seq // tm),
        in_specs=[
            pl.BlockSpec((tm, D_FF), lambda j, i: (i, 0)),
            pl.BlockSpec((D_FF, tn), lambda j, i: (0, j)),
            pl.BlockSpec((tm, tn), lambda j, i: (i, j)),
        ],
        out_specs=pl.BlockSpec((tm, tn), lambda j, i: (i, j)),
        out_shape=jax.ShapeDtypeStruct((seq, D_MODEL), F32),
        compiler_params=_params(("parallel", "parallel")),
        name="ffn_down",
    )(act, w_down, h2)


def _rope_tables(seq):
    half = DIFF_DIM // 2
    pos = jnp.concatenate([jnp.arange(seq) + N_META, jnp.zeros((META_PAD,), jnp.int32),
                           jnp.arange(N_META)]).astype(F32)
    inv_freq = ROPE_THETA ** (-jnp.arange(half, dtype=F32) / half)
    ang = pos[:, None] * inv_freq[None, :]
    cos = jnp.tile(jnp.cos(ang), (1, LANES // half))
    sin = jnp.sin(ang)
    sin = jnp.tile(jnp.concatenate([-sin, sin], axis=1), (1, LANES // DIFF_DIM))
    return cos, sin


def _lane_pad(v, offset):
    return jnp.zeros((1, GATE_LANES), F32).at[0, offset:offset + v.shape[0]].set(v.astype(F32))


def kernel(x, meta_tokens, attn_norm, w_in, conv_w, a_log, dt_bias, gdn_norm, q_norm, k_norm,
           lambda_q1, lambda_k1, lambda_q2, lambda_k2, diff_norm, w_out, ffn_norm, w_gate_up, w_down):
    assert x.shape[0] == 1 and x.shape[2] == D_MODEL
    seq = x.shape[1]
    xs = x[0]
    h = jnp.concatenate([xs, jnp.zeros((META_PAD, D_MODEL), xs.dtype), meta_tokens.astype(xs.dtype)], axis=0)

    gdn_cols = 4 * GDN_WIDTH
    w_in0 = w_in[0]
    w_main = jnp.concatenate([w_in0[:, :gdn_cols], w_in0[:, gdn_cols + 2 * GDN_HEADS:]], axis=1).astype(BF16)
    w_ba = jnp.pad(w_in0[:, gdn_cols:gdn_cols + 2 * GDN_HEADS],
                   ((0, 0), (0, GATE_LANES - 2 * GDN_HEADS))).astype(BF16)

    n1, gcol, grow = _prenorm_gate(h, attn_norm, w_ba, _lane_pad(a_log[0], GDN_HEADS),
                                   _lane_pad(dt_bias[0], GDN_HEADS))
    tm = _pick(h.shape[0], (640, 128))
    proj = _matmul(n1, w_main, F32, tm, 1024, "in_proj")

    w, u, qg, kd, attn, egl = _gdn_local(proj, conv_w[0], gcol, grow)
    mix_g = _gdn_state(w, u, qg, kd, attn, egl, proj, gdn_norm)

    cos, sin = _rope_tables(seq)
    tile2 = lambda g: jnp.tile(g.astype(F32), (1, LANES // DIFF_DIM))
    lane = np.arange(LANES)
    gsum = jnp.asarray((lane[:, None] // DIFF_DIM) == (lane[None, :] // DIFF_DIM), BF16)
    q2, kr, vb = _attn_prep(proj, cos, sin, tile2(q_norm), tile2(k_norm), gsum)
    lam_params = jnp.concatenate([lambda_q1, lambda_k1, lambda_q2, lambda_k2], axis=0).astype(F32)
    mix_d = _diff_attn(q2, kr, vb, lam_params, diff_norm, seq)

    h2, n2 = _out_proj(mix_g, mix_d, w_out[0].astype(BF16), h, ffn_norm, seq)
    act = _gate_up(n2, w_gate_up[0].astype(BF16))
    out = _down(act, w_down[0].astype(BF16), h2)
    return out[None]
```

```python
import functools
import math

import jax
import jax.numpy as jnp
import numpy as np
from jax import lax
from jax.experimental import pallas as pl
from jax.experimental.pallas import tpu as pltpu

F32 = jnp.float32
BF16 = jnp.bfloat16

D_MODEL = 2048
N_META = 16
GDN_HEADS = 8
GDN_DIM = 128
GDN_WIDTH = GDN_HEADS * GDN_DIM
CONV_WIDTH = 4
CHUNK = 64
DIFF_HEADS = 8
DIFF_DIM = 64
DIFF_VDIM = 2 * DIFF_DIM
DIFF_WIDTH = DIFF_HEADS * DIFF_VDIM
ROPE_THETA = 10000.0
D_FF = 5632
NORM_EPS = 1e-6
MASK_VALUE = -1e30
LAMBDA_INIT = 0.8 - 0.6 * math.exp(-0.3 * 0)

LANES = 128
META_BLOCK = 128
META_PAD = META_BLOCK - N_META
GATE_LANES = 128
VMEM_LIMIT = 56 * 1024 * 1024


def _pick(n, candidates):
    for c in candidates:
        if n % c == 0:
            return c
    raise ValueError(f"no tile in {candidates} divides {n}")


def _params(sem, vmem=VMEM_LIMIT):
    return pltpu.CompilerParams(dimension_semantics=sem, vmem_limit_bytes=vmem)


def _dot(a, b):
    return jnp.dot(a, b, preferred_element_type=F32)


def _dot_nt(a, b):
    return lax.dot_general(a, b, (((1,), (1,)), ((), ())), preferred_element_type=F32)


def _dot_tn(a, b):
    return lax.dot_general(a, b, (((0,), (0,)), ((), ())), preferred_element_type=F32)


def _softplus(x):
    return jnp.maximum(x, 0.0) + jnp.log1p(jnp.exp(-jnp.abs(x)))


def _silu(x):
    return x * jax.nn.sigmoid(x)


def _prenorm_gate_kernel(h_ref, gain_ref, wba_ref, alog_ref, dtb_ref, n_ref, gcol_ref, grow_ref):
    h = h_ref[...]
    ms = jnp.mean(h * h, axis=-1, keepdims=True)
    n = (h * lax.rsqrt(ms + NORM_EPS) * gain_ref[...]).astype(BF16)
    n_ref[...] = n
    ba = _dot(n, wba_ref[...])
    beta = jax.nn.sigmoid(ba)
    g = -jnp.exp(alog_ref[...]) * _softplus(ba + dtb_ref[...])
    row = lax.broadcasted_iota(jnp.int32, ba.shape, 0) % CHUNK
    gc = g
    for d in (1, 2, 4, 8, 16, 32):
        gc = gc + jnp.where(row >= d, pltpu.roll(gc, d, axis=0), 0.0)
    lane = lax.broadcasted_iota(jnp.int32, ba.shape, 1)
    out = jnp.where(lane < GDN_HEADS, beta, gc)
    gcol_ref[...] = out
    grow_ref[...] = out.T[: 2 * GDN_HEADS]


def _prenorm_gate(h, gain, wba, alog, dtb):
    rows = h.shape[0]
    tm = _pick(rows, (640, 128))
    return pl.pallas_call(
        _prenorm_gate_kernel,
        grid=(rows // tm,),
        in_specs=[
            pl.BlockSpec((tm, D_MODEL), lambda i: (i, 0)),
            pl.BlockSpec((1, D_MODEL), lambda i: (0, 0)),
            pl.BlockSpec((D_MODEL, GATE_LANES), lambda i: (0, 0)),
            pl.BlockSpec((1, GATE_LANES), lambda i: (0, 0)),
            pl.BlockSpec((1, GATE_LANES), lambda i: (0, 0)),
        ],
        out_specs=[
            pl.BlockSpec((tm, D_MODEL), lambda i: (i, 0)),
            pl.BlockSpec((tm, GATE_LANES), lambda i: (i, 0)),
            pl.BlockSpec((2 * GDN_HEADS, tm), lambda i: (0, i)),
        ],
        out_shape=[
            jax.ShapeDtypeStruct((rows, D_MODEL), BF16),
            jax.ShapeDtypeStruct((rows, GATE_LANES), F32),
            jax.ShapeDtypeStruct((2 * GDN_HEADS, rows), F32),
        ],
        compiler_params=_params(("parallel",)),
        name="prenorm_gate",
    )(h, gain, wba, alog, dtb)


def _matmul_kernel(a_ref, b_ref, o_ref):
    o_ref[...] = _dot(a_ref[...], b_ref[...]).astype(o_ref.dtype)


def _matmul(a, b, out_dtype, tm, tn, name):
    m, k = a.shape
    n = b.shape[1]
    return pl.pallas_call(
        _matmul_kernel,
        grid=(n // tn, m // tm),
        in_specs=[
            pl.BlockSpec((tm, k), lambda j, i: (i, 0)),
            pl.BlockSpec((k, tn), lambda j, i: (0, j)),
        ],
        out_specs=pl.BlockSpec((tm, tn), lambda j, i: (i, j)),
        out_shape=jax.ShapeDtypeStruct((m, n), out_dtype),
        compiler_params=_params(("parallel", "parallel")),
        name=name,
    )(a, b)


GDN_LOCAL_CHUNKS = 2


def _causal_conv_silu(x, prev, w):
    r8 = lax.broadcasted_iota(jnp.int32, prev.shape, 0)
    y = x * w[CONV_WIDTH - 1:CONV_WIDTH]
    for d in range(1, CONV_WIDTH):
        shifted = pltpu.roll(x, d, axis=0)
        top = jnp.where(r8 < d, pltpu.roll(prev, d, axis=0), shifted[:8])
        shifted = jnp.concatenate([top, shifted[8:]], axis=0)
        y = y + shifted * w[CONV_WIDTH - 1 - d:CONV_WIDTH - d]
    return _silu(y)


def _gdn_local_kernel(q_ref, k_ref, v_ref, qp_ref, kp_ref, vp_ref, cw_ref, gcol_ref, grow_ref,
                      w_ref, u_ref, qg_ref, kd_ref, attn_ref, egl_ref):
    cw = cw_ref[...]
    q_all = _causal_conv_silu(q_ref[...], qp_ref[...], cw[:, 0:GDN_WIDTH])
    k_all = _causal_conv_silu(k_ref[...], kp_ref[...], cw[:, GDN_WIDTH:2 * GDN_WIDTH])
    v_all = _causal_conv_silu(v_ref[...], vp_ref[...], cw[:, 2 * GDN_WIDTH:3 * GDN_WIDTH])
    gcol = gcol_ref[...]
    grow = grow_ref[...]
    ii = lax.broadcasted_iota(jnp.int32, (CHUNK, CHUNK), 0)
    jj = lax.broadcasted_iota(jnp.int32, (CHUNK, CHUNK), 1)
    units = [(c, h) for c in range(GDN_LOCAL_CHUNKS) for h in range(GDN_HEADS)]
    rows_of = lambda c: slice(c * CHUNK, (c + 1) * CHUNK)
    lanes_of = lambda h: slice(h * GDN_DIM, (h + 1) * GDN_DIM)
    qs, ks, kbs, vbs, egcs, gcs, kqs = [], [], [], [], [], [], []
    for c, h in units:
        rs, ls = rows_of(c), lanes_of(h)
        q = q_all[rs, ls]
        k = k_all[rs, ls]
        q = q * lax.rsqrt(jnp.sum(q * q, axis=-1, keepdims=True) + NORM_EPS) * (GDN_DIM ** -0.5)
        k = k * lax.rsqrt(jnp.sum(k * k, axis=-1, keepdims=True) + NORM_EPS)
        beta_c = gcol[rs, h:h + 1]
        gc_c = gcol[rs, GDN_HEADS + h:GDN_HEADS + h + 1]
        kb = k * beta_c
        qs.append(q)
        ks.append(k)
        kbs.append(kb)
        vbs.append(v_all[rs, ls] * beta_c)
        gcs.append(gc_c)
        egcs.append(jnp.exp(gc_c))
        kqs.append(_dot_nt(jnp.concatenate([kb.astype(BF16), q.astype(BF16)], axis=0), k.astype(BF16)))
    lms, attns = [], []
    for (c, h), kq, gc_c in zip(units, kqs, gcs):
        gc_r = grow[GDN_HEADS + h:GDN_HEADS + h + 1, rows_of(c)]
        decay = jnp.exp(jnp.where(ii >= jj, gc_c - gc_r, MASK_VALUE))
        lms.append(jnp.where(ii > jj, kq[:CHUNK] * decay, 0.0))
        attns.append(kq[CHUNK:] * decay)
    xor = ii ^ jj
    eye = jnp.where(ii == jj, 1.0, 0.0)
    xs = [eye - jnp.where(xor == 1, lm, 0.0) for lm in lms]
    level = 1
    while (2 << level) <= CHUNK:
        sel = (xor >> level) == 1
        ys = [_dot(jnp.where(sel, lm, 0.0).astype(BF16), x.astype(BF16)) for lm, x in zip(lms, xs)]
        xs = [x - _dot(x.astype(BF16), y.astype(BF16)) for x, y in zip(xs, ys)]
        level += 1
    uws = [_dot(x.astype(BF16), jnp.concatenate([vb.astype(BF16), (kb * egc).astype(BF16)], axis=1))
           for x, vb, kb, egc in zip(xs, vbs, kbs, egcs)]
    for (c, h), uw, q, k, egc, gc_c, attn in zip(units, uws, qs, ks, egcs, gcs, attns):
        rs, ls = rows_of(c), lanes_of(h)
        u_ref[rs, ls] = uw[:, :GDN_DIM]
        w_ref[rs, ls] = uw[:, GDN_DIM:].astype(BF16)
        qg_ref[rs, ls] = (q * egc).astype(BF16)
        gc_last = gc_c[CHUNK - 1:CHUNK]
        kd_ref[rs, ls] = (k * jnp.exp(gc_last - gc_c)).astype(BF16)
        attn_ref[h, rs, :] = attn.astype(BF16)
        egl_ref[c, h:h + 1, :] = jnp.broadcast_to(jnp.exp(gc_last), (1, GDN_DIM))


def _gdn_local(proj, conv_w, gcol, grow):
    rows = proj.shape[0]
    rb = GDN_LOCAL_CHUNKS * CHUNK
    nsteps = rows // rb
    n8 = rows // 8
    blk = lambda col: pl.BlockSpec((rb, GDN_WIDTH), lambda i, col=col: (i, col))
    prev = lambda col: pl.BlockSpec((8, GDN_WIDTH), lambda i, col=col: ((i * (rb // 8) + n8 - 1) % n8, col))
    row_out = lambda dt: jax.ShapeDtypeStruct((rows, GDN_WIDTH), dt)
    return pl.pallas_call(
        _gdn_local_kernel,
        grid=(nsteps,),
        in_specs=[
            blk(0), blk(1), blk(2), prev(0), prev(1), prev(2),
            pl.BlockSpec((CONV_WIDTH, 3 * GDN_WIDTH), lambda i: (0, 0)),
            pl.BlockSpec((rb, GATE_LANES), lambda i: (i, 0)),
            pl.BlockSpec((2 * GDN_HEADS, rb), lambda i: (0, i)),
        ],
        out_specs=[
            pl.BlockSpec((rb, GDN_WIDTH), lambda i: (i, 0)),
            pl.BlockSpec((rb, GDN_WIDTH), lambda i: (i, 0)),
            pl.BlockSpec((rb, GDN_WIDTH), lambda i: (i, 0)),
            pl.BlockSpec((rb, GDN_WIDTH), lambda i: (i, 0)),
            pl.BlockSpec((GDN_HEADS, rb, CHUNK), lambda i: (0, i, 0)),
            pl.BlockSpec((GDN_LOCAL_CHUNKS, GDN_HEADS, GDN_DIM), lambda i: (i, 0, 0)),
        ],
        out_shape=[
            row_out(BF16),
            row_out(F32),
            row_out(BF16),
            row_out(BF16),
            jax.ShapeDtypeStruct((GDN_HEADS, rows, CHUNK), BF16),
            jax.ShapeDtypeStruct((rows // CHUNK, GDN_HEADS, GDN_DIM), F32),
        ],
        compiler_params=_params(("parallel",)),
        name="gdn_local",
    )(proj, proj, proj, proj, proj, proj, conv_w, gcol, grow)


def _gdn_state_kernel(w_ref, u_ref, qg_ref, kd_ref, attn_ref, egl_ref, z_ref, gain_ref, o_ref, s_ref):
    @pl.when(pl.program_id(0) == 0)
    def _():
        s_ref[...] = jnp.zeros_like(s_ref)

    gain = gain_ref[...]
    heads = range(GDN_HEADS)
    lanes = [slice(h * GDN_DIM, (h + 1) * GDN_DIM) for h in heads]
    ss = [s_ref[h] for h in heads]
    rs = [_dot(jnp.concatenate([w_ref[:, ls], qg_ref[:, ls]], axis=0), s.astype(BF16))
          for ls, s in zip(lanes, ss)]
    vns = [(u_ref[:, ls] - r[:CHUNK]).astype(BF16) for ls, r in zip(lanes, rs)]
    os_ = [r[CHUNK:] + _dot(attn_ref[h], vn) for h, r, vn in zip(heads, rs, vns)]
    for h, ls, s, vn in zip(heads, lanes, ss, vns):
        s_ref[h] = s * egl_ref[0, h:h + 1, :] + _dot_tn(kd_ref[:, ls], vn)
    for ls, o in zip(lanes, os_):
        on = o * lax.rsqrt(jnp.mean(o * o, axis=-1, keepdims=True) + NORM_EPS) * gain
        o_ref[:, ls] = (on * _silu(z_ref[:, ls])).astype(BF16)


def _gdn_state(w, u, qg, kd, attn, egl, proj, gain):
    rows = w.shape[0]
    nchunks = rows // CHUNK
    nseq = nchunks - 1
    phys = lambda c: (c + nchunks - 1) % nchunks
    rowblk = lambda col: pl.BlockSpec((CHUNK, GDN_WIDTH), lambda c, col=col: (phys(c), col))
    return pl.pallas_call(
        _gdn_state_kernel,
        grid=(nseq,),
        in_specs=[
            rowblk(0), rowblk(0), rowblk(0), rowblk(0),
            pl.BlockSpec((GDN_HEADS, CHUNK, CHUNK), lambda c: (0, phys(c), 0)),
            pl.BlockSpec((1, GDN_HEADS, GDN_DIM), lambda c: (phys(c), 0, 0)),
            rowblk(3),
            pl.BlockSpec((1, GDN_DIM), lambda c: (0, 0)),
        ],
        out_specs=pl.BlockSpec((CHUNK, GDN_WIDTH), lambda c: (jnp.maximum(c - 1, 0), 0)),
        out_shape=jax.ShapeDtypeStruct((rows - META_BLOCK, GDN_WIDTH), BF16),
        scratch_shapes=[pltpu.VMEM((GDN_HEADS, GDN_DIM, GDN_DIM), F32)],
        compiler_params=_params(("arbitrary",)),
        name="gdn_state",
    )(w, u, qg, kd, attn, egl, proj, gain)


KV_TILE = 128
Q_SCALE = DIFF_DIM ** -0.5 * math.log2(math.e)


def _attn_prep_kernel(q_ref, k_ref, v_ref, cos_ref, sin_ref, qg_ref, kg_ref, gsum_ref,
                      q2_ref, kr_ref, vt_ref):
    cos = cos_ref[...]
    sin = sin_ref[...]
    gsum = gsum_ref[...]
    lane = lax.broadcasted_iota(jnp.int32, cos.shape, 1)
    first_half = (lane % DIFF_DIM) < (DIFF_DIM // 2)
    low_map = lane < DIFF_DIM

    def norm_rope(x, gain):
        ms = _dot((x * x).astype(BF16), gsum) * (1.0 / DIFF_DIM)
        xn = x * lax.rsqrt(ms + NORM_EPS) * gain
        rot = jnp.where(first_half, pltpu.roll(xn, LANES - DIFF_DIM // 2, axis=1),
                        pltpu.roll(xn, DIFF_DIM // 2, axis=1))
        return xn * cos + rot * sin

    for h in range(DIFF_HEADS):
        ls = slice(h * DIFF_VDIM, (h + 1) * DIFF_VDIM)
        q = norm_rope(q_ref[:, ls], qg_ref[...]) * Q_SCALE
        q2_ref[0, :, ls] = jnp.where(low_map, q, 0.0).astype(BF16)
        q2_ref[1, :, ls] = jnp.where(low_map, 0.0, q).astype(BF16)
        kr_ref[:, ls] = norm_rope(k_ref[:, ls], kg_ref[...]).astype(BF16)
        vt_ref[0, ls, :] = v_ref[:, ls].T.astype(BF16)


def _attn_prep(proj, cos, sin, qgain, kgain, gsum):
    rows = proj.shape[0]
    tm = KV_TILE
    col = lambda c: pl.BlockSpec((tm, DIFF_WIDTH), lambda i, c=c: (i, c))
    small = lambda shape: pl.BlockSpec(shape, lambda i: (0,) * len(shape))
    return pl.pallas_call(
        _attn_prep_kernel,
        grid=(rows // tm,),
        in_specs=[
            col(4), col(5), col(6),
            pl.BlockSpec((tm, LANES), lambda i: (i, 0)),
            pl.BlockSpec((tm, LANES), lambda i: (i, 0)),
            small((1, LANES)), small((1, LANES)), small((LANES, LANES)),
        ],
        out_specs=[
            pl.BlockSpec((2, tm, DIFF_WIDTH), lambda i: (0, i, 0)),
            pl.BlockSpec((tm, DIFF_WIDTH), lambda i: (i, 0)),
            pl.BlockSpec((1, DIFF_WIDTH, tm), lambda i: (i, 0, 0)),
        ],
        out_shape=[
            jax.ShapeDtypeStruct((2, rows, DIFF_WIDTH), BF16),
            jax.ShapeDtypeStruct((rows, DIFF_WIDTH), BF16),
            jax.ShapeDtypeStruct((rows // tm, DIFF_WIDTH, tm), BF16),
        ],
        compiler_params=_params(("parallel",)),
        name="attn_prep",
    )(proj, proj, proj, cos, sin, qgain, kgain, gsum)


ATTN_BLOCK = 512
ATTN_QSUB = 256


def _diff_attn_kernel(q_ref, k_ref, vt_ref, lam_ref, gain_ref, o_ref, acc_ref, *, seq):
    i = pl.program_id(1)
    bq = q_ref.shape[1]
    nsub = bq // ATTN_QSUB
    chains = [(mp, sb) for mp in range(2) for sb in range(nsub)]
    qs = [q_ref[mp, sb * ATTN_QSUB:(sb + 1) * ATTN_QSUB, :] for mp, sb in chains]
    kv_tiles = bq // KV_TILE

    def update(sts, vt, ms, ls):
        cms = [jnp.max(st, axis=0, keepdims=True) for st in sts]
        if ms is None:
            m_new = cms
        else:
            m_new = [jnp.maximum(m, cm) for m, cm in zip(ms, cms)]
        ps = [jnp.exp2(st - mn) for st, mn in zip(sts, m_new)]
        pvs = [_dot(vt, p.astype(BF16)) for p in ps]
        psums = [jnp.sum(p, axis=0, keepdims=True) for p in ps]
        if ms is None:
            l_new = psums
            for c, pv in enumerate(pvs):
                acc_ref[c] = pv
        else:
            alphas = [jnp.exp2(m - mn) for m, mn in zip(ms, m_new)]
            l_new = [a * l + ps_ for a, l, ps_ in zip(alphas, ls, psums)]
            for c, (a, pv) in enumerate(zip(alphas, pvs)):
                acc_ref[c] = a * acc_ref[c] + pv
        return m_new, l_new

    k_meta = k_ref[seq:seq + META_BLOCK, :]
    key = lax.broadcasted_iota(jnp.int32, (META_BLOCK, ATTN_QSUB), 0)
    sts = [jnp.where(key >= META_PAD, _dot_nt(k_meta, q), MASK_VALUE) for q in qs]
    ms, ls = update(sts, vt_ref[seq // KV_TILE], None, None)

    def kv_step(j, ms, ls, causal):
        k_c = k_ref[pl.ds(pl.multiple_of(j * bq, bq), bq), :]
        vt = jnp.concatenate([vt_ref[j * kv_tiles + t] for t in range(kv_tiles)], axis=1)
        sts = [_dot_nt(k_c, q) for q in qs]
        if causal:
            key = lax.broadcasted_iota(jnp.int32, (bq, ATTN_QSUB), 0)
            qry = lax.broadcasted_iota(jnp.int32, (bq, ATTN_QSUB), 1)
            sts = [jnp.where(key <= qry + sb * ATTN_QSUB, st, MASK_VALUE)
                   for (mp, sb), st in zip(chains, sts)]
        return update(sts, vt, ms, ls)

    def body(j, carry):
        ms, ls = kv_step(j, list(carry[0]), list(carry[1]), False)
        return tuple(ms), tuple(ls)

    ms, ls = lax.fori_loop(0, i, body, (tuple(ms), tuple(ls)))
    ms, ls = kv_step(i, list(ms), list(ls), True)

    lp = lam_ref[...]
    lam = (jnp.exp(jnp.sum(lp[0:1] * lp[1:2], axis=-1, keepdims=True))
           - jnp.exp(jnp.sum(lp[2:3] * lp[3:4], axis=-1, keepdims=True)) + LAMBDA_INIT)
    gain = gain_ref[...]
    for sb in range(nsub):
        ot = acc_ref[sb] / ls[sb] - lam * (acc_ref[nsub + sb] / ls[nsub + sb])
        ot = ot * lax.rsqrt(jnp.mean(ot * ot, axis=0, keepdims=True) + NORM_EPS) * gain
        o_ref[sb * ATTN_QSUB:(sb + 1) * ATTN_QSUB, :] = (ot * (1.0 - LAMBDA_INIT)).T.astype(BF16)


def _diff_attn(q2, kr, vt, lam_params, gain_col, seq):
    rows = kr.shape[0]
    bq = ATTN_BLOCK
    assert seq % bq == 0
    return pl.pallas_call(
        functools.partial(_diff_attn_kernel, seq=seq),
        grid=(DIFF_HEADS, seq // bq),
        in_specs=[
            pl.BlockSpec((2, bq, DIFF_VDIM), lambda h, i: (0, i, h)),
            pl.BlockSpec((rows, DIFF_VDIM), lambda h, i: (0, h)),
            pl.BlockSpec((rows // KV_TILE, DIFF_VDIM, KV_TILE), lambda h, i: (0, h, 0)),
            pl.BlockSpec((4, DIFF_DIM), lambda h, i: (0, 0)),
            pl.BlockSpec((DIFF_VDIM, 1), lambda h, i: (0, 0)),
        ],
        out_specs=pl.BlockSpec((bq, DIFF_VDIM), lambda h, i: (i, h)),
        out_shape=jax.ShapeDtypeStruct((seq, DIFF_WIDTH), BF16),
        scratch_shapes=[pltpu.VMEM((2 * (bq // ATTN_QSUB), DIFF_VDIM, ATTN_QSUB), F32)],
        compiler_params=_params(("parallel", "arbitrary")),
        name="diff_attn",
    )(q2, kr, vt, lam_params, gain_col)


def _out_proj_kernel(mg_ref, md_ref, wg_ref, wd_ref, h_ref, gain_ref, h2_ref, n2_ref):
    h2 = h_ref[...] + _dot(mg_ref[...], wg_ref[...]) + _dot(md_ref[...], wd_ref[...])
    h2_ref[...] = h2
    ms = jnp.mean(h2 * h2, axis=-1, keepdims=True)
    n2_ref[...] = (h2 * lax.rsqrt(ms + NORM_EPS) * gain_ref[...]).astype(BF16)


def _out_proj(mix_g, mix_d, w_out, h, gain, seq):
    tm = _pick(seq, (512, 128))
    return pl.pallas_call(
        _out_proj_kernel,
        grid=(seq // tm,),
        in_specs=[
            pl.BlockSpec((tm, GDN_WIDTH), lambda i: (i, 0)),
            pl.BlockSpec((tm, DIFF_WIDTH), lambda i: (i, 0)),
            pl.BlockSpec((GDN_WIDTH, D_MODEL), lambda i: (0, 0)),
            pl.BlockSpec((DIFF_WIDTH, D_MODEL), lambda i: (1, 0)),
            pl.BlockSpec((tm, D_MODEL), lambda i: (i, 0)),
            pl.BlockSpec((1, D_MODEL), lambda i: (0, 0)),
        ],
        out_specs=[
            pl.BlockSpec((tm, D_MODEL), lambda i: (i, 0)),
            pl.BlockSpec((tm, D_MODEL), lambda i: (i, 0)),
        ],
        out_shape=[
            jax.ShapeDtypeStruct((seq, D_MODEL), F32),
            jax.ShapeDtypeStruct((seq, D_MODEL), BF16),
        ],
        compiler_params=_params(("parallel",)),
        name="out_proj",
    )(mix_g, mix_d, w_out, w_out, h, gain)


def _gate_up_kernel(n_ref, wg_ref, wu_ref, a_ref):
    n = n_ref[...]
    g = _dot(n, wg_ref[...])
    u = _dot(n, wu_ref[...])
    a_ref[...] = (_silu(g) * u).astype(BF16)


def _gate_up(n2, w_gu):
    seq = n2.shape[0]
    tm = _pick(seq, (1024, 128))
    tn = 512
    nt = D_FF // tn
    return pl.pallas_call(
        _gate_up_kernel,
        grid=(nt, seq // tm),
        in_specs=[
            pl.BlockSpec((tm, D_MODEL), lambda j, i: (i, 0)),
            pl.BlockSpec((D_MODEL, tn), lambda j, i: (0, j)),
            pl.BlockSpec((D_MODEL, tn), lambda j, i: (0, j + nt)),
        ],
        out_specs=pl.BlockSpec((tm, tn), lambda j, i: (i, j)),
        out_shape=jax.ShapeDtypeStruct((seq, D_FF), BF16),
        compiler_params=_params(("parallel", "parallel")),
        name="ffn_gate_up",
    )(n2, w_gu, w_gu)


def _down_kernel(a_ref, w_ref, h_ref, o_ref):
    o_ref[...] = h_ref[...] + _dot(a_ref[...], w_ref[...])


def _down(act, w_down, h2):
    seq = act.shape[0]
    tm = _pick(seq, (512, 128))
    tn = 1024
    return pl.pallas_call(
        _down_kernel,
        grid=(D_MODEL // tn, seq // tm),
        in_specs=[
            pl.BlockSpec((tm, D_FF), lambda j, i: (i, 0)),
            pl.BlockSpec((D_FF, tn), lambda j, i: (0, j)),
            pl.BlockSpec((tm, tn), lambda j, i: (i, j)),
        ],
        out_specs=pl.BlockSpec((tm, tn), lambda j, i: (i, j)),
        out_shape=jax.ShapeDtypeStruct((seq, D_MODEL), F32),
        compiler_params=_params(("parallel", "parallel")),
        name="ffn_down",
    )(act, w_down, h2)


def _rope_tables(seq):
    half = DIFF_DIM // 2
    pos = jnp.concatenate([jnp.arange(seq) + N_META, jnp.zeros((META_PAD,), jnp.int32),
                           jnp.arange(N_META)]).astype(F32)
    inv_freq = ROPE_THETA ** (-jnp.arange(half, dtype=F32) / half)
    ang = pos[:, None] * inv_freq[None, :]
    cos = jnp.tile(jnp.cos(ang), (1, LANES // half))
    sin = jnp.sin(ang)
    sin = jnp.tile(jnp.concatenate([-sin, sin], axis=1), (1, LANES // DIFF_DIM))
    return cos, sin


def _lane_pad(v, offset):
    return jnp.zeros((1, GATE_LANES), F32).at[0, offset:offset + v.shape[0]].set(v.astype(F32))


def kernel(x, meta_tokens, attn_norm, w_in, conv_w, a_log, dt_bias, gdn_norm, q_norm, k_norm,
           lambda_q1, lambda_k1, lambda_q2, lambda_k2, diff_norm, w_out, ffn_norm, w_gate_up, w_down):
    assert x.shape[0] == 1 and x.shape[2] == D_MODEL
    seq = x.shape[1]
    xs = x[0]
    h = jnp.concatenate([xs, jnp.zeros((META_PAD, D_MODEL), xs.dtype), meta_tokens.astype(xs.dtype)], axis=0)

    gdn_cols = 4 * GDN_WIDTH
    w_in0 = w_in[0]
    w_main = jnp.concatenate([w_in0[:, :gdn_cols], w_in0[:, gdn_cols + 2 * GDN_HEADS:]], axis=1).astype(BF16)
    w_ba = jnp.pad(w_in0[:, gdn_cols:gdn_cols + 2 * GDN_HEADS],
                   ((0, 0), (0, GATE_LANES - 2 * GDN_HEADS))).astype(BF16)

    n1, gcol, grow = _prenorm_gate(h, attn_norm, w_ba, _lane_pad(a_log[0], GDN_HEADS),
                                   _lane_pad(dt_bias[0], GDN_HEADS))
    tm = _pick(h.shape[0], (640, 128))
    proj = _matmul(n1, w_main, F32, tm, 1024, "in_proj")

    w, u, qg, kd, attn, egl = _gdn_local(proj, conv_w[0], gcol, grow)
    mix_g = _gdn_state(w, u, qg, kd, attn, egl, proj, gdn_norm)

    cos, sin = _rope_tables(seq)
    tile2 = lambda g: jnp.tile(g.astype(F32), (1, LANES // DIFF_DIM))
    lane = np.arange(LANES)
    gsum = jnp.asarray((lane[:, None] // DIFF_DIM) == (lane[None, :] // DIFF_DIM), BF16)
    q2, kr, vt = _attn_prep(proj, cos, sin, tile2(q_norm), tile2(k_norm), gsum)
    lam_params = jnp.concatenate([lambda_q1, lambda_k1, lambda_q2, lambda_k2], axis=0).astype(F32)
    mix_d = _diff_attn(q2, kr, vt, lam_params, diff_norm.astype(F32).reshape(DIFF_VDIM, 1), seq)

    h2, n2 = _out_proj(mix_g, mix_d, w_out[0].astype(BF16), h, ffn_norm, seq)
    act = _gate_up(n2, w_gate_up[0].astype(BF16))
    out = _down(act, w_down[0].astype(BF16), h2)
    return out[None]
```

```python
import functools
import math

import jax
import jax.numpy as jnp
import numpy as np
from jax import lax
from jax.experimental import pallas as pl
from jax.experimental.pallas import tpu as pltpu

F32 = jnp.float32
BF16 = jnp.bfloat16

D_MODEL = 2048
N_META = 16
GDN_HEADS = 8
GDN_DIM = 128
GDN_WIDTH = GDN_HEADS * GDN_DIM
CONV_WIDTH = 4
CHUNK = 64
DIFF_HEADS = 8
DIFF_DIM = 64
DIFF_VDIM = 2 * DIFF_DIM
DIFF_WIDTH = DIFF_HEADS * DIFF_VDIM
ROPE_THETA = 10000.0
D_FF = 5632
NORM_EPS = 1e-6
MASK_VALUE = -1e30
LAMBDA_INIT = 0.8 - 0.6 * math.exp(-0.3 * 0)

LANES = 128
META_BLOCK = 512
GATE_LANES = 128
VMEM_LIMIT = 56 * 1024 * 1024


def _pick(n, candidates):
    for c in candidates:
        if n % c == 0:
            return c
    raise ValueError(f"no tile in {candidates} divides {n}")


def _params(sem, vmem=VMEM_LIMIT):
    return pltpu.CompilerParams(dimension_semantics=sem, vmem_limit_bytes=vmem)


def _dot(a, b):
    return jnp.dot(a, b, preferred_element_type=F32)


def _dot_nt(a, b):
    return lax.dot_general(a, b, (((1,), (1,)), ((), ())), preferred_element_type=F32)


def _dot_tn(a, b):
    return lax.dot_general(a, b, (((0,), (0,)), ((), ())), preferred_element_type=F32)


def _softplus(x):
    return jnp.maximum(x, 0.0) + jnp.log1p(jnp.exp(-jnp.abs(x)))


def _silu(x):
    return x * jax.nn.sigmoid(x)


def _prenorm_gate_kernel(x_ref, mb_ref, gain_ref, wba_ref, alog_ref, dtb_ref, n_ref, gcol_ref, grow_ref):
    h = jnp.where(pl.program_id(0) < pl.num_programs(0) - 1, x_ref[...], mb_ref[...])
    ms = jnp.mean(h * h, axis=-1, keepdims=True)
    n = (h * lax.rsqrt(ms + NORM_EPS) * gain_ref[...]).astype(BF16)
    n_ref[...] = n
    ba = _dot(n, wba_ref[...])
    beta = jax.nn.sigmoid(ba)
    g = -jnp.exp(alog_ref[...]) * _softplus(ba + dtb_ref[...])
    row = lax.broadcasted_iota(jnp.int32, ba.shape, 0) % CHUNK
    gc = g
    for d in (1, 2, 4, 8, 16, 32):
        gc = gc + jnp.where(row >= d, pltpu.roll(gc, d, axis=0), 0.0)
    lane = lax.broadcasted_iota(jnp.int32, ba.shape, 1)
    out = jnp.where(lane < GDN_HEADS, beta, gc)
    gcol_ref[...] = out
    grow_ref[...] = out.T[: 2 * GDN_HEADS]


def _prenorm_gate(x, meta_block, gain, wba, alog, dtb):
    tm = META_BLOCK
    nx = x.shape[0] // tm
    rows = x.shape[0] + tm
    return pl.pallas_call(
        _prenorm_gate_kernel,
        grid=(nx + 1,),
        in_specs=[
            pl.BlockSpec((tm, D_MODEL), lambda i: (jnp.minimum(i, nx - 1), 0)),
            pl.BlockSpec((tm, D_MODEL), lambda i: (0, 0)),
            pl.BlockSpec((1, D_MODEL), lambda i: (0, 0)),
            pl.BlockSpec((D_MODEL, GATE_LANES), lambda i: (0, 0)),
            pl.BlockSpec((1, GATE_LANES), lambda i: (0, 0)),
            pl.BlockSpec((1, GATE_LANES), lambda i: (0, 0)),
        ],
        out_specs=[
            pl.BlockSpec((tm, D_MODEL), lambda i: (i, 0)),
            pl.BlockSpec((tm, GATE_LANES), lambda i: (i, 0)),
            pl.BlockSpec((2 * GDN_HEADS, tm), lambda i: (0, i)),
        ],
        out_shape=[
            jax.ShapeDtypeStruct((rows, D_MODEL), BF16),
            jax.ShapeDtypeStruct((rows, GATE_LANES), F32),
            jax.ShapeDtypeStruct((2 * GDN_HEADS, rows), F32),
        ],
        compiler_params=_params(("parallel",)),
        name="prenorm_gate",
    )(x, meta_block, gain, wba, alog, dtb)


IN_PROJ_TN = 1024
GDN_COL_TILES = 4 * GDN_WIDTH // IN_PROJ_TN


def _in_proj_kernel(a_ref, wg_ref, wd_ref, o_ref):
    j = pl.program_id(0)

    @pl.when(j < GDN_COL_TILES)
    def _():
        o_ref[...] = _dot(a_ref[...], wg_ref[...])

    @pl.when(j >= GDN_COL_TILES)
    def _():
        o_ref[...] = _dot(a_ref[...], wd_ref[...])


def _in_proj(n1, w_all, w_diff):
    m = n1.shape[0]
    tm, tn = META_BLOCK, IN_PROJ_TN
    n = GDN_COL_TILES * tn + w_diff.shape[1]
    return pl.pallas_call(
        _in_proj_kernel,
        grid=(n // tn, m // tm),
        in_specs=[
            pl.BlockSpec((tm, D_MODEL), lambda j, i: (i, 0)),
            pl.BlockSpec((D_MODEL, tn), lambda j, i: (0, jnp.minimum(j, GDN_COL_TILES - 1))),
            pl.BlockSpec((D_MODEL, tn), lambda j, i: (0, jnp.maximum(j - GDN_COL_TILES, 0))),
        ],
        out_specs=pl.BlockSpec((tm, tn), lambda j, i: (i, j)),
        out_shape=jax.ShapeDtypeStruct((m, n), F32),
        compiler_params=_params(("parallel", "parallel")),
        name="in_proj",
    )(n1, w_all, w_diff)


GDN_LOCAL_CHUNKS = 2


def _causal_conv_silu(x, prev, w):
    r8 = lax.broadcasted_iota(jnp.int32, prev.shape, 0)
    y = x * w[CONV_WIDTH - 1:CONV_WIDTH]
    for d in range(1, CONV_WIDTH):
        shifted = pltpu.roll(x, d, axis=0)
        top = jnp.where(r8 < d, pltpu.roll(prev, d, axis=0), shifted[:8])
        shifted = jnp.concatenate([top, shifted[8:]], axis=0)
        y = y + shifted * w[CONV_WIDTH - 1 - d:CONV_WIDTH - d]
    return _silu(y)


def _gdn_local_kernel(q_ref, k_ref, v_ref, qp_ref, kp_ref, vp_ref, cw_ref, gcol_ref, grow_ref,
                      w_ref, u_ref, qg_ref, kd_ref, attn_ref, egl_ref):
    cw = cw_ref[...]
    q_all = _causal_conv_silu(q_ref[...], qp_ref[...], cw[:, 0:GDN_WIDTH])
    k_all = _causal_conv_silu(k_ref[...], kp_ref[...], cw[:, GDN_WIDTH:2 * GDN_WIDTH])
    v_all = _causal_conv_silu(v_ref[...], vp_ref[...], cw[:, 2 * GDN_WIDTH:3 * GDN_WIDTH])
    gcol = gcol_ref[...]
    grow = grow_ref[...]
    ii = lax.broadcasted_iota(jnp.int32, (CHUNK, CHUNK), 0)
    jj = lax.broadcasted_iota(jnp.int32, (CHUNK, CHUNK), 1)
    units = [(c, h) for c in range(GDN_LOCAL_CHUNKS) for h in range(GDN_HEADS)]
    rows_of = lambda c: slice(c * CHUNK, (c + 1) * CHUNK)
    lanes_of = lambda h: slice(h * GDN_DIM, (h + 1) * GDN_DIM)
    qs, ks, kbs, vbs, egcs, gcs, kqs = [], [], [], [], [], [], []
    for c, h in units:
        rs, ls = rows_of(c), lanes_of(h)
        q = q_all[rs, ls]
        k = k_all[rs, ls]
        q = q * lax.rsqrt(jnp.sum(q * q, axis=-1, keepdims=True) + NORM_EPS) * (GDN_DIM ** -0.5)
        k = k * lax.rsqrt(jnp.sum(k * k, axis=-1, keepdims=True) + NORM_EPS)
        beta_c = gcol[rs, h:h + 1]
        gc_c = gcol[rs, GDN_HEADS + h:GDN_HEADS + h + 1]
        kb = k * beta_c
        qs.append(q)
        ks.append(k)
        kbs.append(kb)
        vbs.append(v_all[rs, ls] * beta_c)
        gcs.append(gc_c)
        egcs.append(jnp.exp(gc_c))
        kqs.append(_dot_nt(jnp.concatenate([kb.astype(BF16), q.astype(BF16)], axis=0), k.astype(BF16)))
    lms, attns = [], []
    for (c, h), kq, gc_c in zip(units, kqs, gcs):
        gc_r = grow[GDN_HEADS + h:GDN_HEADS + h + 1, rows_of(c)]
        decay = jnp.exp(jnp.where(ii >= jj, gc_c - gc_r, MASK_VALUE))
        lms.append(jnp.where(ii > jj, kq[:CHUNK] * decay, 0.0))
        attns.append(kq[CHUNK:] * decay)
    xor = ii ^ jj
    eye = jnp.where(ii == jj, 1.0, 0.0)
    xs = [eye - jnp.where(xor == 1, lm, 0.0) for lm in lms]
    level = 1
    while (2 << level) <= CHUNK:
        sel = (xor >> level) == 1
        ys = [_dot(jnp.where(sel, lm, 0.0).astype(BF16), x.astype(BF16)) for lm, x in zip(lms, xs)]
        xs = [x - _dot(x.astype(BF16), y.astype(BF16)) for x, y in zip(xs, ys)]
        level += 1
    uws = [_dot(x.astype(BF16), jnp.concatenate([vb.astype(BF16), (kb * egc).astype(BF16)], axis=1))
           for x, vb, kb, egc in zip(xs, vbs, kbs, egcs)]
    for (c, h), uw, q, k, egc, gc_c, attn in zip(units, uws, qs, ks, egcs, gcs, attns):
        rs, ls = rows_of(c), lanes_of(h)
        u_ref[rs, ls] = uw[:, :GDN_DIM]
        w_ref[rs, ls] = uw[:, GDN_DIM:].astype(BF16)
        qg_ref[rs, ls] = (q * egc).astype(BF16)
        gc_last = gc_c[CHUNK - 1:CHUNK]
        kd_ref[rs, ls] = (k * jnp.exp(gc_last - gc_c)).astype(BF16)
        attn_ref[h, rs, :] = attn.astype(BF16)
        egl_ref[c, h:h + 1, :] = jnp.broadcast_to(jnp.exp(gc_last), (1, GDN_DIM))


def _tail_block_index(n_x_blocks, n_blocks):
    return lambda i: jnp.where(i < n_x_blocks, i, n_blocks - 1)


def _gdn_local(proj, conv_w, gcol, grow, seq):
    rows = proj.shape[0]
    rb = GDN_LOCAL_CHUNKS * CHUNK
    bidx = _tail_block_index(seq // rb, rows // rb)
    n8 = rows // 8
    blk = lambda col: pl.BlockSpec((rb, GDN_WIDTH), lambda i, col=col: (bidx(i), col))
    prev = lambda col: pl.BlockSpec((8, GDN_WIDTH),
                                    lambda i, col=col: ((bidx(i) * (rb // 8) + n8 - 1) % n8, col))
    row_out = lambda dt: jax.ShapeDtypeStruct((rows, GDN_WIDTH), dt)
    return pl.pallas_call(
        _gdn_local_kernel,
        grid=(seq // rb + 1,),
        in_specs=[
            blk(0), blk(1), blk(2), prev(0), prev(1), prev(2),
            pl.BlockSpec((CONV_WIDTH, 3 * GDN_WIDTH), lambda i: (0, 0)),
            pl.BlockSpec((rb, GATE_LANES), lambda i: (bidx(i), 0)),
            pl.BlockSpec((2 * GDN_HEADS, rb), lambda i: (0, bidx(i))),
        ],
        out_specs=[
            pl.BlockSpec((rb, GDN_WIDTH), lambda i: (bidx(i), 0)),
            pl.BlockSpec((rb, GDN_WIDTH), lambda i: (bidx(i), 0)),
            pl.BlockSpec((rb, GDN_WIDTH), lambda i: (bidx(i), 0)),
            pl.BlockSpec((rb, GDN_WIDTH), lambda i: (bidx(i), 0)),
            pl.BlockSpec((GDN_HEADS, rb, CHUNK), lambda i: (0, bidx(i), 0)),
            pl.BlockSpec((GDN_LOCAL_CHUNKS, GDN_HEADS, GDN_DIM), lambda i: (bidx(i), 0, 0)),
        ],
        out_shape=[
            row_out(BF16),
            row_out(F32),
            row_out(BF16),
            row_out(BF16),
            jax.ShapeDtypeStruct((GDN_HEADS, rows, CHUNK), BF16),
            jax.ShapeDtypeStruct((rows // CHUNK, GDN_HEADS, GDN_DIM), F32),
        ],
        compiler_params=_params(("parallel",)),
        name="gdn_local",
    )(proj, proj, proj, proj, proj, proj, conv_w, gcol, grow)


def _gdn_state_kernel(w_ref, u_ref, qg_ref, kd_ref, attn_ref, egl_ref, z_ref, gain_ref, o_ref, s_ref):
    @pl.when(pl.program_id(0) == 0)
    def _():
        s_ref[...] = jnp.zeros_like(s_ref)

    gain = gain_ref[...]
    heads = range(GDN_HEADS)
    lanes = [slice(h * GDN_DIM, (h + 1) * GDN_DIM) for h in heads]
    ss = [s_ref[h] for h in heads]
    rs = [_dot(jnp.concatenate([w_ref[:, ls], qg_ref[:, ls]], axis=0), s.astype(BF16))
          for ls, s in zip(lanes, ss)]
    vns = [(u_ref[:, ls] - r[:CHUNK]).astype(BF16) for ls, r in zip(lanes, rs)]
    os_ = [r[CHUNK:] + _dot(attn_ref[h], vn) for h, r, vn in zip(heads, rs, vns)]
    for h, ls, s, vn in zip(heads, lanes, ss, vns):
        s_ref[h] = s * egl_ref[0, h:h + 1, :] + _dot_tn(kd_ref[:, ls], vn)
    for ls, o in zip(lanes, os_):
        on = o * lax.rsqrt(jnp.mean(o * o, axis=-1, keepdims=True) + NORM_EPS) * gain
        o_ref[:, ls] = (on * _silu(z_ref[:, ls])).astype(BF16)


def _gdn_state(w, u, qg, kd, attn, egl, proj, gain, seq):
    rows = w.shape[0]
    nchunks = rows // CHUNK
    nseq = seq // CHUNK + 1
    phys = lambda c: (c + nchunks - 1) % nchunks
    rowblk = lambda col: pl.BlockSpec((CHUNK, GDN_WIDTH), lambda c, col=col: (phys(c), col))
    return pl.pallas_call(
        _gdn_state_kernel,
        grid=(nseq,),
        in_specs=[
            rowblk(0), rowblk(0), rowblk(0), rowblk(0),
            pl.BlockSpec((GDN_HEADS, CHUNK, CHUNK), lambda c: (0, phys(c), 0)),
            pl.BlockSpec((1, GDN_HEADS, GDN_DIM), lambda c: (phys(c), 0, 0)),
            rowblk(3),
            pl.BlockSpec((1, GDN_DIM), lambda c: (0, 0)),
        ],
        out_specs=pl.BlockSpec((CHUNK, GDN_WIDTH), lambda c: (jnp.maximum(c - 1, 0), 0)),
        out_shape=jax.ShapeDtypeStruct((seq, GDN_WIDTH), BF16),
        scratch_shapes=[pltpu.VMEM((GDN_HEADS, GDN_DIM, GDN_DIM), F32)],
        compiler_params=_params(("arbitrary",)),
        name="gdn_state",
    )(w, u, qg, kd, attn, egl, proj, gain)


KV_TILE = 128
Q_SCALE = DIFF_DIM ** -0.5 * math.log2(math.e)


def _attn_prep_kernel(q_ref, k_ref, v_ref, cos_ref, sin_ref, qg_ref, kg_ref, gsum_ref,
                      q2_ref, kr_ref, vt_ref):
    cos = cos_ref[...]
    sin = sin_ref[...]
    gsum = gsum_ref[...]
    lane = lax.broadcasted_iota(jnp.int32, cos.shape, 1)
    first_half = (lane % DIFF_DIM) < (DIFF_DIM // 2)
    low_map = lane < DIFF_DIM

    def norm_rope(x, gain):
        ms = _dot((x * x).astype(BF16), gsum) * (1.0 / DIFF_DIM)
        xn = x * lax.rsqrt(ms + NORM_EPS) * gain
        rot = jnp.where(first_half, pltpu.roll(xn, LANES - DIFF_DIM // 2, axis=1),
                        pltpu.roll(xn, DIFF_DIM // 2, axis=1))
        return xn * cos + rot * sin

    for h in range(DIFF_HEADS):
        ls = slice(h * DIFF_VDIM, (h + 1) * DIFF_VDIM)
        q = norm_rope(q_ref[:, ls], qg_ref[...]) * Q_SCALE
        q2_ref[0, :, ls] = jnp.where(low_map, q, 0.0).astype(BF16)
        q2_ref[1, :, ls] = jnp.where(low_map, 0.0, q).astype(BF16)
        kr_ref[:, ls] = norm_rope(k_ref[:, ls], kg_ref[...]).astype(BF16)
        vt_ref[0, ls, :] = v_ref[:, ls].T.astype(BF16)


def _attn_prep(proj, cos, sin, qgain, kgain, gsum, seq):
    rows = proj.shape[0]
    tm = KV_TILE
    bidx = _tail_block_index(seq // tm, rows // tm)
    col = lambda c: pl.BlockSpec((tm, DIFF_WIDTH), lambda i, c=c: (bidx(i), c))
    small = lambda shape: pl.BlockSpec(shape, lambda i: (0,) * len(shape))
    return pl.pallas_call(
        _attn_prep_kernel,
        grid=(seq // tm + 1,),
        in_specs=[
            col(4), col(5), col(6),
            pl.BlockSpec((tm, LANES), lambda i: (bidx(i), 0)),
            pl.BlockSpec((tm, LANES), lambda i: (bidx(i), 0)),
            small((1, LANES)), small((1, LANES)), small((LANES, LANES)),
        ],
        out_specs=[
            pl.BlockSpec((2, tm, DIFF_WIDTH), lambda i: (0, bidx(i), 0)),
            pl.BlockSpec((tm, DIFF_WIDTH), lambda i: (bidx(i), 0)),
            pl.BlockSpec((1, DIFF_WIDTH, tm), lambda i: (bidx(i), 0, 0)),
        ],
        out_shape=[
            jax.ShapeDtypeStruct((2, rows, DIFF_WIDTH), BF16),
            jax.ShapeDtypeStruct((rows, DIFF_WIDTH), BF16),
            jax.ShapeDtypeStruct((rows // tm, DIFF_WIDTH, tm), BF16),
        ],
        compiler_params=_params(("parallel",)),
        name="attn_prep",
    )(proj, proj, proj, cos, sin, qgain, kgain, gsum)


ATTN_BLOCK = 512
ATTN_QSUB = 256


def _diff_attn_kernel(q_ref, k_ref, vt_ref, lam_ref, gain_ref, o_ref, acc_ref, st_ref):
    i = pl.program_id(1)
    rows = k_ref.shape[0]
    bk = ATTN_QSUB
    chains = [(mp, sb) for mp in range(2) for sb in range(2)]
    every = list(range(len(chains)))
    upper = [c for c in every if chains[c][1] == 1]
    qs = [q_ref[mp, sb * ATTN_QSUB:(sb + 1) * ATTN_QSUB, :] for mp, sb in chains]
    kv_tiles = bk // KV_TILE

    def update(sts, vt, ms, ls, which):
        ms, ls = list(ms), list(ls)
        first = ms[which[0]] is None
        cms = [jnp.max(st, axis=0, keepdims=True) for st in sts]
        m_new = cms if first else [jnp.maximum(ms[c], cm) for c, cm in zip(which, cms)]
        ps = [jnp.exp2(st - mn) for st, mn in zip(sts, m_new)]
        pvs = [_dot(vt, p.astype(BF16)) for p in ps]
        psums = [jnp.sum(p, axis=0, keepdims=True) for p in ps]
        for n, c in enumerate(which):
            if first:
                ls[c] = psums[n]
                acc_ref[c] = pvs[n]
            else:
                alpha = jnp.exp2(ms[c] - m_new[n])
                ls[c] = alpha * ls[c] + psums[n]
                acc_ref[c] = alpha * acc_ref[c] + pvs[n]
            ms[c] = m_new[n]
        return ms, ls

    def store_scores(j, slot, which):
        start = j * bk if isinstance(j, int) else pl.multiple_of(j * bk, bk)
        k_c = k_ref[pl.ds(start, bk), :]
        for c in which:
            st_ref[slot, c] = _dot_nt(k_c, qs[c])

    def values_t(j):
        return jnp.concatenate([vt_ref[j * kv_tiles + t] for t in range(kv_tiles)], axis=1)

    k_meta = k_ref[rows - KV_TILE:rows, :]
    key = lax.broadcasted_iota(jnp.int32, (KV_TILE, ATTN_QSUB), 0)
    sts = [jnp.where(key >= KV_TILE - N_META, _dot_nt(k_meta, q), MASK_VALUE) for q in qs]
    ms, ls = update(sts, vt_ref[rows // KV_TILE - 1], [None] * 4, [None] * 4, every)

    store_scores(0, 0, every)

    def body(t, carry):
        ms, ls = carry
        store_scores(2 * t + 1, 1, every)
        ms, ls = update([st_ref[0, c] for c in every], values_t(2 * t), ms, ls, every)
        store_scores(2 * t + 2, 0, every)
        ms, ls = update([st_ref[1, c] for c in every], values_t(2 * t + 1), ms, ls, every)
        return tuple(ms), tuple(ls)

    ms, ls = lax.fori_loop(0, i, body, (tuple(ms), tuple(ls)))
    store_scores(2 * i + 1, 1, upper)
    tri = (lax.broadcasted_iota(jnp.int32, (bk, ATTN_QSUB), 0)
           <= lax.broadcasted_iota(jnp.int32, (bk, ATTN_QSUB), 1))
    sts = [st_ref[0, c] if c in upper else jnp.where(tri, st_ref[0, c], MASK_VALUE) for c in every]
    ms, ls = update(sts, values_t(2 * i), ms, ls, every)
    sts = [jnp.where(tri, st_ref[1, c], MASK_VALUE) for c in upper]
    ms, ls = update(sts, values_t(2 * i + 1), ms, ls, upper)
    nsub = 2

    lp = lam_ref[...]
    lam = (jnp.exp(jnp.sum(lp[0:1] * lp[1:2], axis=-1, keepdims=True))
           - jnp.exp(jnp.sum(lp[2:3] * lp[3:4], axis=-1, keepdims=True)) + LAMBDA_INIT)
    gain = gain_ref[...]
    for sb in range(nsub):
        ot = acc_ref[sb] / ls[sb] - lam * (acc_ref[nsub + sb] / ls[nsub + sb])
        ot = ot * lax.rsqrt(jnp.mean(ot * ot, axis=0, keepdims=True) + NORM_EPS) * gain
        o_ref[sb * ATTN_QSUB:(sb + 1) * ATTN_QSUB, :] = (ot * (1.0 - LAMBDA_INIT)).T.astype(BF16)


def _diff_attn(q2, kr, vt, lam_params, gain_col, seq):
    rows = kr.shape[0]
    bq = ATTN_BLOCK
    assert seq % bq == 0 and bq == 2 * ATTN_QSUB
    return pl.pallas_call(
        _diff_attn_kernel,
        grid=(DIFF_HEADS, seq // bq),
        in_specs=[
            pl.BlockSpec((2, bq, DIFF_VDIM), lambda h, i: (0, i, h)),
            pl.BlockSpec((rows, DIFF_VDIM), lambda h, i: (0, h)),
            pl.BlockSpec((rows // KV_TILE, DIFF_VDIM, KV_TILE), lambda h, i: (0, h, 0)),
            pl.BlockSpec((4, DIFF_DIM), lambda h, i: (0, 0)),
            pl.BlockSpec((DIFF_VDIM, 1), lambda h, i: (0, 0)),
        ],
        out_specs=pl.BlockSpec((bq, DIFF_VDIM), lambda h, i: (i, h)),
        out_shape=jax.ShapeDtypeStruct((seq, DIFF_WIDTH), BF16),
        scratch_shapes=[pltpu.VMEM((4, DIFF_VDIM, ATTN_QSUB), F32),
                        pltpu.VMEM((2, 4, ATTN_QSUB, ATTN_QSUB), F32)],
        compiler_params=_params(("parallel", "arbitrary")),
        name="diff_attn",
    )(q2, kr, vt, lam_params, gain_col)


def _out_proj_kernel(mg_ref, md_ref, wg_ref, wd_ref, h_ref, gain_ref, h2_ref, n2_ref):
    h2 = h_ref[...] + _dot(mg_ref[...], wg_ref[...]) + _dot(md_ref[...], wd_ref[...])
    h2_ref[...] = h2
    ms = jnp.mean(h2 * h2, axis=-1, keepdims=True)
    n2_ref[...] = (h2 * lax.rsqrt(ms + NORM_EPS) * gain_ref[...]).astype(BF16)


def _out_proj(mix_g, mix_d, w_out, h, gain, seq):
    tm = _pick(seq, (512, 128))
    return pl.pallas_call(
        _out_proj_kernel,
        grid=(seq // tm,),
        in_specs=[
            pl.BlockSpec((tm, GDN_WIDTH), lambda i: (i, 0)),
            pl.BlockSpec((tm, DIFF_WIDTH), lambda i: (i, 0)),
            pl.BlockSpec((GDN_WIDTH, D_MODEL), lambda i: (0, 0)),
            pl.BlockSpec((DIFF_WIDTH, D_MODEL), lambda i: (1, 0)),
            pl.BlockSpec((tm, D_MODEL), lambda i: (i, 0)),
            pl.BlockSpec((1, D_MODEL), lambda i: (0, 0)),
        ],
        out_specs=[
            pl.BlockSpec((tm, D_MODEL), lambda i: (i, 0)),
            pl.BlockSpec((tm, D_MODEL), lambda i: (i, 0)),
        ],
        out_shape=[
            jax.ShapeDtypeStruct((seq, D_MODEL), F32),
            jax.ShapeDtypeStruct((seq, D_MODEL), BF16),
        ],
        compiler_params=_params(("parallel",)),
        name="out_proj",
    )(mix_g, mix_d, w_out, w_out, h, gain)


def _gate_up_kernel(n_ref, wg_ref, wu_ref, a_ref, wg16_ref, wu16_ref):
    @pl.when(pl.program_id(1) == 0)
    def _():
        wg16_ref[...] = wg_ref[...].astype(BF16)
        wu16_ref[...] = wu_ref[...].astype(BF16)

    n = n_ref[...]
    g = _dot(n, wg16_ref[...])
    u = _dot(n, wu16_ref[...])
    a_ref[...] = (_silu(g) * u).astype(BF16)


def _gate_up(n2, w_gu):
    seq = n2.shape[0]
    tm = _pick(seq, (1024, 128))
    tn = 512
    nt = D_FF // tn
    return pl.pallas_call(
        _gate_up_kernel,
        grid=(nt, seq // tm),
        in_specs=[
            pl.BlockSpec((tm, D_MODEL), lambda j, i: (i, 0)),
            pl.BlockSpec((D_MODEL, tn), lambda j, i: (0, j)),
            pl.BlockSpec((D_MODEL, tn), lambda j, i: (0, j + nt)),
        ],
        out_specs=pl.BlockSpec((tm, tn), lambda j, i: (i, j)),
        out_shape=jax.ShapeDtypeStruct((seq, D_FF), BF16),
        scratch_shapes=[pltpu.VMEM((D_MODEL, tn), BF16), pltpu.VMEM((D_MODEL, tn), BF16)],
        compiler_params=_params(("parallel", "arbitrary")),
        name="ffn_gate_up",
    )(n2, w_gu, w_gu)


def _down_kernel(a_ref, w_ref, h_ref, o_ref):
    o_ref[...] = h_ref[...] + _dot(a_ref[...], w_ref[...])


def _down(act, w_down, h2):
    seq = act.shape[0]
    tm = _pick(seq, (512, 128))
    tn = 1024
    return pl.pallas_call(
        _down_kernel,
        grid=(D_MODEL // tn, seq // tm),
        in_specs=[
            pl.BlockSpec((tm, D_FF), lambda j, i: (i, 0)),
            pl.BlockSpec((D_FF, tn), lambda j, i: (0, j)),
            pl.BlockSpec((tm, tn), lambda j, i: (i, j)),
        ],
        out_specs=pl.BlockSpec((tm, tn), lambda j, i: (i, j)),
        out_shape=jax.ShapeDtypeStruct((seq, D_MODEL), F32),
        compiler_params=_params(("parallel", "parallel")),
        name="ffn_down",
    )(act, w_down, h2)


def _rope_tables(seq):
    half = DIFF_DIM // 2
    pos = jnp.concatenate([jnp.arange(seq) + N_META, jnp.zeros((META_BLOCK - N_META,), jnp.int32),
                           jnp.arange(N_META)]).astype(F32)
    inv_freq = ROPE_THETA ** (-jnp.arange(half, dtype=F32) / half)
    ang = pos[:, None] * inv_freq[None, :]
    cos = jnp.tile(jnp.cos(ang), (1, LANES // half))
    sin = jnp.sin(ang)
    sin = jnp.tile(jnp.concatenate([-sin, sin], axis=1), (1, LANES // DIFF_DIM))
    return cos, sin


def _lane_pad(v, offset):
    return jnp.zeros((1, GATE_LANES), F32).at[0, offset:offset + v.shape[0]].set(v.astype(F32))


def kernel(x, meta_tokens, attn_norm, w_in, conv_w, a_log, dt_bias, gdn_norm, q_norm, k_norm,
           lambda_q1, lambda_k1, lambda_q2, lambda_k2, diff_norm, w_out, ffn_norm, w_gate_up, w_down):
    assert x.shape[0] == 1 and x.shape[2] == D_MODEL
    seq = x.shape[1]
    assert seq % META_BLOCK == 0
    xs = x[0]
    meta_block = jnp.concatenate([jnp.zeros((META_BLOCK - N_META, D_MODEL), xs.dtype),
                                  meta_tokens.astype(xs.dtype)], axis=0)

    gdn_cols = 4 * GDN_WIDTH
    w_in16 = w_in[0].astype(BF16)
    w_diff = w_in16[:, gdn_cols + 2 * GDN_HEADS:]
    w_ba = jnp.pad(w_in16[:, gdn_cols:gdn_cols + 2 * GDN_HEADS], ((0, 0), (0, GATE_LANES - 2 * GDN_HEADS)))

    n1, gcol, grow = _prenorm_gate(xs, meta_block, attn_norm, w_ba, _lane_pad(a_log[0], GDN_HEADS),
                                   _lane_pad(dt_bias[0], GDN_HEADS))
    proj = _in_proj(n1, w_in16, w_diff)

    w, u, qg, kd, attn, egl = _gdn_local(proj, conv_w[0], gcol, grow, seq)
    mix_g = _gdn_state(w, u, qg, kd, attn, egl, proj, gdn_norm, seq)

    cos, sin = _rope_tables(seq)
    tile2 = lambda g: jnp.tile(g.astype(F32), (1, LANES // DIFF_DIM))
    lane = np.arange(LANES)
    gsum = jnp.asarray((lane[:, None] // DIFF_DIM) == (lane[None, :] // DIFF_DIM), BF16)
    q2, kr, vt = _attn_prep(proj, cos, sin, tile2(q_norm), tile2(k_norm), gsum, seq)
    lam_params = jnp.concatenate([lambda_q1, lambda_k1, lambda_q2, lambda_k2], axis=0).astype(F32)
    mix_d = _diff_attn(q2, kr, vt, lam_params, diff_norm.astype(F32).reshape(DIFF_VDIM, 1), seq)

    h2, n2 = _out_proj(mix_g, mix_d, w_out[0].astype(BF16), xs, ffn_norm, seq)
    act = _gate_up(n2, w_gate_up[0])
    out = _down(act, w_down[0].astype(BF16), h2)
    return out[None]
```

```python
import functools
import math

import jax
import jax.numpy as jnp
import numpy as np
from jax import lax
from jax.experimental import pallas as pl
from jax.experimental.pallas import tpu as pltpu

F32 = jnp.float32
BF16 = jnp.bfloat16

D_MODEL = 2048
N_META = 16
GDN_HEADS = 8
GDN_DIM = 128
GDN_WIDTH = GDN_HEADS * GDN_DIM
CONV_WIDTH = 4
CHUNK = 64
DIFF_HEADS = 8
DIFF_DIM = 64
DIFF_VDIM = 2 * DIFF_DIM
DIFF_WIDTH = DIFF_HEADS * DIFF_VDIM
ROPE_THETA = 10000.0
D_FF = 5632
NORM_EPS = 1e-6
MASK_VALUE = -1e30
LAMBDA_INIT = 0.8 - 0.6 * math.exp(-0.3 * 0)

LANES = 128
META_BLOCK = 512
GATE_LANES = 128
VMEM_LIMIT = 56 * 1024 * 1024


def _pick(n, candidates):
    for c in candidates:
        if n % c == 0:
            return c
    raise ValueError(f"no tile in {candidates} divides {n}")


def _params(sem, vmem=VMEM_LIMIT):
    return pltpu.CompilerParams(dimension_semantics=sem, vmem_limit_bytes=vmem)


def _dot(a, b):
    return jnp.dot(a, b, preferred_element_type=F32)


def _dot_nt(a, b):
    return lax.dot_general(a, b, (((1,), (1,)), ((), ())), preferred_element_type=F32)


def _dot_tn(a, b):
    return lax.dot_general(a, b, (((0,), (0,)), ((), ())), preferred_element_type=F32)


def _softplus(x):
    return jnp.maximum(x, 0.0) + jnp.log1p(jnp.exp(-jnp.abs(x)))


def _silu(x):
    return x * jax.nn.sigmoid(x)


def _prenorm_gate_kernel(x_ref, mb_ref, gain_ref, wba_ref, alog_ref, dtb_ref, n_ref, gcol_ref, grow_ref):
    h = jnp.where(pl.program_id(0) < pl.num_programs(0) - 1, x_ref[...], mb_ref[...])
    ms = jnp.mean(h * h, axis=-1, keepdims=True)
    n = (h * lax.rsqrt(ms + NORM_EPS) * gain_ref[...]).astype(BF16)
    n_ref[...] = n
    ba = _dot_nt(n, wba_ref[...])
    beta = jax.nn.sigmoid(ba)
    g = -jnp.exp(alog_ref[...]) * _softplus(ba + dtb_ref[...])
    row = lax.broadcasted_iota(jnp.int32, ba.shape, 0) % CHUNK
    gc = g
    for d in (1, 2, 4, 8, 16, 32):
        gc = gc + jnp.where(row >= d, pltpu.roll(gc, d, axis=0), 0.0)
    lane = lax.broadcasted_iota(jnp.int32, ba.shape, 1)
    out = jnp.where(lane < GDN_HEADS, beta, gc)
    gcol_ref[...] = out
    grow_ref[...] = out.T[: 2 * GDN_HEADS]


def _prenorm_gate(x, meta_block, gain, wba, alog, dtb):
    tm = META_BLOCK
    nx = x.shape[0] // tm
    rows = x.shape[0] + tm
    return pl.pallas_call(
        _prenorm_gate_kernel,
        grid=(nx + 1,),
        in_specs=[
            pl.BlockSpec((tm, D_MODEL), lambda i: (jnp.minimum(i, nx - 1), 0)),
            pl.BlockSpec((tm, D_MODEL), lambda i: (0, 0)),
            pl.BlockSpec((1, D_MODEL), lambda i: (0, 0)),
            pl.BlockSpec((GATE_LANES, D_MODEL), lambda i: (0, 0)),
            pl.BlockSpec((1, GATE_LANES), lambda i: (0, 0)),
            pl.BlockSpec((1, GATE_LANES), lambda i: (0, 0)),
        ],
        out_specs=[
            pl.BlockSpec((tm, D_MODEL), lambda i: (i, 0)),
            pl.BlockSpec((tm, GATE_LANES), lambda i: (i, 0)),
            pl.BlockSpec((2 * GDN_HEADS, tm), lambda i: (0, i)),
        ],
        out_shape=[
            jax.ShapeDtypeStruct((rows, D_MODEL), BF16),
            jax.ShapeDtypeStruct((rows, GATE_LANES), F32),
            jax.ShapeDtypeStruct((2 * GDN_HEADS, rows), F32),
        ],
        compiler_params=_params(("parallel",)),
        name="prenorm_gate",
    )(x, meta_block, gain, wba, alog, dtb)


IN_PROJ_TN = 1024
GDN_COL_TILES = 4 * GDN_WIDTH // IN_PROJ_TN


def _in_proj_kernel(a_ref, wg_ref, wd_ref, o_ref, w16_ref):
    j = pl.program_id(0)
    first = pl.program_id(1) == 0

    @pl.when(first & (j < GDN_COL_TILES))
    def _():
        w16_ref[...] = wg_ref[...].astype(BF16)

    @pl.when(first & (j >= GDN_COL_TILES))
    def _():
        w16_ref[...] = wd_ref[...].astype(BF16)

    o_ref[...] = _dot_nt(a_ref[...], w16_ref[...])


def _in_proj(n1, wt_all, wt_diff):
    m = n1.shape[0]
    tm, tn = META_BLOCK, IN_PROJ_TN
    n = GDN_COL_TILES * tn + wt_diff.shape[0]
    return pl.pallas_call(
        _in_proj_kernel,
        grid=(n // tn, m // tm),
        in_specs=[
            pl.BlockSpec((tm, D_MODEL), lambda j, i: (i, 0)),
            pl.BlockSpec((tn, D_MODEL), lambda j, i: (jnp.minimum(j, GDN_COL_TILES - 1), 0)),
            pl.BlockSpec((tn, D_MODEL), lambda j, i: (jnp.maximum(j - GDN_COL_TILES, 0), 0)),
        ],
        out_specs=pl.BlockSpec((tm, tn), lambda j, i: (i, j)),
        out_shape=jax.ShapeDtypeStruct((m, n), F32),
        scratch_shapes=[pltpu.VMEM((tn, D_MODEL), BF16)],
        compiler_params=_params(("parallel", "arbitrary")),
        name="in_proj",
    )(n1, wt_all, wt_diff)


GDN_LOCAL_CHUNKS = 4


def _causal_conv_silu(x, prev, w):
    r8 = lax.broadcasted_iota(jnp.int32, prev.shape, 0)
    y = x * w[CONV_WIDTH - 1:CONV_WIDTH]
    for d in range(1, CONV_WIDTH):
        shifted = pltpu.roll(x, d, axis=0)
        top = jnp.where(r8 < d, pltpu.roll(prev, d, axis=0), shifted[:8])
        shifted = jnp.concatenate([top, shifted[8:]], axis=0)
        y = y + shifted * w[CONV_WIDTH - 1 - d:CONV_WIDTH - d]
    return _silu(y)


def _gdn_local_kernel(q_ref, k_ref, v_ref, qp_ref, kp_ref, vp_ref, cw_ref, gcol_ref, grow_ref,
                      w_ref, u_ref, qg_ref, kd_ref, attn_ref, egl_ref):
    cw = cw_ref[...]
    q_all = _causal_conv_silu(q_ref[...], qp_ref[...], cw[:, 0:GDN_WIDTH])
    k_all = _causal_conv_silu(k_ref[...], kp_ref[...], cw[:, GDN_WIDTH:2 * GDN_WIDTH])
    v_all = _causal_conv_silu(v_ref[...], vp_ref[...], cw[:, 2 * GDN_WIDTH:3 * GDN_WIDTH])
    gcol = gcol_ref[...]
    grow = grow_ref[...]
    ii = lax.broadcasted_iota(jnp.int32, (CHUNK, CHUNK), 0)
    jj = lax.broadcasted_iota(jnp.int32, (CHUNK, CHUNK), 1)
    units = [(c, h) for c in range(GDN_LOCAL_CHUNKS) for h in range(GDN_HEADS)]
    rows_of = lambda c: slice(c * CHUNK, (c + 1) * CHUNK)
    lanes_of = lambda h: slice(h * GDN_DIM, (h + 1) * GDN_DIM)
    qs, ks, kbs, vbs, egcs, gcs, kqs = [], [], [], [], [], [], []
    for c, h in units:
        rs, ls = rows_of(c), lanes_of(h)
        q = q_all[rs, ls]
        k = k_all[rs, ls]
        q = q * lax.rsqrt(jnp.sum(q * q, axis=-1, keepdims=True) + NORM_EPS) * (GDN_DIM ** -0.5)
        k = k * lax.rsqrt(jnp.sum(k * k, axis=-1, keepdims=True) + NORM_EPS)
        beta_c = gcol[rs, h:h + 1]
        gc_c = gcol[rs, GDN_HEADS + h:GDN_HEADS + h + 1]
        kb = k * beta_c
        qs.append(q)
        ks.append(k)
        kbs.append(kb)
        vbs.append(v_all[rs, ls] * beta_c)
        gcs.append(gc_c)
        egcs.append(jnp.exp(gc_c))
        kqs.append(_dot_nt(jnp.concatenate([kb.astype(BF16), q.astype(BF16)], axis=0), k.astype(BF16)))
    lms, attns = [], []
    for (c, h), kq, gc_c in zip(units, kqs, gcs):
        gc_r = grow[GDN_HEADS + h:GDN_HEADS + h + 1, rows_of(c)]
        decay = jnp.exp(jnp.where(ii >= jj, gc_c - gc_r, MASK_VALUE))
        lms.append(jnp.where(ii > jj, kq[:CHUNK] * decay, 0.0))
        attns.append(kq[CHUNK:] * decay)
    xor = ii ^ jj
    eye = jnp.where(ii == jj, 1.0, 0.0)
    xs = [eye - jnp.where(xor == 1, lm, 0.0) for lm in lms]
    level = 1
    while (2 << level) <= CHUNK:
        sel = (xor >> level) == 1
        ys = [_dot(jnp.where(sel, lm, 0.0).astype(BF16), x.astype(BF16)) for lm, x in zip(lms, xs)]
        xs = [x - _dot(x.astype(BF16), y.astype(BF16)) for x, y in zip(xs, ys)]
        level += 1
    uws = [_dot(x.astype(BF16), jnp.concatenate([vb.astype(BF16), (kb * egc).astype(BF16)], axis=1))
           for x, vb, kb, egc in zip(xs, vbs, kbs, egcs)]
    for (c, h), uw, q, k, egc, gc_c, attn in zip(units, uws, qs, ks, egcs, gcs, attns):
        rs, ls = rows_of(c), lanes_of(h)
        u_ref[rs, ls] = uw[:, :GDN_DIM]
        w_ref[rs, ls] = uw[:, GDN_DIM:].astype(BF16)
        qg_ref[rs, ls] = (q * egc).astype(BF16)
        gc_last = gc_c[CHUNK - 1:CHUNK]
        kd_ref[rs, ls] = (k * jnp.exp(gc_last - gc_c)).astype(BF16)
        attn_ref[h, rs, :] = attn.astype(BF16)
        egl_ref[c, h:h + 1, :] = jnp.broadcast_to(jnp.exp(gc_last), (1, GDN_DIM))


def _tail_block_index(n_x_blocks, n_blocks):
    return lambda i: jnp.where(i < n_x_blocks, i, n_blocks - 1)


def _gdn_local(proj, conv_w, gcol, grow, seq):
    rows = proj.shape[0]
    rb = GDN_LOCAL_CHUNKS * CHUNK
    bidx = _tail_block_index(seq // rb, rows // rb)
    n8 = rows // 8
    blk = lambda col: pl.BlockSpec((rb, GDN_WIDTH), lambda i, col=col: (bidx(i), col))
    prev = lambda col: pl.BlockSpec((8, GDN_WIDTH),
                                    lambda i, col=col: ((bidx(i) * (rb // 8) + n8 - 1) % n8, col))
    row_out = lambda dt: jax.ShapeDtypeStruct((rows, GDN_WIDTH), dt)
    return pl.pallas_call(
        _gdn_local_kernel,
        grid=(seq // rb + 1,),
        in_specs=[
            blk(0), blk(1), blk(2), prev(0), prev(1), prev(2),
            pl.BlockSpec((CONV_WIDTH, 3 * GDN_WIDTH), lambda i: (0, 0)),
            pl.BlockSpec((rb, GATE_LANES), lambda i: (bidx(i), 0)),
            pl.BlockSpec((2 * GDN_HEADS, rb), lambda i: (0, bidx(i))),
        ],
        out_specs=[
            pl.BlockSpec((rb, GDN_WIDTH), lambda i: (bidx(i), 0)),
            pl.BlockSpec((rb, GDN_WIDTH), lambda i: (bidx(i), 0)),
            pl.BlockSpec((rb, GDN_WIDTH), lambda i: (bidx(i), 0)),
            pl.BlockSpec((rb, GDN_WIDTH), lambda i: (bidx(i), 0)),
            pl.BlockSpec((GDN_HEADS, rb, CHUNK), lambda i: (0, bidx(i), 0)),
            pl.BlockSpec((GDN_LOCAL_CHUNKS, GDN_HEADS, GDN_DIM), lambda i: (bidx(i), 0, 0)),
        ],
        out_shape=[
            row_out(BF16),
            row_out(F32),
            row_out(BF16),
            row_out(BF16),
            jax.ShapeDtypeStruct((GDN_HEADS, rows, CHUNK), BF16),
            jax.ShapeDtypeStruct((rows // CHUNK, GDN_HEADS, GDN_DIM), F32),
        ],
        compiler_params=_params(("parallel",)),
        name="gdn_local",
    )(proj, proj, proj, proj, proj, proj, conv_w, gcol, grow)


GDN_STATE_CHUNKS = 4


def _gdn_state_kernel(w_ref, u_ref, qg_ref, kd_ref, attn_ref, egl_ref, z_ref, gain_ref, o_ref, s_ref):
    @pl.when(pl.program_id(0) == 0)
    def _():
        s_ref[...] = jnp.zeros_like(s_ref)

    gain = gain_ref[...]
    heads = range(GDN_HEADS)
    lanes = [slice(h * GDN_DIM, (h + 1) * GDN_DIM) for h in heads]
    ss = [s_ref[h] for h in heads]
    for ck in range(GDN_STATE_CHUNKS):
        rw = slice(ck * CHUNK, (ck + 1) * CHUNK)
        rs = [_dot(jnp.concatenate([w_ref[rw, ls], qg_ref[rw, ls]], axis=0), s.astype(BF16))
              for ls, s in zip(lanes, ss)]
        vns = [(u_ref[rw, ls] - r[:CHUNK]).astype(BF16) for ls, r in zip(lanes, rs)]
        os_ = [r[CHUNK:] + _dot(attn_ref[h, rw, :], vn) for h, r, vn in zip(heads, rs, vns)]
        ss = [s * egl_ref[ck, h:h + 1, :] + _dot_tn(kd_ref[rw, ls], vn)
              for h, ls, s, vn in zip(heads, lanes, ss, vns)]
        for ls, o in zip(lanes, os_):
            on = o * lax.rsqrt(jnp.mean(o * o, axis=-1, keepdims=True) + NORM_EPS) * gain
            o_ref[rw, ls] = (on * _silu(z_ref[rw, ls])).astype(BF16)
    for h, s in zip(heads, ss):
        s_ref[h] = s


def _gdn_state(w, u, qg, kd, attn, egl, proj, gain, seq):
    rows = w.shape[0]
    rb = GDN_STATE_CHUNKS * CHUNK
    nblocks = rows // rb
    phys = lambda c: (c + nblocks - 1) % nblocks
    rowblk = lambda col: pl.BlockSpec((rb, GDN_WIDTH), lambda c, col=col: (phys(c), col))
    return pl.pallas_call(
        _gdn_state_kernel,
        grid=(seq // rb + 1,),
        in_specs=[
            rowblk(0), rowblk(0), rowblk(0), rowblk(0),
            pl.BlockSpec((GDN_HEADS, rb, CHUNK), lambda c: (0, phys(c), 0)),
            pl.BlockSpec((GDN_STATE_CHUNKS, GDN_HEADS, GDN_DIM), lambda c: (phys(c), 0, 0)),
            rowblk(3),
            pl.BlockSpec((1, GDN_DIM), lambda c: (0, 0)),
        ],
        out_specs=pl.BlockSpec((rb, GDN_WIDTH), lambda c: (jnp.maximum(c - 1, 0), 0)),
        out_shape=jax.ShapeDtypeStruct((seq, GDN_WIDTH), BF16),
        scratch_shapes=[pltpu.VMEM((GDN_HEADS, GDN_DIM, GDN_DIM), F32)],
        compiler_params=_params(("arbitrary",)),
        name="gdn_state",
    )(w, u, qg, kd, attn, egl, proj, gain)


KV_TILE = 128
Q_SCALE = DIFF_DIM ** -0.5 * math.log2(math.e)


def _attn_prep_kernel(q_ref, k_ref, v_ref, cos_ref, sin_ref, qg_ref, kg_ref, gsum_ref,
                      q2_ref, kr_ref, vt_ref):
    cos = cos_ref[...]
    sin = sin_ref[...]
    gsum = gsum_ref[...]
    lane = lax.broadcasted_iota(jnp.int32, cos.shape, 1)
    first_half = (lane % DIFF_DIM) < (DIFF_DIM // 2)
    low_map = lane < DIFF_DIM

    def norm_rope(x, gain):
        ms = _dot((x * x).astype(BF16), gsum) * (1.0 / DIFF_DIM)
        xn = x * lax.rsqrt(ms + NORM_EPS) * gain
        rot = jnp.where(first_half, pltpu.roll(xn, LANES - DIFF_DIM // 2, axis=1),
                        pltpu.roll(xn, DIFF_DIM // 2, axis=1))
        return xn * cos + rot * sin

    for h in range(DIFF_HEADS):
        ls = slice(h * DIFF_VDIM, (h + 1) * DIFF_VDIM)
        q = norm_rope(q_ref[:, ls], qg_ref[...]) * Q_SCALE
        q2_ref[0, :, ls] = jnp.where(low_map, q, 0.0).astype(BF16)
        q2_ref[1, :, ls] = jnp.where(low_map, 0.0, q).astype(BF16)
        kr_ref[:, ls] = norm_rope(k_ref[:, ls], kg_ref[...]).astype(BF16)
        vt_ref[0, ls, :] = v_ref[:, ls].T.astype(BF16)


def _attn_prep(proj, cos, sin, qgain, kgain, gsum, seq):
    rows = proj.shape[0]
    tm = KV_TILE
    bidx = _tail_block_index(seq // tm, rows // tm)
    col = lambda c: pl.BlockSpec((tm, DIFF_WIDTH), lambda i, c=c: (bidx(i), c))
    small = lambda shape: pl.BlockSpec(shape, lambda i: (0,) * len(shape))
    return pl.pallas_call(
        _attn_prep_kernel,
        grid=(seq // tm + 1,),
        in_specs=[
            col(4), col(5), col(6),
            pl.BlockSpec((tm, LANES), lambda i: (bidx(i), 0)),
            pl.BlockSpec((tm, LANES), lambda i: (bidx(i), 0)),
            small((1, LANES)), small((1, LANES)), small((LANES, LANES)),
        ],
        out_specs=[
            pl.BlockSpec((2, tm, DIFF_WIDTH), lambda i: (0, bidx(i), 0)),
            pl.BlockSpec((tm, DIFF_WIDTH), lambda i: (bidx(i), 0)),
            pl.BlockSpec((1, DIFF_WIDTH, tm), lambda i: (bidx(i), 0, 0)),
        ],
        out_shape=[
            jax.ShapeDtypeStruct((2, rows, DIFF_WIDTH), BF16),
            jax.ShapeDtypeStruct((rows, DIFF_WIDTH), BF16),
            jax.ShapeDtypeStruct((rows // tm, DIFF_WIDTH, tm), BF16),
        ],
        compiler_params=_params(("parallel",)),
        name="attn_prep",
    )(proj, proj, proj, cos, sin, qgain, kgain, gsum)


ATTN_BLOCK = 512
ATTN_QSUB = 256


def _diff_attn_kernel(q_ref, k_ref, vt_ref, lam_ref, gain_ref, o_ref, acc_ref, st_ref):
    i = pl.program_id(1)
    rows = k_ref.shape[0]
    bk = ATTN_QSUB
    chains = [(mp, sb) for mp in range(2) for sb in range(2)]
    every = list(range(len(chains)))
    upper = [c for c in every if chains[c][1] == 1]
    qs = [q_ref[mp, sb * ATTN_QSUB:(sb + 1) * ATTN_QSUB, :] for mp, sb in chains]
    kv_tiles = bk // KV_TILE

    def update(sts, vt, ms, ls, which):
        ms, ls = list(ms), list(ls)
        first = ms[which[0]] is None
        cms = [jnp.max(st, axis=0, keepdims=True) for st in sts]
        m_new = cms if first else [jnp.maximum(ms[c], cm) for c, cm in zip(which, cms)]
        ps = [jnp.exp2(st - mn) for st, mn in zip(sts, m_new)]
        pvs = [_dot(vt, p.astype(BF16)) for p in ps]
        psums = [jnp.sum(p, axis=0, keepdims=True) for p in ps]
        for n, c in enumerate(which):
            if first:
                ls[c] = psums[n]
                acc_ref[c] = pvs[n]
            else:
                alpha = jnp.exp2(ms[c] - m_new[n])
                ls[c] = alpha * ls[c] + psums[n]
                acc_ref[c] = alpha * acc_ref[c] + pvs[n]
            ms[c] = m_new[n]
        return ms, ls

    def store_scores(j, slot, which):
        start = j * bk if isinstance(j, int) else pl.multiple_of(j * bk, bk)
        k_c = k_ref[pl.ds(start, bk), :]
        for c in which:
            st_ref[slot, c] = _dot_nt(k_c, qs[c])

    def values_t(j):
        return jnp.concatenate([vt_ref[j * kv_tiles + t] for t in range(kv_tiles)], axis=1)

    k_meta = k_ref[rows - KV_TILE:rows, :]
    key = lax.broadcasted_iota(jnp.int32, (KV_TILE, ATTN_QSUB), 0)
    sts = [jnp.where(key >= KV_TILE - N_META, _dot_nt(k_meta, q), MASK_VALUE) for q in qs]
    ms, ls = update(sts, vt_ref[rows // KV_TILE - 1], [None] * 4, [None] * 4, every)

    store_scores(0, 0, every)

    def pair(t, ms, ls):
        store_scores(2 * t + 1, 1, every)
        ms, ls = update([st_ref[0, c] for c in every], values_t(2 * t), ms, ls, every)
        store_scores(2 * t + 2, 0, every)
        return update([st_ref[1, c] for c in every], values_t(2 * t + 1), ms, ls, every)

    def two_pairs(t, carry):
        ms, ls = pair(2 * t, *carry)
        ms, ls = pair(2 * t + 1, ms, ls)
        return tuple(ms), tuple(ls)

    def last_pair(t, carry):
        ms, ls = pair(i - 1, *carry)
        return tuple(ms), tuple(ls)

    carry = lax.fori_loop(0, i // 2, two_pairs, (tuple(ms), tuple(ls)))
    ms, ls = lax.fori_loop(0, i % 2, last_pair, carry)
    store_scores(2 * i + 1, 1, upper)
    tri = (lax.broadcasted_iota(jnp.int32, (bk, ATTN_QSUB), 0)
           <= lax.broadcasted_iota(jnp.int32, (bk, ATTN_QSUB), 1))
    sts = [st_ref[0, c] if c in upper else jnp.where(tri, st_ref[0, c], MASK_VALUE) for c in every]
    ms, ls = update(sts, values_t(2 * i), ms, ls, every)
    sts = [jnp.where(tri, st_ref[1, c], MASK_VALUE) for c in upper]
    ms, ls = update(sts, values_t(2 * i + 1), ms, ls, upper)
    nsub = 2

    lp = lam_ref[...]
    lam = (jnp.exp(jnp.sum(lp[0:1] * lp[1:2], axis=-1, keepdims=True))
           - jnp.exp(jnp.sum(lp[2:3] * lp[3:4], axis=-1, keepdims=True)) + LAMBDA_INIT)
    gain = gain_ref[...]
    for sb in range(nsub):
        ot = acc_ref[sb] * (1.0 / ls[sb]) - acc_ref[nsub + sb] * (lam / ls[nsub + sb])
        ot = ot * lax.rsqrt(jnp.mean(ot * ot, axis=0, keepdims=True) + NORM_EPS) * gain
        o_ref[sb * ATTN_QSUB:(sb + 1) * ATTN_QSUB, :] = (ot * (1.0 - LAMBDA_INIT)).T.astype(BF16)


def _diff_attn(q2, kr, vt, lam_params, gain_col, seq):
    rows = kr.shape[0]
    bq = ATTN_BLOCK
    assert seq % bq == 0 and bq == 2 * ATTN_QSUB
    return pl.pallas_call(
        _diff_attn_kernel,
        grid=(DIFF_HEADS, seq // bq),
        in_specs=[
            pl.BlockSpec((2, bq, DIFF_VDIM), lambda h, i: (0, i, h)),
            pl.BlockSpec((rows, DIFF_VDIM), lambda h, i: (0, h)),
            pl.BlockSpec((rows // KV_TILE, DIFF_VDIM, KV_TILE), lambda h, i: (0, h, 0)),
            pl.BlockSpec((4, DIFF_DIM), lambda h, i: (0, 0)),
            pl.BlockSpec((DIFF_VDIM, 1), lambda h, i: (0, 0)),
        ],
        out_specs=pl.BlockSpec((bq, DIFF_VDIM), lambda h, i: (i, h)),
        out_shape=jax.ShapeDtypeStruct((seq, DIFF_WIDTH), BF16),
        scratch_shapes=[pltpu.VMEM((4, DIFF_VDIM, ATTN_QSUB), F32),
                        pltpu.VMEM((2, 4, ATTN_QSUB, ATTN_QSUB), F32)],
        compiler_params=_params(("parallel", "arbitrary")),
        name="diff_attn",
    )(q2, kr, vt, lam_params, gain_col)


def _out_proj_kernel(mg_ref, md_ref, wg_ref, wd_ref, h_ref, gain_ref, h2_ref, n2_ref):
    h2 = h_ref[...] + _dot(mg_ref[...], wg_ref[...]) + _dot(md_ref[...], wd_ref[...])
    h2_ref[...] = h2
    ms = jnp.mean(h2 * h2, axis=-1, keepdims=True)
    n2_ref[...] = (h2 * lax.rsqrt(ms + NORM_EPS) * gain_ref[...]).astype(BF16)


def _out_proj(mix_g, mix_d, w_out, h, gain, seq):
    tm = _pick(seq, (512, 128))
    return pl.pallas_call(
        _out_proj_kernel,
        grid=(seq // tm,),
        in_specs=[
            pl.BlockSpec((tm, GDN_WIDTH), lambda i: (i, 0)),
            pl.BlockSpec((tm, DIFF_WIDTH), lambda i: (i, 0)),
            pl.BlockSpec((GDN_WIDTH, D_MODEL), lambda i: (0, 0)),
            pl.BlockSpec((DIFF_WIDTH, D_MODEL), lambda i: (1, 0)),
            pl.BlockSpec((tm, D_MODEL), lambda i: (i, 0)),
            pl.BlockSpec((1, D_MODEL), lambda i: (0, 0)),
        ],
        out_specs=[
            pl.BlockSpec((tm, D_MODEL), lambda i: (i, 0)),
            pl.BlockSpec((tm, D_MODEL), lambda i: (i, 0)),
        ],
        out_shape=[
            jax.ShapeDtypeStruct((seq, D_MODEL), F32),
            jax.ShapeDtypeStruct((seq, D_MODEL), BF16),
        ],
        compiler_params=_params(("parallel",)),
        name="out_proj",
    )(mix_g, mix_d, w_out, w_out, h, gain)


def _gate_up_kernel(n_ref, wg_ref, wu_ref, a_ref, wg16_ref, wu16_ref):
    @pl.when(pl.program_id(1) == 0)
    def _():
        wg16_ref[...] = wg_ref[...].astype(BF16)
        wu16_ref[...] = wu_ref[...].astype(BF16)

    n = n_ref[...]
    g = _dot(n, wg16_ref[...])
    u = _dot(n, wu16_ref[...])
    a_ref[...] = (_silu(g) * u).astype(BF16)


def _gate_up(n2, w_gu):
    seq = n2.shape[0]
    tm = _pick(seq, (1024, 128))
    tn = 512
    nt = D_FF // tn
    return pl.pallas_call(
        _gate_up_kernel,
        grid=(nt, seq // tm),
        in_specs=[
            pl.BlockSpec((tm, D_MODEL), lambda j, i: (i, 0)),
            pl.BlockSpec((D_MODEL, tn), lambda j, i: (0, j)),
            pl.BlockSpec((D_MODEL, tn), lambda j, i: (0, j + nt)),
        ],
        out_specs=pl.BlockSpec((tm, tn), lambda j, i: (i, j)),
        out_shape=jax.ShapeDtypeStruct((seq, D_FF), BF16),
        scratch_shapes=[pltpu.VMEM((D_MODEL, tn), BF16), pltpu.VMEM((D_MODEL, tn), BF16)],
        compiler_params=_params(("parallel", "arbitrary")),
        name="ffn_gate_up",
    )(n2, w_gu, w_gu)


def _down_kernel(a_ref, w_ref, h_ref, o_ref):
    o_ref[...] = h_ref[...] + _dot(a_ref[...], w_ref[...])


def _down(act, w_down, h2):
    seq = act.shape[0]
    tm = _pick(seq, (512, 128))
    tn = 1024
    return pl.pallas_call(
        _down_kernel,
        grid=(D_MODEL // tn, seq // tm),
        in_specs=[
            pl.BlockSpec((tm, D_FF), lambda j, i: (i, 0)),
            pl.BlockSpec((D_FF, tn), lambda j, i: (0, j)),
            pl.BlockSpec((tm, tn), lambda j, i: (i, j)),
        ],
        out_specs=pl.BlockSpec((tm, tn), lambda j, i: (i, j)),
        out_shape=jax.ShapeDtypeStruct((seq, D_MODEL), F32),
        compiler_params=_params(("parallel", "parallel")),
        name="ffn_down",
    )(act, w_down, h2)


def _rope_tables(seq):
    half = DIFF_DIM // 2
    pos = jnp.concatenate([jnp.arange(seq) + N_META, jnp.zeros((META_BLOCK - N_META,), jnp.int32),
                           jnp.arange(N_META)]).astype(F32)
    inv_freq = ROPE_THETA ** (-jnp.arange(half, dtype=F32) / half)
    ang = pos[:, None] * inv_freq[None, :]
    cos = jnp.tile(jnp.cos(ang), (1, LANES // half))
    sin = jnp.sin(ang)
    sin = jnp.tile(jnp.concatenate([-sin, sin], axis=1), (1, LANES // DIFF_DIM))
    return cos, sin


def _lane_pad(v, offset):
    return jnp.zeros((1, GATE_LANES), F32).at[0, offset:offset + v.shape[0]].set(v.astype(F32))


def kernel(x, meta_tokens, attn_norm, w_in, conv_w, a_log, dt_bias, gdn_norm, q_norm, k_norm,
           lambda_q1, lambda_k1, lambda_q2, lambda_k2, diff_norm, w_out, ffn_norm, w_gate_up, w_down):
    assert x.shape[0] == 1 and x.shape[2] == D_MODEL
    seq = x.shape[1]
    assert seq % META_BLOCK == 0
    xs = x[0]
    meta_block = jnp.concatenate([jnp.zeros((META_BLOCK - N_META, D_MODEL), xs.dtype),
                                  meta_tokens.astype(xs.dtype)], axis=0)

    gdn_cols = 4 * GDN_WIDTH
    wt_in = w_in[0].T
    wt_diff = wt_in[gdn_cols + 2 * GDN_HEADS:]
    wt_ba = jnp.pad(wt_in[gdn_cols:gdn_cols + 2 * GDN_HEADS],
                    ((0, GATE_LANES - 2 * GDN_HEADS), (0, 0))).astype(BF16)

    n1, gcol, grow = _prenorm_gate(xs, meta_block, attn_norm, wt_ba, _lane_pad(a_log[0], GDN_HEADS),
                                   _lane_pad(dt_bias[0], GDN_HEADS))
    proj = _in_proj(n1, wt_in, wt_diff)

    w, u, qg, kd, attn, egl = _gdn_local(proj, conv_w[0], gcol, grow, seq)
    mix_g = _gdn_state(w, u, qg, kd, attn, egl, proj, gdn_norm, seq)

    cos, sin = _rope_tables(seq)
    tile2 = lambda g: jnp.tile(g.astype(F32), (1, LANES // DIFF_DIM))
    lane = np.arange(LANES)
    gsum = jnp.asarray((lane[:, None] // DIFF_DIM) == (lane[None, :] // DIFF_DIM), BF16)
    q2, kr, vt = _attn_prep(proj, cos, sin, tile2(q_norm), tile2(k_norm), gsum, seq)
    lam_params = jnp.concatenate([lambda_q1, lambda_k1, lambda_q2, lambda_k2], axis=0).astype(F32)
    mix_d = _diff_attn(q2, kr, vt, lam_params, diff_norm.astype(F32).reshape(DIFF_VDIM, 1), seq)

    h2, n2 = _out_proj(mix_g, mix_d, w_out[0].astype(BF16), xs, ffn_norm, seq)
    act = _gate_up(n2, w_gate_up[0])
    out = _down(act, w_down[0].astype(BF16), h2)
    return out[None]
```

```python
import functools
import math

import jax
import jax.numpy as jnp
import numpy as np
from jax import lax
from jax.experimental import pallas as pl
from jax.experimental.pallas import tpu as pltpu

F32 = jnp.float32
BF16 = jnp.bfloat16

D_MODEL = 2048
N_META = 16
GDN_HEADS = 8
GDN_DIM = 128
GDN_WIDTH = GDN_HEADS * GDN_DIM
CONV_WIDTH = 4
CHUNK = 64
DIFF_HEADS = 8
DIFF_DIM = 64
DIFF_VDIM = 2 * DIFF_DIM
DIFF_WIDTH = DIFF_HEADS * DIFF_VDIM
ROPE_THETA = 10000.0
D_FF = 5632
NORM_EPS = 1e-6
MASK_VALUE = -1e30
LAMBDA_INIT = 0.8 - 0.6 * math.exp(-0.3 * 0)

LANES = 128
META_BLOCK = 512
GATE_LANES = 128
VMEM_LIMIT = 56 * 1024 * 1024


def _pick(n, candidates):
    for c in candidates:
        if n % c == 0:
            return c
    raise ValueError(f"no tile in {candidates} divides {n}")


def _params(sem, vmem=VMEM_LIMIT):
    return pltpu.CompilerParams(dimension_semantics=sem, vmem_limit_bytes=vmem)


def _dot(a, b):
    return jnp.dot(a, b, preferred_element_type=F32)


def _dot_nt(a, b):
    return lax.dot_general(a, b, (((1,), (1,)), ((), ())), preferred_element_type=F32)


def _dot_tn(a, b):
    return lax.dot_general(a, b, (((0,), (0,)), ((), ())), preferred_element_type=F32)


def _softplus(x):
    return jnp.maximum(x, 0.0) + jnp.log1p(jnp.exp(-jnp.abs(x)))


def _silu(x):
    return x * jax.nn.sigmoid(x)


def _prenorm_gate_kernel(x_ref, mb_ref, gain_ref, wba_ref, alog_ref, dtb_ref, n_ref, gcol_ref, grow_ref):
    h = jnp.where(pl.program_id(0) < pl.num_programs(0) - 1, x_ref[...], mb_ref[...])
    ms = jnp.mean(h * h, axis=-1, keepdims=True)
    n = (h * lax.rsqrt(ms + NORM_EPS) * gain_ref[...]).astype(BF16)
    n_ref[...] = n
    ba = _dot_nt(n, wba_ref[...])
    beta = jax.nn.sigmoid(ba)
    g = -jnp.exp(alog_ref[...]) * _softplus(ba + dtb_ref[...])
    row = lax.broadcasted_iota(jnp.int32, ba.shape, 0) % CHUNK
    gc = g
    for d in (1, 2, 4, 8, 16, 32):
        gc = gc + jnp.where(row >= d, pltpu.roll(gc, d, axis=0), 0.0)
    lane = lax.broadcasted_iota(jnp.int32, ba.shape, 1)
    out = jnp.where(lane < GDN_HEADS, beta, gc)
    gcol_ref[...] = out
    grow_ref[...] = out.T[: 2 * GDN_HEADS]


def _prenorm_gate(x, meta_block, gain, wba, alog, dtb):
    tm = META_BLOCK
    nx = x.shape[0] // tm
    rows = x.shape[0] + tm
    return pl.pallas_call(
        _prenorm_gate_kernel,
        grid=(nx + 1,),
        in_specs=[
            pl.BlockSpec((tm, D_MODEL), lambda i: (jnp.minimum(i, nx - 1), 0)),
            pl.BlockSpec((tm, D_MODEL), lambda i: (0, 0)),
            pl.BlockSpec((1, D_MODEL), lambda i: (0, 0)),
            pl.BlockSpec((GATE_LANES, D_MODEL), lambda i: (0, 0)),
            pl.BlockSpec((1, GATE_LANES), lambda i: (0, 0)),
            pl.BlockSpec((1, GATE_LANES), lambda i: (0, 0)),
        ],
        out_specs=[
            pl.BlockSpec((tm, D_MODEL), lambda i: (i, 0)),
            pl.BlockSpec((tm, GATE_LANES), lambda i: (i, 0)),
            pl.BlockSpec((2 * GDN_HEADS, tm), lambda i: (0, i)),
        ],
        out_shape=[
            jax.ShapeDtypeStruct((rows, D_MODEL), BF16),
            jax.ShapeDtypeStruct((rows, GATE_LANES), F32),
            jax.ShapeDtypeStruct((2 * GDN_HEADS, rows), F32),
        ],
        compiler_params=_params(("parallel",)),
        name="prenorm_gate",
    )(x, meta_block, gain, wba, alog, dtb)


IN_PROJ_TN = 1024
GDN_COL_TILES = 4 * GDN_WIDTH // IN_PROJ_TN


def _in_proj_kernel(a_ref, wg_ref, wd_ref, o_ref):
    j = pl.program_id(0)

    @pl.when(j < GDN_COL_TILES)
    def _():
        o_ref[...] = _dot_nt(a_ref[...], wg_ref[...].astype(BF16))

    @pl.when(j >= GDN_COL_TILES)
    def _():
        o_ref[...] = _dot_nt(a_ref[...], wd_ref[...].astype(BF16))


def _in_proj(n1, wt_all, wt_diff):
    m = n1.shape[0]
    tm, tn = META_BLOCK, IN_PROJ_TN
    n = GDN_COL_TILES * tn + wt_diff.shape[0]
    return pl.pallas_call(
        _in_proj_kernel,
        grid=(n // tn, m // tm),
        in_specs=[
            pl.BlockSpec((tm, D_MODEL), lambda j, i: (i, 0)),
            pl.BlockSpec((tn, D_MODEL), lambda j, i: (jnp.minimum(j, GDN_COL_TILES - 1), 0)),
            pl.BlockSpec((tn, D_MODEL), lambda j, i: (jnp.maximum(j - GDN_COL_TILES, 0), 0)),
        ],
        out_specs=pl.BlockSpec((tm, tn), lambda j, i: (i, j)),
        out_shape=jax.ShapeDtypeStruct((m, n), F32),
        compiler_params=_params(("parallel", "parallel")),
        name="in_proj",
    )(n1, wt_all, wt_diff)


GDN_LOCAL_CHUNKS = 4


def _causal_conv_silu(x, prev, w):
    r8 = lax.broadcasted_iota(jnp.int32, prev.shape, 0)
    y = x * w[CONV_WIDTH - 1:CONV_WIDTH]
    for d in range(1, CONV_WIDTH):
        shifted = pltpu.roll(x, d, axis=0)
        top = jnp.where(r8 < d, pltpu.roll(prev, d, axis=0), shifted[:8])
        shifted = jnp.concatenate([top, shifted[8:]], axis=0)
        y = y + shifted * w[CONV_WIDTH - 1 - d:CONV_WIDTH - d]
    return _silu(y)


def _gdn_local_kernel(q_ref, k_ref, v_ref, qp_ref, kp_ref, vp_ref, cw_ref, gcol_ref, grow_ref,
                      w_ref, u_ref, qg_ref, kd_ref, attn_ref, egl_ref):
    cw = cw_ref[...]
    q_all = _causal_conv_silu(q_ref[...], qp_ref[...], cw[:, 0:GDN_WIDTH])
    k_all = _causal_conv_silu(k_ref[...], kp_ref[...], cw[:, GDN_WIDTH:2 * GDN_WIDTH])
    v_all = _causal_conv_silu(v_ref[...], vp_ref[...], cw[:, 2 * GDN_WIDTH:3 * GDN_WIDTH])
    gcol = gcol_ref[...]
    grow = grow_ref[...]
    ii = lax.broadcasted_iota(jnp.int32, (CHUNK, CHUNK), 0)
    jj = lax.broadcasted_iota(jnp.int32, (CHUNK, CHUNK), 1)
    units = [(c, h) for c in range(GDN_LOCAL_CHUNKS) for h in range(GDN_HEADS)]
    rows_of = lambda c: slice(c * CHUNK, (c + 1) * CHUNK)
    lanes_of = lambda h: slice(h * GDN_DIM, (h + 1) * GDN_DIM)
    qs, ks, kbs, vbs, egcs, gcs, kqs = [], [], [], [], [], [], []
    for c, h in units:
        rs, ls = rows_of(c), lanes_of(h)
        q = q_all[rs, ls]
        k = k_all[rs, ls]
        q = q * lax.rsqrt(jnp.sum(q * q, axis=-1, keepdims=True) + NORM_EPS) * (GDN_DIM ** -0.5)
        k = k * lax.rsqrt(jnp.sum(k * k, axis=-1, keepdims=True) + NORM_EPS)
        beta_c = gcol[rs, h:h + 1]
        gc_c = gcol[rs, GDN_HEADS + h:GDN_HEADS + h + 1]
        kb = k * beta_c
        qs.append(q)
        ks.append(k)
        kbs.append(kb)
        vbs.append(v_all[rs, ls] * beta_c)
        gcs.append(gc_c)
        egcs.append(jnp.exp(gc_c))
        kqs.append(_dot_nt(jnp.concatenate([kb.astype(BF16), q.astype(BF16)], axis=0), k.astype(BF16)))
    lms, attns = [], []
    for (c, h), kq, gc_c in zip(units, kqs, gcs):
        gc_r = grow[GDN_HEADS + h:GDN_HEADS + h + 1, rows_of(c)]
        decay = jnp.exp(jnp.where(ii >= jj, gc_c - gc_r, MASK_VALUE))
        lms.append(jnp.where(ii > jj, kq[:CHUNK] * decay, 0.0))
        attns.append(kq[CHUNK:] * decay)
    xor = ii ^ jj
    eye = jnp.where(ii == jj, 1.0, 0.0)
    xs = [eye - jnp.where(xor == 1, lm, 0.0) for lm in lms]
    level = 1
    while (2 << level) <= CHUNK:
        sel = (xor >> level) == 1
        ys = [_dot(jnp.where(sel, lm, 0.0).astype(BF16), x.astype(BF16)) for lm, x in zip(lms, xs)]
        xs = [x - _dot(x.astype(BF16), y.astype(BF16)) for x, y in zip(xs, ys)]
        level += 1
    uws = [_dot(x.astype(BF16), jnp.concatenate([vb.astype(BF16), (kb * egc).astype(BF16)], axis=1))
           for x, vb, kb, egc in zip(xs, vbs, kbs, egcs)]
    for (c, h), uw, q, k, egc, gc_c, attn in zip(units, uws, qs, ks, egcs, gcs, attns):
        rs, ls = rows_of(c), lanes_of(h)
        u_ref[rs, ls] = uw[:, :GDN_DIM]
        w_ref[rs, ls] = uw[:, GDN_DIM:].astype(BF16)
        qg_ref[rs, ls] = (q * egc).astype(BF16)
        gc_last = gc_c[CHUNK - 1:CHUNK]
        kd_ref[rs, ls] = (k * jnp.exp(gc_last - gc_c)).astype(BF16)
        attn_ref[h, rs, :] = attn.astype(BF16)
        egl_ref[c, h:h + 1, :] = jnp.broadcast_to(jnp.exp(gc_last), (1, GDN_DIM))


def _gdn_local(proj, conv_w, gcol, grow):
    rows = proj.shape[0]
    rb = GDN_LOCAL_CHUNKS * CHUNK
    bidx = lambda i: i
    n8 = rows // 8
    blk = lambda col: pl.BlockSpec((rb, GDN_WIDTH), lambda i, col=col: (bidx(i), col))
    prev = lambda col: pl.BlockSpec((8, GDN_WIDTH),
                                    lambda i, col=col: ((bidx(i) * (rb // 8) + n8 - 1) % n8, col))
    row_out = lambda dt: jax.ShapeDtypeStruct((rows, GDN_WIDTH), dt)
    return pl.pallas_call(
        _gdn_local_kernel,
        grid=(rows // rb,),
        in_specs=[
            blk(0), blk(1), blk(2), prev(0), prev(1), prev(2),
            pl.BlockSpec((CONV_WIDTH, 3 * GDN_WIDTH), lambda i: (0, 0)),
            pl.BlockSpec((rb, GATE_LANES), lambda i: (bidx(i), 0)),
            pl.BlockSpec((2 * GDN_HEADS, rb), lambda i: (0, bidx(i))),
        ],
        out_specs=[
            pl.BlockSpec((rb, GDN_WIDTH), lambda i: (bidx(i), 0)),
            pl.BlockSpec((rb, GDN_WIDTH), lambda i: (bidx(i), 0)),
            pl.BlockSpec((rb, GDN_WIDTH), lambda i: (bidx(i), 0)),
            pl.BlockSpec((rb, GDN_WIDTH), lambda i: (bidx(i), 0)),
            pl.BlockSpec((GDN_HEADS, rb, CHUNK), lambda i: (0, bidx(i), 0)),
            pl.BlockSpec((GDN_LOCAL_CHUNKS, GDN_HEADS, GDN_DIM), lambda i: (bidx(i), 0, 0)),
        ],
        out_shape=[
            row_out(BF16),
            row_out(F32),
            row_out(BF16),
            row_out(BF16),
            jax.ShapeDtypeStruct((GDN_HEADS, rows, CHUNK), BF16),
            jax.ShapeDtypeStruct((rows // CHUNK, GDN_HEADS, GDN_DIM), F32),
        ],
        compiler_params=_params(("parallel",)),
        name="gdn_local",
    )(proj, proj, proj, proj, proj, proj, conv_w, gcol, grow)


GDN_STATE_CHUNKS = 4


def _gdn_state_kernel(w_ref, u_ref, qg_ref, kd_ref, attn_ref, egl_ref, z_ref, gain_ref, o_ref, s_ref):
    @pl.when(pl.program_id(0) == 0)
    def _():
        s_ref[...] = jnp.zeros_like(s_ref)

    gain = gain_ref[...]
    heads = range(GDN_HEADS)
    lanes = [slice(h * GDN_DIM, (h + 1) * GDN_DIM) for h in heads]
    ss = [s_ref[h] for h in heads]
    for ck in range(GDN_STATE_CHUNKS):
        rw = slice(ck * CHUNK, (ck + 1) * CHUNK)
        rs = [_dot(jnp.concatenate([w_ref[rw, ls], qg_ref[rw, ls]], axis=0), s.astype(BF16))
              for ls, s in zip(lanes, ss)]
        vns = [(u_ref[rw, ls] - r[:CHUNK]).astype(BF16) for ls, r in zip(lanes, rs)]
        os_ = [r[CHUNK:] + _dot(attn_ref[h, rw, :], vn) for h, r, vn in zip(heads, rs, vns)]
        ss = [s * egl_ref[ck, h:h + 1, :] + _dot_tn(kd_ref[rw, ls], vn)
              for h, ls, s, vn in zip(heads, lanes, ss, vns)]
        for ls, o in zip(lanes, os_):
            on = o * lax.rsqrt(jnp.mean(o * o, axis=-1, keepdims=True) + NORM_EPS) * gain
            o_ref[rw, ls] = (on * _silu(z_ref[rw, ls])).astype(BF16)
    for h, s in zip(heads, ss):
        s_ref[h] = s


def _gdn_state(w, u, qg, kd, attn, egl, proj, gain, seq):
    rows = w.shape[0]
    rb = GDN_STATE_CHUNKS * CHUNK
    nblocks = rows // rb
    phys = lambda c: (c + nblocks - 1) % nblocks
    rowblk = lambda col: pl.BlockSpec((rb, GDN_WIDTH), lambda c, col=col: (phys(c), col))
    return pl.pallas_call(
        _gdn_state_kernel,
        grid=(seq // rb + 1,),
        in_specs=[
            rowblk(0), rowblk(0), rowblk(0), rowblk(0),
            pl.BlockSpec((GDN_HEADS, rb, CHUNK), lambda c: (0, phys(c), 0)),
            pl.BlockSpec((GDN_STATE_CHUNKS, GDN_HEADS, GDN_DIM), lambda c: (phys(c), 0, 0)),
            rowblk(3),
            pl.BlockSpec((1, GDN_DIM), lambda c: (0, 0)),
        ],
        out_specs=pl.BlockSpec((rb, GDN_WIDTH), lambda c: (jnp.maximum(c - 1, 0), 0)),
        out_shape=jax.ShapeDtypeStruct((seq, GDN_WIDTH), BF16),
        scratch_shapes=[pltpu.VMEM((GDN_HEADS, GDN_DIM, GDN_DIM), F32)],
        compiler_params=_params(("arbitrary",)),
        name="gdn_state",
    )(w, u, qg, kd, attn, egl, proj, gain)


KV_TILE = 128
Q_SCALE = DIFF_DIM ** -0.5 * math.log2(math.e)


def _attn_prep_kernel(q_ref, k_ref, v_ref, cos_ref, sin_ref, qg_ref, kg_ref, gsum_ref,
                      q2_ref, kr_ref, vt_ref):
    cos = cos_ref[...]
    sin = sin_ref[...]
    gsum = gsum_ref[...]
    lane = lax.broadcasted_iota(jnp.int32, cos.shape, 1)
    first_half = (lane % DIFF_DIM) < (DIFF_DIM // 2)
    low_map = lane < DIFF_DIM

    def norm_rope(x, gain):
        ms = _dot((x * x).astype(BF16), gsum) * (1.0 / DIFF_DIM)
        xn = x * lax.rsqrt(ms + NORM_EPS) * gain
        rot = jnp.where(first_half, pltpu.roll(xn, LANES - DIFF_DIM // 2, axis=1),
                        pltpu.roll(xn, DIFF_DIM // 2, axis=1))
        return xn * cos + rot * sin

    for h in range(DIFF_HEADS):
        ls = slice(h * DIFF_VDIM, (h + 1) * DIFF_VDIM)
        q = norm_rope(q_ref[:, ls], qg_ref[...]) * Q_SCALE
        q2_ref[0, :, ls] = jnp.where(low_map, q, 0.0).astype(BF16)
        q2_ref[1, :, ls] = jnp.where(low_map, 0.0, q).astype(BF16)
        kr_ref[:, ls] = norm_rope(k_ref[:, ls], kg_ref[...]).astype(BF16)
        vt_ref[0, ls, :] = v_ref[:, ls].T.astype(BF16)


def _attn_prep(proj, cos, sin, qgain, kgain, gsum):
    rows = proj.shape[0]
    tm = KV_TILE
    bidx = lambda i: i
    col = lambda c: pl.BlockSpec((tm, DIFF_WIDTH), lambda i, c=c: (bidx(i), c))
    small = lambda shape: pl.BlockSpec(shape, lambda i: (0,) * len(shape))
    return pl.pallas_call(
        _attn_prep_kernel,
        grid=(rows // tm,),
        in_specs=[
            col(4), col(5), col(6),
            pl.BlockSpec((tm, LANES), lambda i: (bidx(i), 0)),
            pl.BlockSpec((tm, LANES), lambda i: (bidx(i), 0)),
            small((1, LANES)), small((1, LANES)), small((LANES, LANES)),
        ],
        out_specs=[
            pl.BlockSpec((2, tm, DIFF_WIDTH), lambda i: (0, bidx(i), 0)),
            pl.BlockSpec((tm, DIFF_WIDTH), lambda i: (bidx(i), 0)),
            pl.BlockSpec((1, DIFF_WIDTH, tm), lambda i: (bidx(i), 0, 0)),
        ],
        out_shape=[
            jax.ShapeDtypeStruct((2, rows, DIFF_WIDTH), BF16),
            jax.ShapeDtypeStruct((rows, DIFF_WIDTH), BF16),
            jax.ShapeDtypeStruct((rows // tm, DIFF_WIDTH, tm), BF16),
        ],
        compiler_params=_params(("parallel",)),
        name="attn_prep",
    )(proj, proj, proj, cos, sin, qgain, kgain, gsum)


ATTN_BLOCK = 1024
ATTN_QSUB = 256

def _diff_attn_kernel(q_ref, k_ref, vt_ref, lam_ref, gain_ref, o_ref, acc_ref, st_ref):
    i = pl.program_id(1)
    rows = k_ref.shape[0]
    bk = ATTN_QSUB
    nsub = q_ref.shape[1] // ATTN_QSUB
    chains = [(mp, sb) for mp in range(2) for sb in range(nsub)]
    every = list(range(len(chains)))
    qs = [q_ref[mp, sb * ATTN_QSUB:(sb + 1) * ATTN_QSUB, :] for mp, sb in chains]
    kv_tiles = bk // KV_TILE

    def update(sts, vt, ms, ls, which):
        ms, ls = list(ms), list(ls)
        first = ms[which[0]] is None
        cms = [jnp.max(st, axis=0, keepdims=True) for st in sts]
        m_new = cms if first else [jnp.maximum(ms[c], cm) for c, cm in zip(which, cms)]
        ps = [jnp.exp2(st - mn) for st, mn in zip(sts, m_new)]
        pvs = [_dot(vt, p.astype(BF16)) for p in ps]
        psums = [jnp.sum(p, axis=0, keepdims=True) for p in ps]
        for n, c in enumerate(which):
            if first:
                ls[c] = psums[n]
                acc_ref[c] = pvs[n]
            else:
                alpha = jnp.exp2(ms[c] - m_new[n])
                ls[c] = alpha * ls[c] + psums[n]
                acc_ref[c] = alpha * acc_ref[c] + pvs[n]
            ms[c] = m_new[n]
        return ms, ls

    def store_scores(j, slot, which):
        start = j * bk if isinstance(j, int) else pl.multiple_of(j * bk, bk)
        k_c = k_ref[pl.ds(start, bk), :]
        for c in which:
            st_ref[slot, c] = _dot_nt(k_c, qs[c])

    def values_t(j):
        return jnp.concatenate([vt_ref[j * kv_tiles + t] for t in range(kv_tiles)], axis=1)

    k_meta = k_ref[rows - KV_TILE:rows, :]
    key = lax.broadcasted_iota(jnp.int32, (KV_TILE, ATTN_QSUB), 0)
    sts = [jnp.where(key >= KV_TILE - N_META, _dot_nt(k_meta, q), MASK_VALUE) for q in qs]
    store_scores(0, 0, every)
    none = [None] * len(chains)
    ms, ls = update(sts, vt_ref[rows // KV_TILE - 1], none, none, every)

    def full_blocks(t, carry):
        ms, ls = carry
        for n in range(nsub):
            j = nsub * t + n
            store_scores(j + 1, (n + 1) % 2, every)
            ms, ls = update([st_ref[n % 2, c] for c in every], values_t(j), ms, ls, every)
        return tuple(ms), tuple(ls)

    ms, ls = lax.fori_loop(0, i, full_blocks, (tuple(ms), tuple(ls)))
    tri = (lax.broadcasted_iota(jnp.int32, (bk, ATTN_QSUB), 0)
           <= lax.broadcasted_iota(jnp.int32, (bk, ATTN_QSUB), 1))
    for d in range(nsub):
        if d + 1 < nsub:
            store_scores(i * nsub + d + 1, (d + 1) % 2, [c for c in every if chains[c][1] > d])
        which = [c for c in every if chains[c][1] >= d]
        sts = [jnp.where(tri, st_ref[d % 2, c], MASK_VALUE) if chains[c][1] == d else st_ref[d % 2, c]
               for c in which]
        ms, ls = update(sts, values_t(i * nsub + d), ms, ls, which)

    lp = lam_ref[...]
    lam = (jnp.exp(jnp.sum(lp[0:1] * lp[1:2], axis=-1, keepdims=True))
           - jnp.exp(jnp.sum(lp[2:3] * lp[3:4], axis=-1, keepdims=True)) + LAMBDA_INIT)
    gain = gain_ref[...]
    for sb in range(nsub):
        ot = acc_ref[sb] * (1.0 / ls[sb]) - acc_ref[nsub + sb] * (lam / ls[nsub + sb])
        ot = ot * lax.rsqrt(jnp.mean(ot * ot, axis=0, keepdims=True) + NORM_EPS) * gain
        o_ref[sb * ATTN_QSUB:(sb + 1) * ATTN_QSUB, :] = (ot * (1.0 - LAMBDA_INIT)).T.astype(BF16)


def _diff_attn(q2, kr, vt, lam_params, gain_col, seq):
    rows = kr.shape[0]
    bq = ATTN_BLOCK
    nchains = 2 * (bq // ATTN_QSUB)
    assert seq % bq == 0 and bq % (2 * ATTN_QSUB) == 0
    return pl.pallas_call(
        _diff_attn_kernel,
        grid=(DIFF_HEADS, seq // bq),
        in_specs=[
            pl.BlockSpec((2, bq, DIFF_VDIM), lambda h, i: (0, i, h)),
            pl.BlockSpec((rows, DIFF_VDIM), lambda h, i: (0, h)),
            pl.BlockSpec((rows // KV_TILE, DIFF_VDIM, KV_TILE), lambda h, i: (0, h, 0)),
            pl.BlockSpec((4, DIFF_DIM), lambda h, i: (0, 0)),
            pl.BlockSpec((DIFF_VDIM, 1), lambda h, i: (0, 0)),
        ],
        out_specs=pl.BlockSpec((bq, DIFF_VDIM), lambda h, i: (i, h)),
        out_shape=jax.ShapeDtypeStruct((seq, DIFF_WIDTH), BF16),
        scratch_shapes=[pltpu.VMEM((nchains, DIFF_VDIM, ATTN_QSUB), F32),
                        pltpu.VMEM((2, nchains, ATTN_QSUB, ATTN_QSUB), F32)],
        compiler_params=_params(("parallel", "arbitrary")),
        name="diff_attn",
    )(q2, kr, vt, lam_params, gain_col)


def _out_proj_kernel(mg_ref, md_ref, wg_ref, wd_ref, h_ref, gain_ref, h2_ref, n2_ref):
    h2 = (h_ref[...] + _dot(mg_ref[...], wg_ref[...].astype(BF16))
          + _dot(md_ref[...], wd_ref[...].astype(BF16)))
    h2_ref[...] = h2
    ms = jnp.mean(h2 * h2, axis=-1, keepdims=True)
    n2_ref[...] = (h2 * lax.rsqrt(ms + NORM_EPS) * gain_ref[...]).astype(BF16)


def _out_proj(mix_g, mix_d, w_out, h, gain, seq):
    tm = _pick(seq, (512, 128))
    return pl.pallas_call(
        _out_proj_kernel,
        grid=(seq // tm,),
        in_specs=[
            pl.BlockSpec((tm, GDN_WIDTH), lambda i: (i, 0)),
            pl.BlockSpec((tm, DIFF_WIDTH), lambda i: (i, 0)),
            pl.BlockSpec((GDN_WIDTH, D_MODEL), lambda i: (0, 0), pipeline_mode=pl.Buffered(1)),
            pl.BlockSpec((DIFF_WIDTH, D_MODEL), lambda i: (1, 0), pipeline_mode=pl.Buffered(1)),
            pl.BlockSpec((tm, D_MODEL), lambda i: (i, 0)),
            pl.BlockSpec((1, D_MODEL), lambda i: (0, 0)),
        ],
        out_specs=[
            pl.BlockSpec((tm, D_MODEL), lambda i: (i, 0)),
            pl.BlockSpec((tm, D_MODEL), lambda i: (i, 0)),
        ],
        out_shape=[
            jax.ShapeDtypeStruct((seq, D_MODEL), F32),
            jax.ShapeDtypeStruct((seq, D_MODEL), BF16),
        ],
        compiler_params=_params(("parallel",)),
        name="out_proj",
    )(mix_g, mix_d, w_out, w_out, h, gain)


def _gate_up_kernel(n_ref, wg_ref, wu_ref, a_ref):
    n = n_ref[...]
    g = _dot(n, wg_ref[...].astype(BF16))
    u = _dot(n, wu_ref[...].astype(BF16))
    a_ref[...] = (_silu(g) * u).astype(BF16)


def _gate_up(n2, w_gu):
    seq = n2.shape[0]
    tm = _pick(seq, (1024, 128))
    tn = 512
    nt = D_FF // tn
    return pl.pallas_call(
        _gate_up_kernel,
        grid=(nt, seq // tm),
        in_specs=[
            pl.BlockSpec((tm, D_MODEL), lambda j, i: (i, 0)),
            pl.BlockSpec((D_MODEL, tn), lambda j, i: (0, j)),
            pl.BlockSpec((D_MODEL, tn), lambda j, i: (0, j + nt)),
        ],
        out_specs=pl.BlockSpec((tm, tn), lambda j, i: (i, j)),
        out_shape=jax.ShapeDtypeStruct((seq, D_FF), BF16),
        compiler_params=_params(("parallel", "parallel")),
        name="ffn_gate_up",
    )(n2, w_gu, w_gu)


def _down_kernel(a_ref, w_ref, h_ref, o_ref):
    o_ref[...] = h_ref[...] + _dot(a_ref[...], w_ref[...].astype(BF16))


def _down(act, w_down, h2):
    seq = act.shape[0]
    tm = _pick(seq, (512, 128))
    tn = 512
    return pl.pallas_call(
        _down_kernel,
        grid=(D_MODEL // tn, seq // tm),
        in_specs=[
            pl.BlockSpec((tm, D_FF), lambda j, i: (i, 0)),
            pl.BlockSpec((D_FF, tn), lambda j, i: (0, j)),
            pl.BlockSpec((tm, tn), lambda j, i: (i, j)),
        ],
        out_specs=pl.BlockSpec((tm, tn), lambda j, i: (i, j)),
        out_shape=jax.ShapeDtypeStruct((seq, D_MODEL), F32),
        compiler_params=_params(("parallel", "parallel")),
        name="ffn_down",
    )(act, w_down, h2)


def _rope_tables(seq):
    half = DIFF_DIM // 2
    pos = jnp.concatenate([jnp.arange(seq) + N_META, jnp.zeros((META_BLOCK - N_META,), jnp.int32),
                           jnp.arange(N_META)]).astype(F32)
    inv_freq = ROPE_THETA ** (-jnp.arange(half, dtype=F32) / half)
    ang = pos[:, None] * inv_freq[None, :]
    cos = jnp.tile(jnp.cos(ang), (1, LANES // half))
    sin = jnp.sin(ang)
    sin = jnp.tile(jnp.concatenate([-sin, sin], axis=1), (1, LANES // DIFF_DIM))
    return cos, sin


def _lane_pad(v, offset):
    return jnp.zeros((1, GATE_LANES), F32).at[0, offset:offset + v.shape[0]].set(v.astype(F32))


def kernel(x, meta_tokens, attn_norm, w_in, conv_w, a_log, dt_bias, gdn_norm, q_norm, k_norm,
           lambda_q1, lambda_k1, lambda_q2, lambda_k2, diff_norm, w_out, ffn_norm, w_gate_up, w_down):
    assert x.shape[0] == 1 and x.shape[2] == D_MODEL
    seq = x.shape[1]
    assert seq % META_BLOCK == 0
    xs = x[0]
    meta_block = jnp.concatenate([jnp.zeros((META_BLOCK - N_META, D_MODEL), xs.dtype),
                                  meta_tokens.astype(xs.dtype)], axis=0)

    gdn_cols = 4 * GDN_WIDTH
    wt_in = w_in[0].T
    wt_diff = wt_in[gdn_cols + 2 * GDN_HEADS:]
    wt_ba = jnp.pad(wt_in[gdn_cols:gdn_cols + 2 * GDN_HEADS],
                    ((0, GATE_LANES - 2 * GDN_HEADS), (0, 0))).astype(BF16)

    n1, gcol, grow = _prenorm_gate(xs, meta_block, attn_norm, wt_ba, _lane_pad(a_log[0], GDN_HEADS),
                                   _lane_pad(dt_bias[0], GDN_HEADS))
    proj = _in_proj(n1, wt_in, wt_diff)

    w, u, qg, kd, attn, egl = _gdn_local(proj, conv_w[0], gcol, grow)
    mix_g = _gdn_state(w, u, qg, kd, attn, egl, proj, gdn_norm, seq)

    cos, sin = _rope_tables(seq)
    tile2 = lambda g: jnp.tile(g.astype(F32), (1, LANES // DIFF_DIM))
    lane = np.arange(LANES)
    gsum = jnp.asarray((lane[:, None] // DIFF_DIM) == (lane[None, :] // DIFF_DIM), BF16)
    q2, kr, vt = _attn_prep(proj, cos, sin, tile2(q_norm), tile2(k_norm), gsum)
    lam_params = jnp.concatenate([lambda_q1, lambda_k1, lambda_q2, lambda_k2], axis=0).astype(F32)
    mix_d = _diff_attn(q2, kr, vt, lam_params, diff_norm.astype(F32).reshape(DIFF_VDIM, 1), seq)

    h2, n2 = _out_proj(mix_g, mix_d, w_out[0], xs, ffn_norm, seq)
    act = _gate_up(n2, w_gate_up[0])
    out = _down(act, w_down[0], h2)
    return out[None]
```

```python
import functools
import math

import jax
import jax.numpy as jnp
import numpy as np
from jax import lax
from jax.experimental import pallas as pl
from jax.experimental.pallas import tpu as pltpu

F32 = jnp.float32
BF16 = jnp.bfloat16

D_MODEL = 2048
N_META = 16
GDN_HEADS = 8
GDN_DIM = 128
GDN_WIDTH = GDN_HEADS * GDN_DIM
CONV_WIDTH = 4
CHUNK = 64
DIFF_HEADS = 8
DIFF_DIM = 64
DIFF_VDIM = 2 * DIFF_DIM
DIFF_WIDTH = DIFF_HEADS * DIFF_VDIM
ROPE_THETA = 10000.0
D_FF = 5632
NORM_EPS = 1e-6
MASK_VALUE = -1e30
LAMBDA_INIT = 0.8 - 0.6 * math.exp(-0.3 * 0)

LANES = 128
META_BLOCK = 512
GATE_LANES = 128
VMEM_LIMIT = 56 * 1024 * 1024


def _pick(n, candidates):
    for c in candidates:
        if n % c == 0:
            return c
    raise ValueError(f"no tile in {candidates} divides {n}")


def _params(sem, vmem=VMEM_LIMIT):
    return pltpu.CompilerParams(dimension_semantics=sem, vmem_limit_bytes=vmem)


def _dot(a, b):
    return jnp.dot(a, b, preferred_element_type=F32)


def _dot_nt(a, b):
    return lax.dot_general(a, b, (((1,), (1,)), ((), ())), preferred_element_type=F32)


def _dot_tn(a, b):
    return lax.dot_general(a, b, (((0,), (0,)), ((), ())), preferred_element_type=F32)


def _softplus(x):
    return jnp.maximum(x, 0.0) + jnp.log1p(jnp.exp(-jnp.abs(x)))


def _silu(x):
    return x * jax.nn.sigmoid(x)


def _prenorm_gate_kernel(x_ref, mb_ref, gain_ref, wba_ref, alog_ref, dtb_ref, n_ref, gcol_ref, grow_ref):
    h = jnp.where(pl.program_id(0) < pl.num_programs(0) - 1, x_ref[...], mb_ref[...])
    ms = jnp.mean(h * h, axis=-1, keepdims=True)
    n = (h * lax.rsqrt(ms + NORM_EPS) * gain_ref[...]).astype(BF16)
    n_ref[...] = n
    ba = _dot_nt(n, wba_ref[...])
    beta = jax.nn.sigmoid(ba)
    g = -jnp.exp(alog_ref[...]) * _softplus(ba + dtb_ref[...])
    row = lax.broadcasted_iota(jnp.int32, ba.shape, 0) % CHUNK
    gc = g
    for d in (1, 2, 4, 8, 16, 32):
        gc = gc + jnp.where(row >= d, pltpu.roll(gc, d, axis=0), 0.0)
    lane = lax.broadcasted_iota(jnp.int32, ba.shape, 1)
    out = jnp.where(lane < GDN_HEADS, beta, gc)
    gcol_ref[...] = out
    grow_ref[...] = out.T[: 2 * GDN_HEADS]


def _prenorm_gate(x, meta_block, gain, wba, alog, dtb):
    tm = META_BLOCK
    nx = x.shape[0] // tm
    rows = x.shape[0] + tm
    return pl.pallas_call(
        _prenorm_gate_kernel,
        grid=(nx + 1,),
        in_specs=[
            pl.BlockSpec((tm, D_MODEL), lambda i: (jnp.minimum(i, nx - 1), 0)),
            pl.BlockSpec((tm, D_MODEL), lambda i: (0, 0)),
            pl.BlockSpec((1, D_MODEL), lambda i: (0, 0)),
            pl.BlockSpec((GATE_LANES, D_MODEL), lambda i: (0, 0)),
            pl.BlockSpec((1, GATE_LANES), lambda i: (0, 0)),
            pl.BlockSpec((1, GATE_LANES), lambda i: (0, 0)),
        ],
        out_specs=[
            pl.BlockSpec((tm, D_MODEL), lambda i: (i, 0)),
            pl.BlockSpec((tm, GATE_LANES), lambda i: (i, 0)),
            pl.BlockSpec((2 * GDN_HEADS, tm), lambda i: (0, i)),
        ],
        out_shape=[
            jax.ShapeDtypeStruct((rows, D_MODEL), BF16),
            jax.ShapeDtypeStruct((rows, GATE_LANES), F32),
            jax.ShapeDtypeStruct((2 * GDN_HEADS, rows), F32),
        ],
        compiler_params=_params(("parallel",)),
        name="prenorm_gate",
    )(x, meta_block, gain, wba, alog, dtb)


IN_PROJ_TN = 1024
GDN_COL_TILES = 4 * GDN_WIDTH // IN_PROJ_TN


def _causal_conv_silu(x, prev, w):
    r8 = lax.broadcasted_iota(jnp.int32, prev.shape, 0)
    y = x * w[CONV_WIDTH - 1:CONV_WIDTH]
    for d in range(1, CONV_WIDTH):
        shifted = pltpu.roll(x, d, axis=0)
        top = jnp.where(r8 < d, pltpu.roll(prev, d, axis=0), shifted[:8])
        shifted = jnp.concatenate([top, shifted[8:]], axis=0)
        y = y + shifted * w[CONV_WIDTH - 1 - d:CONV_WIDTH - d]
    return _silu(y)


def _in_proj_kernel(a_ref, wg_ref, wd_ref, cw_ref, o_ref, tail_ref, raw_ref):
    j = pl.program_id(0)
    i = pl.program_id(1)
    tm = a_ref.shape[0]

    @pl.when(i == 0)
    def _():
        tail_ref[...] = jnp.zeros_like(tail_ref)

    def gdn_qkv(l2_scale):
        a = a_ref[...]
        for pair in range(GDN_HEADS // 2):
            cs = slice(pair * 2 * GDN_DIM, (pair + 1) * 2 * GDN_DIM)
            raw_ref[pair] = _dot_nt(a, wg_ref[cs, :].astype(BF16))
        for pair in range(GDN_HEADS // 2):
            cs = slice(pair * 2 * GDN_DIM, (pair + 1) * 2 * GDN_DIM)
            raw = raw_ref[pair]
            y = _causal_conv_silu(raw, tail_ref[:, cs], cw_ref[:, cs])
            tail_ref[:, cs] = raw[tm - 8:]
            for half in range(2):
                ls = slice(half * GDN_DIM, (half + 1) * GDN_DIM)
                yh = y[:, ls]
                if l2_scale is not None:
                    yh = yh * (lax.rsqrt(jnp.sum(yh * yh, axis=-1, keepdims=True) + NORM_EPS) * l2_scale)
                o_ref[:, pair * 2 * GDN_DIM + half * GDN_DIM:pair * 2 * GDN_DIM + (half + 1) * GDN_DIM] = yh

    pl.when(j == 0)(functools.partial(gdn_qkv, GDN_DIM ** -0.5))
    pl.when(j == 1)(functools.partial(gdn_qkv, 1.0))
    pl.when(j == 2)(functools.partial(gdn_qkv, None))

    @pl.when(j == GDN_COL_TILES - 1)
    def _():
        o_ref[...] = _dot_nt(a_ref[...], wg_ref[...].astype(BF16))

    @pl.when(j >= GDN_COL_TILES)
    def _():
        o_ref[...] = _dot_nt(a_ref[...], wd_ref[...].astype(BF16))


def _in_proj(n1, wt_all, wt_diff, conv_w):
    m = n1.shape[0]
    tm, tn = META_BLOCK, IN_PROJ_TN
    nm = m // tm
    n = GDN_COL_TILES * tn + wt_diff.shape[0]
    seq_order = lambda i: (i + nm - 1) % nm
    return pl.pallas_call(
        _in_proj_kernel,
        grid=(n // tn, nm),
        in_specs=[
            pl.BlockSpec((tm, D_MODEL), lambda j, i: (seq_order(i), 0)),
            pl.BlockSpec((tn, D_MODEL), lambda j, i: (jnp.minimum(j, GDN_COL_TILES - 1), 0)),
            pl.BlockSpec((tn, D_MODEL), lambda j, i: (jnp.maximum(j - GDN_COL_TILES, 0), 0)),
            pl.BlockSpec((CONV_WIDTH, tn), lambda j, i: (0, jnp.minimum(j, 2))),
        ],
        out_specs=pl.BlockSpec((tm, tn), lambda j, i: (seq_order(i), j)),
        out_shape=jax.ShapeDtypeStruct((m, n), F32),
        scratch_shapes=[pltpu.VMEM((8, tn), F32), pltpu.VMEM((GDN_HEADS // 2, tm, 2 * GDN_DIM), F32)],
        compiler_params=_params(("parallel", "arbitrary")),
        name="in_proj",
    )(n1, wt_all, wt_diff, conv_w)


GDN_LOCAL_CHUNKS = 4


def _gdn_local_kernel(q_ref, k_ref, v_ref, gcol_ref, grow_ref,
                      w_ref, u_ref, qg_ref, kd_ref, attn_ref, egl_ref):
    q_all = q_ref[...]
    k_all = k_ref[...]
    v_all = v_ref[...]
    gcol = gcol_ref[...]
    grow = grow_ref[...]
    ii = lax.broadcasted_iota(jnp.int32, (CHUNK, CHUNK), 0)
    jj = lax.broadcasted_iota(jnp.int32, (CHUNK, CHUNK), 1)
    units = [(c, h) for c in range(GDN_LOCAL_CHUNKS) for h in range(GDN_HEADS)]
    rows_of = lambda c: slice(c * CHUNK, (c + 1) * CHUNK)
    lanes_of = lambda h: slice(h * GDN_DIM, (h + 1) * GDN_DIM)
    qs, ks, kbs, vbs, egcs, gcs, kqs = [], [], [], [], [], [], []
    for c, h in units:
        rs, ls = rows_of(c), lanes_of(h)
        q = q_all[rs, ls]
        k = k_all[rs, ls]
        beta_c = gcol[rs, h:h + 1]
        gc_c = gcol[rs, GDN_HEADS + h:GDN_HEADS + h + 1]
        kb = k * beta_c
        qs.append(q)
        ks.append(k)
        kbs.append(kb)
        vbs.append(v_all[rs, ls] * beta_c)
        gcs.append(gc_c)
        egcs.append(jnp.exp(gc_c))
        kqs.append(_dot_nt(jnp.concatenate([kb.astype(BF16), q.astype(BF16)], axis=0), k.astype(BF16)))
    lms, attns = [], []
    for (c, h), kq, gc_c in zip(units, kqs, gcs):
        gc_r = grow[GDN_HEADS + h:GDN_HEADS + h + 1, rows_of(c)]
        decay = jnp.exp(jnp.where(ii >= jj, gc_c - gc_r, MASK_VALUE))
        lms.append(jnp.where(ii > jj, kq[:CHUNK] * decay, 0.0))
        attns.append(kq[CHUNK:] * decay)
    xor = ii ^ jj
    eye = jnp.where(ii == jj, 1.0, 0.0)
    xs = [eye - jnp.where(xor == 1, lm, 0.0) for lm in lms]
    level = 1
    while (2 << level) <= CHUNK:
        sel = (xor >> level) == 1
        ys = [_dot(jnp.where(sel, lm, 0.0).astype(BF16), x.astype(BF16)) for lm, x in zip(lms, xs)]
        xs = [x - _dot(x.astype(BF16), y.astype(BF16)) for x, y in zip(xs, ys)]
        level += 1
    uws = [_dot(x.astype(BF16), jnp.concatenate([vb.astype(BF16), (kb * egc).astype(BF16)], axis=1))
           for x, vb, kb, egc in zip(xs, vbs, kbs, egcs)]
    for (c, h), uw, q, k, egc, gc_c, attn in zip(units, uws, qs, ks, egcs, gcs, attns):
        rs, ls = rows_of(c), lanes_of(h)
        u_ref[rs, ls] = uw[:, :GDN_DIM]
        w_ref[rs, ls] = uw[:, GDN_DIM:].astype(BF16)
        qg_ref[rs, ls] = (q * egc).astype(BF16)
        gc_last = gc_c[CHUNK - 1:CHUNK]
        kd_ref[rs, ls] = (k * jnp.exp(gc_last - gc_c)).astype(BF16)
        attn_ref[h, rs, :] = attn.astype(BF16)
        egl_ref[c, h:h + 1, :] = jnp.broadcast_to(jnp.exp(gc_last), (1, GDN_DIM))


def _gdn_local(proj, gcol, grow):
    rows = proj.shape[0]
    rb = GDN_LOCAL_CHUNKS * CHUNK
    bidx = lambda i: i
    blk = lambda col: pl.BlockSpec((rb, GDN_WIDTH), lambda i, col=col: (bidx(i), col))
    row_out = lambda dt: jax.ShapeDtypeStruct((rows, GDN_WIDTH), dt)
    return pl.pallas_call(
        _gdn_local_kernel,
        grid=(rows // rb,),
        in_specs=[
            blk(0), blk(1), blk(2),
            pl.BlockSpec((rb, GATE_LANES), lambda i: (bidx(i), 0)),
            pl.BlockSpec((2 * GDN_HEADS, rb), lambda i: (0, bidx(i))),
        ],
        out_specs=[
            pl.BlockSpec((rb, GDN_WIDTH), lambda i: (bidx(i), 0)),
            pl.BlockSpec((rb, GDN_WIDTH), lambda i: (bidx(i), 0)),
            pl.BlockSpec((rb, GDN_WIDTH), lambda i: (bidx(i), 0)),
            pl.BlockSpec((rb, GDN_WIDTH), lambda i: (bidx(i), 0)),
            pl.BlockSpec((GDN_HEADS, rb, CHUNK), lambda i: (0, bidx(i), 0)),
            pl.BlockSpec((GDN_LOCAL_CHUNKS, GDN_HEADS, GDN_DIM), lambda i: (bidx(i), 0, 0)),
        ],
        out_shape=[
            row_out(BF16),
            row_out(F32),
            row_out(BF16),
            row_out(BF16),
            jax.ShapeDtypeStruct((GDN_HEADS, rows, CHUNK), BF16),
            jax.ShapeDtypeStruct((rows // CHUNK, GDN_HEADS, GDN_DIM), F32),
        ],
        compiler_params=_params(("parallel",)),
        name="gdn_local",
    )(proj, proj, proj, gcol, grow)


GDN_STATE_CHUNKS = 4


def _gdn_state_kernel(w_ref, u_ref, qg_ref, kd_ref, attn_ref, egl_ref, z_ref, gain_ref, o_ref, s_ref):
    @pl.when(pl.program_id(0) == 0)
    def _():
        s_ref[...] = jnp.zeros_like(s_ref)

    gain = gain_ref[...]
    heads = range(GDN_HEADS)
    lanes = [slice(h * GDN_DIM, (h + 1) * GDN_DIM) for h in heads]
    ss = [s_ref[h] for h in heads]
    for ck in range(GDN_STATE_CHUNKS):
        rw = slice(ck * CHUNK, (ck + 1) * CHUNK)
        rs = [_dot(jnp.concatenate([w_ref[rw, ls], qg_ref[rw, ls]], axis=0), s.astype(BF16))
              for ls, s in zip(lanes, ss)]
        vns = [(u_ref[rw, ls] - r[:CHUNK]).astype(BF16) for ls, r in zip(lanes, rs)]
        os_ = [r[CHUNK:] + _dot(attn_ref[h, rw, :], vn) for h, r, vn in zip(heads, rs, vns)]
        ss = [s * egl_ref[ck, h:h + 1, :] + _dot_tn(kd_ref[rw, ls], vn)
              for h, ls, s, vn in zip(heads, lanes, ss, vns)]
        for ls, o in zip(lanes, os_):
            on = o * lax.rsqrt(jnp.mean(o * o, axis=-1, keepdims=True) + NORM_EPS) * gain
            o_ref[rw, ls] = (on * _silu(z_ref[rw, ls])).astype(BF16)
    for h, s in zip(heads, ss):
        s_ref[h] = s


def _gdn_state(w, u, qg, kd, attn, egl, proj, gain, seq):
    rows = w.shape[0]
    rb = GDN_STATE_CHUNKS * CHUNK
    nblocks = rows // rb
    phys = lambda c: (c + nblocks - 1) % nblocks
    rowblk = lambda col: pl.BlockSpec((rb, GDN_WIDTH), lambda c, col=col: (phys(c), col))
    return pl.pallas_call(
        _gdn_state_kernel,
        grid=(seq // rb + 1,),
        in_specs=[
            rowblk(0), rowblk(0), rowblk(0), rowblk(0),
            pl.BlockSpec((GDN_HEADS, rb, CHUNK), lambda c: (0, phys(c), 0)),
            pl.BlockSpec((GDN_STATE_CHUNKS, GDN_HEADS, GDN_DIM), lambda c: (phys(c), 0, 0)),
            rowblk(3),
            pl.BlockSpec((1, GDN_DIM), lambda c: (0, 0)),
        ],
        out_specs=pl.BlockSpec((rb, GDN_WIDTH), lambda c: (jnp.maximum(c - 1, 0), 0)),
        out_shape=jax.ShapeDtypeStruct((seq, GDN_WIDTH), BF16),
        scratch_shapes=[pltpu.VMEM((GDN_HEADS, GDN_DIM, GDN_DIM), F32)],
        compiler_params=_params(("arbitrary",)),
        name="gdn_state",
    )(w, u, qg, kd, attn, egl, proj, gain)


KV_TILE = 128
Q_SCALE = DIFF_DIM ** -0.5 * math.log2(math.e)


def _attn_prep_kernel(q_ref, k_ref, v_ref, cos_ref, sin_ref, qg_ref, kg_ref, gsum_ref,
                      q2_ref, kr_ref, vt_ref):
    cos = cos_ref[...]
    sin = sin_ref[...]
    gsum = gsum_ref[...]
    lane = lax.broadcasted_iota(jnp.int32, cos.shape, 1)
    first_half = (lane % DIFF_DIM) < (DIFF_DIM // 2)
    low_map = lane < DIFF_DIM

    def norm_rope(x, gain):
        ms = _dot((x * x).astype(BF16), gsum) * (1.0 / DIFF_DIM)
        xn = x * lax.rsqrt(ms + NORM_EPS) * gain
        rot = jnp.where(first_half, pltpu.roll(xn, LANES - DIFF_DIM // 2, axis=1),
                        pltpu.roll(xn, DIFF_DIM // 2, axis=1))
        return xn * cos + rot * sin

    for h in range(DIFF_HEADS):
        ls = slice(h * DIFF_VDIM, (h + 1) * DIFF_VDIM)
        q = norm_rope(q_ref[:, ls], qg_ref[...]) * Q_SCALE
        q2_ref[0, :, ls] = jnp.where(low_map, q, 0.0).astype(BF16)
        q2_ref[1, :, ls] = jnp.where(low_map, 0.0, q).astype(BF16)
        kr_ref[:, ls] = norm_rope(k_ref[:, ls], kg_ref[...]).astype(BF16)
        vt_ref[0, ls, :] = v_ref[:, ls].T.astype(BF16)


def _attn_prep(proj, cos, sin, qgain, kgain, gsum):
    rows = proj.shape[0]
    tm = KV_TILE
    bidx = lambda i: i
    col = lambda c: pl.BlockSpec((tm, DIFF_WIDTH), lambda i, c=c: (bidx(i), c))
    small = lambda shape: pl.BlockSpec(shape, lambda i: (0,) * len(shape))
    return pl.pallas_call(
        _attn_prep_kernel,
        grid=(rows // tm,),
        in_specs=[
            col(4), col(5), col(6),
            pl.BlockSpec((tm, LANES), lambda i: (bidx(i), 0)),
            pl.BlockSpec((tm, LANES), lambda i: (bidx(i), 0)),
            small((1, LANES)), small((1, LANES)), small((LANES, LANES)),
        ],
        out_specs=[
            pl.BlockSpec((2, tm, DIFF_WIDTH), lambda i: (0, bidx(i), 0)),
            pl.BlockSpec((tm, DIFF_WIDTH), lambda i: (bidx(i), 0)),
            pl.BlockSpec((1, DIFF_WIDTH, tm), lambda i: (bidx(i), 0, 0)),
        ],
        out_shape=[
            jax.ShapeDtypeStruct((2, rows, DIFF_WIDTH), BF16),
            jax.ShapeDtypeStruct((rows, DIFF_WIDTH), BF16),
            jax.ShapeDtypeStruct((rows // tm, DIFF_WIDTH, tm), BF16),
        ],
        compiler_params=_params(("parallel",)),
        name="attn_prep",
    )(proj, proj, proj, cos, sin, qgain, kgain, gsum)


ATTN_BLOCK = 1024
ATTN_QSUB = 256

def _diff_attn_kernel(q_ref, k_ref, vt_ref, lam_ref, gain_ref, o_ref, acc_ref, st_ref):
    i = pl.program_id(1)
    rows = k_ref.shape[0]
    bk = ATTN_QSUB
    nsub = q_ref.shape[1] // ATTN_QSUB
    chains = [(mp, sb) for mp in range(2) for sb in range(nsub)]
    every = list(range(len(chains)))
    qs = [q_ref[mp, sb * ATTN_QSUB:(sb + 1) * ATTN_QSUB, :] for mp, sb in chains]
    kv_tiles = bk // KV_TILE

    def update(sts, vt, ms, ls, which):
        ms, ls = list(ms), list(ls)
        first = ms[which[0]] is None
        cms = [jnp.max(st, axis=0, keepdims=True) for st in sts]
        m_new = cms if first else [jnp.maximum(ms[c], cm) for c, cm in zip(which, cms)]
        ps = [jnp.exp2(st - mn) for st, mn in zip(sts, m_new)]
        pvs = [_dot(vt, p.astype(BF16)) for p in ps]
        psums = [jnp.sum(p, axis=0, keepdims=True) for p in ps]
        for n, c in enumerate(which):
            if first:
                ls[c] = psums[n]
                acc_ref[c] = pvs[n]
            else:
                alpha = jnp.exp2(ms[c] - m_new[n])
                ls[c] = alpha * ls[c] + psums[n]
                acc_ref[c] = alpha * acc_ref[c] + pvs[n]
            ms[c] = m_new[n]
        return ms, ls

    def store_scores(j, slot, which):
        start = j * bk if isinstance(j, int) else pl.multiple_of(j * bk, bk)
        k_c = k_ref[pl.ds(start, bk), :]
        for c in which:
            st_ref[slot, c] = _dot_nt(k_c, qs[c])

    def values_t(j):
        return jnp.concatenate([vt_ref[j * kv_tiles + t] for t in range(kv_tiles)], axis=1)

    k_meta = k_ref[rows - KV_TILE:rows, :]
    key = lax.broadcasted_iota(jnp.int32, (KV_TILE, ATTN_QSUB), 0)
    sts = [jnp.where(key >= KV_TILE - N_META, _dot_nt(k_meta, q), MASK_VALUE) for q in qs]
    store_scores(0, 0, every)
    none = [None] * len(chains)
    ms, ls = update(sts, vt_ref[rows // KV_TILE - 1], none, none, every)

    def full_blocks(t, carry):
        ms, ls = carry
        for n in range(nsub):
            j = nsub * t + n
            store_scores(j + 1, (n + 1) % 2, every)
            ms, ls = update([st_ref[n % 2, c] for c in every], values_t(j), ms, ls, every)
        return tuple(ms), tuple(ls)

    ms, ls = lax.fori_loop(0, i, full_blocks, (tuple(ms), tuple(ls)))
    tri = (lax.broadcasted_iota(jnp.int32, (bk, ATTN_QSUB), 0)
           <= lax.broadcasted_iota(jnp.int32, (bk, ATTN_QSUB), 1))
    for d in range(nsub):
        if d + 1 < nsub:
            store_scores(i * nsub + d + 1, (d + 1) % 2, [c for c in every if chains[c][1] > d])
        which = [c for c in every if chains[c][1] >= d]
        sts = [jnp.where(tri, st_ref[d % 2, c], MASK_VALUE) if chains[c][1] == d else st_ref[d % 2, c]
               for c in which]
        ms, ls = update(sts, values_t(i * nsub + d), ms, ls, which)

    lp = lam_ref[...]
    lam = (jnp.exp(jnp.sum(lp[0:1] * lp[1:2], axis=-1, keepdims=True))
           - jnp.exp(jnp.sum(lp[2:3] * lp[3:4], axis=-1, keepdims=True)) + LAMBDA_INIT)
    gain = gain_ref[...]
    for sb in range(nsub):
        ot = acc_ref[sb] * (1.0 / ls[sb]) - acc_ref[nsub + sb] * (lam / ls[nsub + sb])
        ot = ot * lax.rsqrt(jnp.mean(ot * ot, axis=0, keepdims=True) + NORM_EPS) * gain
        o_ref[sb * ATTN_QSUB:(sb + 1) * ATTN_QSUB, :] = (ot * (1.0 - LAMBDA_INIT)).T.astype(BF16)


def _diff_attn(q2, kr, vt, lam_params, gain_col, seq):
    rows = kr.shape[0]
    bq = ATTN_BLOCK
    nchains = 2 * (bq // ATTN_QSUB)
    assert seq % bq == 0 and bq % (2 * ATTN_QSUB) == 0
    return pl.pallas_call(
        _diff_attn_kernel,
        grid=(DIFF_HEADS, seq // bq),
        in_specs=[
            pl.BlockSpec((2, bq, DIFF_VDIM), lambda h, i: (0, i, h)),
            pl.BlockSpec((rows, DIFF_VDIM), lambda h, i: (0, h)),
            pl.BlockSpec((rows // KV_TILE, DIFF_VDIM, KV_TILE), lambda h, i: (0, h, 0)),
            pl.BlockSpec((4, DIFF_DIM), lambda h, i: (0, 0)),
            pl.BlockSpec((DIFF_VDIM, 1), lambda h, i: (0, 0)),
        ],
        out_specs=pl.BlockSpec((bq, DIFF_VDIM), lambda h, i: (i, h)),
        out_shape=jax.ShapeDtypeStruct((seq, DIFF_WIDTH), BF16),
        scratch_shapes=[pltpu.VMEM((nchains, DIFF_VDIM, ATTN_QSUB), F32),
                        pltpu.VMEM((2, nchains, ATTN_QSUB, ATTN_QSUB), F32)],
        compiler_params=_params(("parallel", "arbitrary")),
        name="diff_attn",
    )(q2, kr, vt, lam_params, gain_col)


def _out_proj_kernel(mg_ref, md_ref, wg_ref, wd_ref, h_ref, gain_ref, h2_ref, n2_ref):
    h2 = (h_ref[...] + _dot(mg_ref[...], wg_ref[...].astype(BF16))
          + _dot(md_ref[...], wd_ref[...].astype(BF16)))
    h2_ref[...] = h2
    ms = jnp.mean(h2 * h2, axis=-1, keepdims=True)
    n2_ref[...] = (h2 * lax.rsqrt(ms + NORM_EPS) * gain_ref[...]).astype(BF16)


def _out_proj(mix_g, mix_d, w_out, h, gain, seq):
    tm = _pick(seq, (512, 128))
    return pl.pallas_call(
        _out_proj_kernel,
        grid=(seq // tm,),
        in_specs=[
            pl.BlockSpec((tm, GDN_WIDTH), lambda i: (i, 0)),
            pl.BlockSpec((tm, DIFF_WIDTH), lambda i: (i, 0)),
            pl.BlockSpec((GDN_WIDTH, D_MODEL), lambda i: (0, 0), pipeline_mode=pl.Buffered(1)),
            pl.BlockSpec((DIFF_WIDTH, D_MODEL), lambda i: (1, 0), pipeline_mode=pl.Buffered(1)),
            pl.BlockSpec((tm, D_MODEL), lambda i: (i, 0)),
            pl.BlockSpec((1, D_MODEL), lambda i: (0, 0)),
        ],
        out_specs=[
            pl.BlockSpec((tm, D_MODEL), lambda i: (i, 0)),
            pl.BlockSpec((tm, D_MODEL), lambda i: (i, 0)),
        ],
        out_shape=[
            jax.ShapeDtypeStruct((seq, D_MODEL), F32),
            jax.ShapeDtypeStruct((seq, D_MODEL), BF16),
        ],
        compiler_params=_params(("parallel",)),
        name="out_proj",
    )(mix_g, mix_d, w_out, w_out, h, gain)


def _gate_up_kernel(n_ref, wg_ref, wu_ref, a_ref):
    n = n_ref[...]
    g = _dot(n, wg_ref[...].astype(BF16))
    u = _dot(n, wu_ref[...].astype(BF16))
    a_ref[...] = (_silu(g) * u).astype(BF16)


def _gate_up(n2, w_gu):
    seq = n2.shape[0]
    tm = _pick(seq, (2048, 1024, 128))
    tn = 512
    nt = D_FF // tn
    return pl.pallas_call(
        _gate_up_kernel,
        grid=(nt, seq // tm),
        in_specs=[
            pl.BlockSpec((tm, D_MODEL), lambda j, i: (i, 0)),
            pl.BlockSpec((D_MODEL, tn), lambda j, i: (0, j)),
            pl.BlockSpec((D_MODEL, tn), lambda j, i: (0, j + nt)),
        ],
        out_specs=pl.BlockSpec((tm, tn), lambda j, i: (i, j)),
        out_shape=jax.ShapeDtypeStruct((seq, D_FF), BF16),
        compiler_params=_params(("parallel", "parallel")),
        name="ffn_gate_up",
    )(n2, w_gu, w_gu)


def _down_kernel(a_ref, w_ref, h_ref, o_ref):
    o_ref[...] = h_ref[...] + _dot(a_ref[...], w_ref[...].astype(BF16))


def _down(act, w_down, h2):
    seq = act.shape[0]
    tm = _pick(seq, (512, 128))
    tn = 512
    return pl.pallas_call(
        _down_kernel,
        grid=(D_MODEL // tn, seq // tm),
        in_specs=[
            pl.BlockSpec((tm, D_FF), lambda j, i: (i, 0)),
            pl.BlockSpec((D_FF, tn), lambda j, i: (0, j)),
            pl.BlockSpec((tm, tn), lambda j, i: (i, j)),
        ],
        out_specs=pl.BlockSpec((tm, tn), lambda j, i: (i, j)),
        out_shape=jax.ShapeDtypeStruct((seq, D_MODEL), F32),
        compiler_params=_params(("parallel", "parallel")),
        name="ffn_down",
    )(act, w_down, h2)


def _rope_tables(seq):
    half = DIFF_DIM // 2
    pos = jnp.concatenate([jnp.arange(seq) + N_META, jnp.zeros((META_BLOCK - N_META,), jnp.int32),
                           jnp.arange(N_META)]).astype(F32)
    inv_freq = ROPE_THETA ** (-jnp.arange(half, dtype=F32) / half)
    ang = pos[:, None] * inv_freq[None, :]
    cos = jnp.tile(jnp.cos(ang), (1, LANES // half))
    sin = jnp.sin(ang)
    sin = jnp.tile(jnp.concatenate([-sin, sin], axis=1), (1, LANES // DIFF_DIM))
    return cos, sin


def _lane_pad(v, offset):
    return jnp.zeros((1, GATE_LANES), F32).at[0, offset:offset + v.shape[0]].set(v.astype(F32))


def kernel(x, meta_tokens, attn_norm, w_in, conv_w, a_log, dt_bias, gdn_norm, q_norm, k_norm,
           lambda_q1, lambda_k1, lambda_q2, lambda_k2, diff_norm, w_out, ffn_norm, w_gate_up, w_down):
    assert x.shape[0] == 1 and x.shape[2] == D_MODEL
    seq = x.shape[1]
    assert seq % META_BLOCK == 0
    xs = x[0]
    meta_block = jnp.concatenate([jnp.zeros((META_BLOCK - N_META, D_MODEL), xs.dtype),
                                  meta_tokens.astype(xs.dtype)], axis=0)

    gdn_cols = 4 * GDN_WIDTH
    wt_in = w_in[0].T
    wt_diff = wt_in[gdn_cols + 2 * GDN_HEADS:]
    wt_ba = jnp.pad(wt_in[gdn_cols:gdn_cols + 2 * GDN_HEADS],
                    ((0, GATE_LANES - 2 * GDN_HEADS), (0, 0))).astype(BF16)

    n1, gcol, grow = _prenorm_gate(xs, meta_block, attn_norm, wt_ba, _lane_pad(a_log[0], GDN_HEADS),
                                   _lane_pad(dt_bias[0], GDN_HEADS))
    proj = _in_proj(n1, wt_in, wt_diff, conv_w[0])

    w, u, qg, kd, attn, egl = _gdn_local(proj, gcol, grow)
    mix_g = _gdn_state(w, u, qg, kd, attn, egl, proj, gdn_norm, seq)

    cos, sin = _rope_tables(seq)
    tile2 = lambda g: jnp.tile(g.astype(F32), (1, LANES // DIFF_DIM))
    lane = np.arange(LANES)
    gsum = jnp.asarray((lane[:, None] // DIFF_DIM) == (lane[None, :] // DIFF_DIM), BF16)
    q2, kr, vt = _attn_prep(proj, cos, sin, tile2(q_norm), tile2(k_norm), gsum)
    lam_params = jnp.concatenate([lambda_q1, lambda_k1, lambda_q2, lambda_k2], axis=0).astype(F32)
    mix_d = _diff_attn(q2, kr, vt, lam_params, diff_norm.astype(F32).reshape(DIFF_VDIM, 1), seq)

    h2, n2 = _out_proj(mix_g, mix_d, w_out[0], xs, ffn_norm, seq)
    act = _gate_up(n2, w_gate_up[0])
    out = _down(act, w_down[0], h2)
    return out[None]
```

```python
import functools
import math

import jax
import jax.numpy as jnp
import numpy as np
from jax import lax
from jax.experimental import pallas as pl
from jax.experimental.pallas import tpu as pltpu

F32 = jnp.float32
BF16 = jnp.bfloat16

D_MODEL = 2048
N_META = 16
GDN_HEADS = 8
GDN_DIM = 128
GDN_WIDTH = GDN_HEADS * GDN_DIM
CONV_WIDTH = 4
CHUNK = 64
DIFF_HEADS = 8
DIFF_DIM = 64
DIFF_VDIM = 2 * DIFF_DIM
DIFF_WIDTH = DIFF_HEADS * DIFF_VDIM
ROPE_THETA = 10000.0
D_FF = 5632
NORM_EPS = 1e-6
MASK_VALUE = -1e30
LAMBDA_INIT = 0.8 - 0.6 * math.exp(-0.3 * 0)

LANES = 128
META_BLOCK = 512
GATE_LANES = 128
VMEM_LIMIT = 56 * 1024 * 1024


def _pick(n, candidates):
    for c in candidates:
        if n % c == 0:
            return c
    raise ValueError(f"no tile in {candidates} divides {n}")


def _params(sem, vmem=VMEM_LIMIT):
    return pltpu.CompilerParams(dimension_semantics=sem, vmem_limit_bytes=vmem)


def _dot(a, b):
    return jnp.dot(a, b, preferred_element_type=F32)


def _dot_nt(a, b):
    return lax.dot_general(a, b, (((1,), (1,)), ((), ())), preferred_element_type=F32)


def _dot_tn(a, b):
    return lax.dot_general(a, b, (((0,), (0,)), ((), ())), preferred_element_type=F32)


def _softplus(x):
    return jnp.maximum(x, 0.0) + jnp.log1p(jnp.exp(-jnp.abs(x)))


def _silu(x):
    return x * jax.nn.sigmoid(x)


def _prenorm_gate_kernel(x_ref, mb_ref, gain_ref, wba_ref, alog_ref, dtb_ref, n_ref, gcol_ref, grow_ref):
    h = jnp.where(pl.program_id(0) < pl.num_programs(0) - 1, x_ref[...], mb_ref[...])
    ms = jnp.mean(h * h, axis=-1, keepdims=True)
    n = (h * lax.rsqrt(ms + NORM_EPS) * gain_ref[...]).astype(BF16)
    n_ref[...] = n
    ba = _dot_nt(n, wba_ref[...])
    beta = jax.nn.sigmoid(ba)
    g = -jnp.exp(alog_ref[...]) * _softplus(ba + dtb_ref[...])
    row = lax.broadcasted_iota(jnp.int32, ba.shape, 0) % CHUNK
    gc = g
    for d in (1, 2, 4, 8, 16, 32):
        gc = gc + jnp.where(row >= d, pltpu.roll(gc, d, axis=0), 0.0)
    lane = lax.broadcasted_iota(jnp.int32, ba.shape, 1)
    out = jnp.where(lane < GDN_HEADS, beta, gc)
    gcol_ref[...] = out
    grow_ref[...] = out.T[: 2 * GDN_HEADS]


def _prenorm_gate(x, meta_block, gain, wba, alog, dtb):
    tm = META_BLOCK
    nx = x.shape[0] // tm
    rows = x.shape[0] + tm
    return pl.pallas_call(
        _prenorm_gate_kernel,
        grid=(nx + 1,),
        in_specs=[
            pl.BlockSpec((tm, D_MODEL), lambda i: (jnp.minimum(i, nx - 1), 0)),
            pl.BlockSpec((tm, D_MODEL), lambda i: (0, 0)),
            pl.BlockSpec((1, D_MODEL), lambda i: (0, 0)),
            pl.BlockSpec((GATE_LANES, D_MODEL), lambda i: (0, 0)),
            pl.BlockSpec((1, GATE_LANES), lambda i: (0, 0)),
            pl.BlockSpec((1, GATE_LANES), lambda i: (0, 0)),
        ],
        out_specs=[
            pl.BlockSpec((tm, D_MODEL), lambda i: (i, 0)),
            pl.BlockSpec((tm, GATE_LANES), lambda i: (i, 0)),
            pl.BlockSpec((2 * GDN_HEADS, tm), lambda i: (0, i)),
        ],
        out_shape=[
            jax.ShapeDtypeStruct((rows, D_MODEL), BF16),
            jax.ShapeDtypeStruct((rows, GATE_LANES), F32),
            jax.ShapeDtypeStruct((2 * GDN_HEADS, rows), F32),
        ],
        compiler_params=_params(("parallel",)),
        name="prenorm_gate",
    )(x, meta_block, gain, wba, alog, dtb)


IN_PROJ_TN = 1024
GDN_COL_TILES = 4 * GDN_WIDTH // IN_PROJ_TN


def _causal_conv_silu(x, prev, w):
    r8 = lax.broadcasted_iota(jnp.int32, prev.shape, 0)
    y = x * w[CONV_WIDTH - 1:CONV_WIDTH]
    for d in range(1, CONV_WIDTH):
        shifted = pltpu.roll(x, d, axis=0)
        top = jnp.where(r8 < d, pltpu.roll(prev, d, axis=0), shifted[:8])
        shifted = jnp.concatenate([top, shifted[8:]], axis=0)
        y = y + shifted * w[CONV_WIDTH - 1 - d:CONV_WIDTH - d]
    return _silu(y)


def _in_proj_kernel(a_ref, wg_ref, wd_ref, cw_ref, o_ref, tail_ref, raw_ref, w16_ref):
    j = pl.program_id(0)
    i = pl.program_id(1)
    tm = a_ref.shape[0]

    @pl.when(i == 0)
    def _():
        tail_ref[...] = jnp.zeros_like(tail_ref)

    @pl.when((i == 0) & (j < GDN_COL_TILES))
    def _():
        w16_ref[...] = wg_ref[...].T.astype(BF16)

    @pl.when((i == 0) & (j >= GDN_COL_TILES))
    def _():
        w16_ref[...] = wd_ref[...].T.astype(BF16)

    def gdn_qkv(l2_scale):
        a = a_ref[...]
        for pair in range(GDN_HEADS // 2):
            cs = slice(pair * 2 * GDN_DIM, (pair + 1) * 2 * GDN_DIM)
            raw_ref[pair] = _dot(a, w16_ref[:, cs])
        for pair in range(GDN_HEADS // 2):
            cs = slice(pair * 2 * GDN_DIM, (pair + 1) * 2 * GDN_DIM)
            raw = raw_ref[pair]
            y = _causal_conv_silu(raw, tail_ref[:, cs], cw_ref[:, cs])
            tail_ref[:, cs] = raw[tm - 8:]
            for half in range(2):
                ls = slice(half * GDN_DIM, (half + 1) * GDN_DIM)
                yh = y[:, ls]
                if l2_scale is not None:
                    yh = yh * (lax.rsqrt(jnp.sum(yh * yh, axis=-1, keepdims=True) + NORM_EPS) * l2_scale)
                o_ref[:, pair * 2 * GDN_DIM + half * GDN_DIM:pair * 2 * GDN_DIM + (half + 1) * GDN_DIM] = yh

    pl.when(j == 0)(functools.partial(gdn_qkv, GDN_DIM ** -0.5))
    pl.when(j == 1)(functools.partial(gdn_qkv, 1.0))
    pl.when(j == 2)(functools.partial(gdn_qkv, None))

    @pl.when(j >= GDN_COL_TILES - 1)
    def _():
        o_ref[...] = _dot(a_ref[...], w16_ref[...])


def _in_proj(n1, wt_all, wt_diff, conv_w):
    m = n1.shape[0]
    tm, tn = META_BLOCK, IN_PROJ_TN
    nm = m // tm
    n = GDN_COL_TILES * tn + wt_diff.shape[0]
    seq_order = lambda i: (i + nm - 1) % nm
    return pl.pallas_call(
        _in_proj_kernel,
        grid=(n // tn, nm),
        in_specs=[
            pl.BlockSpec((tm, D_MODEL), lambda j, i: (seq_order(i), 0)),
            pl.BlockSpec((tn, D_MODEL), lambda j, i: (jnp.minimum(j, GDN_COL_TILES - 1), 0)),
            pl.BlockSpec((tn, D_MODEL), lambda j, i: (jnp.maximum(j - GDN_COL_TILES, 0), 0)),
            pl.BlockSpec((CONV_WIDTH, tn), lambda j, i: (0, jnp.minimum(j, 2))),
        ],
        out_specs=pl.BlockSpec((tm, tn), lambda j, i: (seq_order(i), j)),
        out_shape=jax.ShapeDtypeStruct((m, n), F32),
        scratch_shapes=[pltpu.VMEM((8, tn), F32), pltpu.VMEM((GDN_HEADS // 2, tm, 2 * GDN_DIM), F32),
                        pltpu.VMEM((D_MODEL, tn), BF16)],
        compiler_params=_params(("parallel", "arbitrary")),
        name="in_proj",
    )(n1, wt_all, wt_diff, conv_w)


GDN_LOCAL_CHUNKS = 4


def _gdn_local_kernel(q_ref, k_ref, v_ref, gcol_ref, grow_ref,
                      w_ref, u_ref, qg_ref, kd_ref, attn_ref, egl_ref):
    q_all = q_ref[...]
    k_all = k_ref[...]
    v_all = v_ref[...]
    gcol = gcol_ref[...]
    grow = grow_ref[...]
    ii = lax.broadcasted_iota(jnp.int32, (CHUNK, CHUNK), 0)
    jj = lax.broadcasted_iota(jnp.int32, (CHUNK, CHUNK), 1)
    units = [(c, h) for c in range(GDN_LOCAL_CHUNKS) for h in range(GDN_HEADS)]
    rows_of = lambda c: slice(c * CHUNK, (c + 1) * CHUNK)
    lanes_of = lambda h: slice(h * GDN_DIM, (h + 1) * GDN_DIM)
    qs, ks, kbs, vbs, egcs, gcs, kqs = [], [], [], [], [], [], []
    for c, h in units:
        rs, ls = rows_of(c), lanes_of(h)
        q = q_all[rs, ls]
        k = k_all[rs, ls]
        beta_c = gcol[rs, h:h + 1]
        gc_c = gcol[rs, GDN_HEADS + h:GDN_HEADS + h + 1]
        kb = k * beta_c
        qs.append(q)
        ks.append(k)
        kbs.append(kb)
        vbs.append(v_all[rs, ls] * beta_c)
        gcs.append(gc_c)
        egcs.append(jnp.exp(gc_c))
        kqs.append(_dot_nt(jnp.concatenate([kb.astype(BF16), q.astype(BF16)], axis=0), k.astype(BF16)))
    lms, attns = [], []
    for (c, h), kq, gc_c in zip(units, kqs, gcs):
        gc_r = grow[GDN_HEADS + h:GDN_HEADS + h + 1, rows_of(c)]
        decay = jnp.exp(jnp.where(ii >= jj, gc_c - gc_r, MASK_VALUE))
        lms.append(jnp.where(ii > jj, kq[:CHUNK] * decay, 0.0))
        attns.append(kq[CHUNK:] * decay)
    xor = ii ^ jj
    eye = jnp.where(ii == jj, 1.0, 0.0)
    xs = [eye - jnp.where(xor == 1, lm, 0.0) for lm in lms]
    level = 1
    while (2 << level) <= CHUNK:
        sel = (xor >> level) == 1
        ys = [_dot(jnp.where(sel, lm, 0.0).astype(BF16), x.astype(BF16)) for lm, x in zip(lms, xs)]
        xs = [x - _dot(x.astype(BF16), y.astype(BF16)) for x, y in zip(xs, ys)]
        level += 1
    uws = [_dot(x.astype(BF16), jnp.concatenate([vb.astype(BF16), (kb * egc).astype(BF16)], axis=1))
           for x, vb, kb, egc in zip(xs, vbs, kbs, egcs)]
    for (c, h), uw, q, k, egc, gc_c, attn in zip(units, uws, qs, ks, egcs, gcs, attns):
        rs, ls = rows_of(c), lanes_of(h)
        u_ref[rs, ls] = uw[:, :GDN_DIM]
        w_ref[rs, ls] = uw[:, GDN_DIM:].astype(BF16)
        qg_ref[rs, ls] = (q * egc).astype(BF16)
        gc_last = gc_c[CHUNK - 1:CHUNK]
        kd_ref[rs, ls] = (k * jnp.exp(gc_last - gc_c)).astype(BF16)
        attn_ref[h, rs, :] = attn.astype(BF16)
        egl_ref[c, h:h + 1, :] = jnp.broadcast_to(jnp.exp(gc_last), (1, GDN_DIM))


def _gdn_local(proj, gcol, grow):
    rows = proj.shape[0]
    rb = GDN_LOCAL_CHUNKS * CHUNK
    bidx = lambda i: i
    blk = lambda col: pl.BlockSpec((rb, GDN_WIDTH), lambda i, col=col: (bidx(i), col))
    row_out = lambda dt: jax.ShapeDtypeStruct((rows, GDN_WIDTH), dt)
    return pl.pallas_call(
        _gdn_local_kernel,
        grid=(rows // rb,),
        in_specs=[
            blk(0), blk(1), blk(2),
            pl.BlockSpec((rb, GATE_LANES), lambda i: (bidx(i), 0)),
            pl.BlockSpec((2 * GDN_HEADS, rb), lambda i: (0, bidx(i))),
        ],
        out_specs=[
            pl.BlockSpec((rb, GDN_WIDTH), lambda i: (bidx(i), 0)),
            pl.BlockSpec((rb, GDN_WIDTH), lambda i: (bidx(i), 0)),
            pl.BlockSpec((rb, GDN_WIDTH), lambda i: (bidx(i), 0)),
            pl.BlockSpec((rb, GDN_WIDTH), lambda i: (bidx(i), 0)),
            pl.BlockSpec((GDN_HEADS, rb, CHUNK), lambda i: (0, bidx(i), 0)),
            pl.BlockSpec((GDN_LOCAL_CHUNKS, GDN_HEADS, GDN_DIM), lambda i: (bidx(i), 0, 0)),
        ],
        out_shape=[
            row_out(BF16),
            row_out(F32),
            row_out(BF16),
            row_out(BF16),
            jax.ShapeDtypeStruct((GDN_HEADS, rows, CHUNK), BF16),
            jax.ShapeDtypeStruct((rows // CHUNK, GDN_HEADS, GDN_DIM), F32),
        ],
        compiler_params=_params(("parallel",)),
        name="gdn_local",
    )(proj, proj, proj, gcol, grow)


GDN_STATE_CHUNKS = 4


def _gdn_state_kernel(w_ref, u_ref, qg_ref, kd_ref, attn_ref, egl_ref, z_ref, gain_ref, o_ref, s_ref):
    @pl.when(pl.program_id(0) == 0)
    def _():
        s_ref[...] = jnp.zeros_like(s_ref)

    gain = gain_ref[...]
    heads = range(GDN_HEADS)
    lanes = [slice(h * GDN_DIM, (h + 1) * GDN_DIM) for h in heads]
    ss = [s_ref[h] for h in heads]
    for ck in range(GDN_STATE_CHUNKS):
        rw = slice(ck * CHUNK, (ck + 1) * CHUNK)
        rs = [_dot(jnp.concatenate([w_ref[rw, ls], qg_ref[rw, ls]], axis=0), s.astype(BF16))
              for ls, s in zip(lanes, ss)]
        vns = [(u_ref[rw, ls] - r[:CHUNK]).astype(BF16) for ls, r in zip(lanes, rs)]
        os_ = [r[CHUNK:] + _dot(attn_ref[h, rw, :], vn) for h, r, vn in zip(heads, rs, vns)]
        ss = [s * egl_ref[ck, h:h + 1, :] + _dot_tn(kd_ref[rw, ls], vn)
              for h, ls, s, vn in zip(heads, lanes, ss, vns)]
        for ls, o in zip(lanes, os_):
            on = o * lax.rsqrt(jnp.mean(o * o, axis=-1, keepdims=True) + NORM_EPS) * gain
            o_ref[rw, ls] = (on * _silu(z_ref[rw, ls])).astype(BF16)
    for h, s in zip(heads, ss):
        s_ref[h] = s


def _gdn_state(w, u, qg, kd, attn, egl, proj, gain, seq):
    rows = w.shape[0]
    rb = GDN_STATE_CHUNKS * CHUNK
    nblocks = rows // rb
    phys = lambda c: (c + nblocks - 1) % nblocks
    rowblk = lambda col: pl.BlockSpec((rb, GDN_WIDTH), lambda c, col=col: (phys(c), col))
    return pl.pallas_call(
        _gdn_state_kernel,
        grid=(seq // rb + 1,),
        in_specs=[
            rowblk(0), rowblk(0), rowblk(0), rowblk(0),
            pl.BlockSpec((GDN_HEADS, rb, CHUNK), lambda c: (0, phys(c), 0)),
            pl.BlockSpec((GDN_STATE_CHUNKS, GDN_HEADS, GDN_DIM), lambda c: (phys(c), 0, 0)),
            rowblk(3),
            pl.BlockSpec((1, GDN_DIM), lambda c: (0, 0)),
        ],
        out_specs=pl.BlockSpec((rb, GDN_WIDTH), lambda c: (jnp.maximum(c - 1, 0), 0)),
        out_shape=jax.ShapeDtypeStruct((seq, GDN_WIDTH), BF16),
        scratch_shapes=[pltpu.VMEM((GDN_HEADS, GDN_DIM, GDN_DIM), F32)],
        compiler_params=_params(("arbitrary",)),
        name="gdn_state",
    )(w, u, qg, kd, attn, egl, proj, gain)


KV_TILE = 128
Q_SCALE = DIFF_DIM ** -0.5 * math.log2(math.e)


def _attn_prep_kernel(q_ref, k_ref, v_ref, cos_ref, sin_ref, qg_ref, kg_ref, gsum_ref,
                      q2_ref, kr_ref, vt_ref):
    cos = cos_ref[...]
    sin = sin_ref[...]
    gsum = gsum_ref[...]
    lane = lax.broadcasted_iota(jnp.int32, cos.shape, 1)
    first_half = (lane % DIFF_DIM) < (DIFF_DIM // 2)
    low_map = lane < DIFF_DIM

    def norm_rope(x, gain):
        ms = _dot((x * x).astype(BF16), gsum) * (1.0 / DIFF_DIM)
        xn = x * lax.rsqrt(ms + NORM_EPS) * gain
        rot = jnp.where(first_half, pltpu.roll(xn, LANES - DIFF_DIM // 2, axis=1),
                        pltpu.roll(xn, DIFF_DIM // 2, axis=1))
        return xn * cos + rot * sin

    for h in range(DIFF_HEADS):
        ls = slice(h * DIFF_VDIM, (h + 1) * DIFF_VDIM)
        q = norm_rope(q_ref[:, ls], qg_ref[...]) * Q_SCALE
        q2_ref[0, :, ls] = jnp.where(low_map, q, 0.0).astype(BF16)
        q2_ref[1, :, ls] = jnp.where(low_map, 0.0, q).astype(BF16)
        kr_ref[:, ls] = norm_rope(k_ref[:, ls], kg_ref[...]).astype(BF16)
        vt_ref[0, ls, :] = v_ref[:, ls].T.astype(BF16)


def _attn_prep(proj, cos, sin, qgain, kgain, gsum):
    rows = proj.shape[0]
    tm = KV_TILE
    bidx = lambda i: i
    col = lambda c: pl.BlockSpec((tm, DIFF_WIDTH), lambda i, c=c: (bidx(i), c))
    small = lambda shape: pl.BlockSpec(shape, lambda i: (0,) * len(shape))
    return pl.pallas_call(
        _attn_prep_kernel,
        grid=(rows // tm,),
        in_specs=[
            col(4), col(5), col(6),
            pl.BlockSpec((tm, LANES), lambda i: (bidx(i), 0)),
            pl.BlockSpec((tm, LANES), lambda i: (bidx(i), 0)),
            small((1, LANES)), small((1, LANES)), small((LANES, LANES)),
        ],
        out_specs=[
            pl.BlockSpec((2, tm, DIFF_WIDTH), lambda i: (0, bidx(i), 0)),
            pl.BlockSpec((tm, DIFF_WIDTH), lambda i: (bidx(i), 0)),
            pl.BlockSpec((1, DIFF_WIDTH, tm), lambda i: (bidx(i), 0, 0)),
        ],
        out_shape=[
            jax.ShapeDtypeStruct((2, rows, DIFF_WIDTH), BF16),
            jax.ShapeDtypeStruct((rows, DIFF_WIDTH), BF16),
            jax.ShapeDtypeStruct((rows // tm, DIFF_WIDTH, tm), BF16),
        ],
        compiler_params=_params(("parallel",)),
        name="attn_prep",
    )(proj, proj, proj, cos, sin, qgain, kgain, gsum)


ATTN_BLOCK = 1024
ATTN_QSUB = 256

def _diff_attn_kernel(q_ref, k_ref, vt_ref, lam_ref, gain_ref, o_ref, acc_ref, st_ref, qt_ref):
    i = pl.program_id(1)
    rows = k_ref.shape[0]
    bk = ATTN_QSUB
    nsub = q_ref.shape[1] // ATTN_QSUB
    chains = [(mp, sb) for mp in range(2) for sb in range(nsub)]
    every = list(range(len(chains)))
    qs = [q_ref[mp, sb * ATTN_QSUB:(sb + 1) * ATTN_QSUB, :] for mp, sb in chains]
    eye = jnp.where(lax.broadcasted_iota(jnp.int32, (DIFF_VDIM, DIFF_VDIM), 0)
                    == lax.broadcasted_iota(jnp.int32, (DIFF_VDIM, DIFF_VDIM), 1), 1.0, 0.0).astype(BF16)
    for c, q in enumerate(qs):
        qt_ref[c] = _dot_nt(eye, q).astype(BF16)
    kv_tiles = bk // KV_TILE

    def update(sts, vt, ms, ls, which):
        ms, ls = list(ms), list(ls)
        first = ms[which[0]] is None
        cms = [jnp.max(st, axis=0, keepdims=True) for st in sts]
        m_new = cms if first else [jnp.maximum(ms[c], cm) for c, cm in zip(which, cms)]
        ps = [jnp.exp2(st - mn) for st, mn in zip(sts, m_new)]
        pvs = [_dot(vt, p.astype(BF16)) for p in ps]
        psums = [jnp.sum(p, axis=0, keepdims=True) for p in ps]
        for n, c in enumerate(which):
            if first:
                ls[c] = psums[n]
                acc_ref[c] = pvs[n]
            else:
                alpha = jnp.exp2(ms[c] - m_new[n])
                ls[c] = alpha * ls[c] + psums[n]
                acc_ref[c] = alpha * acc_ref[c] + pvs[n]
            ms[c] = m_new[n]
        return ms, ls

    def store_scores(j, slot, which):
        start = j * bk if isinstance(j, int) else pl.multiple_of(j * bk, bk)
        k_c = k_ref[pl.ds(start, bk), :]
        for c in which:
            st_ref[slot, c] = _dot(k_c, qt_ref[c])

    def values_t(j):
        return jnp.concatenate([vt_ref[j * kv_tiles + t] for t in range(kv_tiles)], axis=1)

    k_meta = k_ref[rows - KV_TILE:rows, :]
    key = lax.broadcasted_iota(jnp.int32, (KV_TILE, ATTN_QSUB), 0)
    sts = [jnp.where(key >= KV_TILE - N_META, _dot(k_meta, qt_ref[c]), MASK_VALUE) for c in every]
    store_scores(0, 0, every)
    none = [None] * len(chains)
    ms, ls = update(sts, vt_ref[rows // KV_TILE - 1], none, none, every)

    def full_blocks(t, carry):
        ms, ls = carry
        for n in range(nsub):
            j = nsub * t + n
            store_scores(j + 1, (n + 1) % 2, every)
            ms, ls = update([st_ref[n % 2, c] for c in every], values_t(j), ms, ls, every)
        return tuple(ms), tuple(ls)

    ms, ls = lax.fori_loop(0, i, full_blocks, (tuple(ms), tuple(ls)))
    tri = (lax.broadcasted_iota(jnp.int32, (bk, ATTN_QSUB), 0)
           <= lax.broadcasted_iota(jnp.int32, (bk, ATTN_QSUB), 1))
    for d in range(nsub):
        if d + 1 < nsub:
            store_scores(i * nsub + d + 1, (d + 1) % 2, [c for c in every if chains[c][1] > d])
        which = [c for c in every if chains[c][1] >= d]
        sts = [jnp.where(tri, st_ref[d % 2, c], MASK_VALUE) if chains[c][1] == d else st_ref[d % 2, c]
               for c in which]
        ms, ls = update(sts, values_t(i * nsub + d), ms, ls, which)

    lp = lam_ref[...]
    lam = (jnp.exp(jnp.sum(lp[0:1] * lp[1:2], axis=-1, keepdims=True))
           - jnp.exp(jnp.sum(lp[2:3] * lp[3:4], axis=-1, keepdims=True)) + LAMBDA_INIT)
    gain = gain_ref[...]
    for sb in range(nsub):
        ot = acc_ref[sb] * (1.0 / ls[sb]) - acc_ref[nsub + sb] * (lam / ls[nsub + sb])
        ot = ot * lax.rsqrt(jnp.mean(ot * ot, axis=0, keepdims=True) + NORM_EPS) * gain
        o_ref[sb * ATTN_QSUB:(sb + 1) * ATTN_QSUB, :] = (ot * (1.0 - LAMBDA_INIT)).T.astype(BF16)


def _diff_attn(q2, kr, vt, lam_params, gain_col, seq):
    rows = kr.shape[0]
    bq = ATTN_BLOCK
    nchains = 2 * (bq // ATTN_QSUB)
    assert seq % bq == 0 and bq % (2 * ATTN_QSUB) == 0
    return pl.pallas_call(
        _diff_attn_kernel,
        grid=(DIFF_HEADS, seq // bq),
        in_specs=[
            pl.BlockSpec((2, bq, DIFF_VDIM), lambda h, i: (0, i, h)),
            pl.BlockSpec((rows, DIFF_VDIM), lambda h, i: (0, h)),
            pl.BlockSpec((rows // KV_TILE, DIFF_VDIM, KV_TILE), lambda h, i: (0, h, 0)),
            pl.BlockSpec((4, DIFF_DIM), lambda h, i: (0, 0)),
            pl.BlockSpec((DIFF_VDIM, 1), lambda h, i: (0, 0)),
        ],
        out_specs=pl.BlockSpec((bq, DIFF_VDIM), lambda h, i: (i, h)),
        out_shape=jax.ShapeDtypeStruct((seq, DIFF_WIDTH), BF16),
        scratch_shapes=[pltpu.VMEM((nchains, DIFF_VDIM, ATTN_QSUB), F32),
                        pltpu.VMEM((2, nchains, ATTN_QSUB, ATTN_QSUB), F32),
                        pltpu.VMEM((nchains, DIFF_VDIM, ATTN_QSUB), BF16)],
        compiler_params=_params(("parallel", "arbitrary")),
        name="diff_attn",
    )(q2, kr, vt, lam_params, gain_col)


def _out_proj_kernel(mg_ref, md_ref, wg_ref, wd_ref, h_ref, gain_ref, h2_ref, n2_ref):
    h2 = (h_ref[...] + _dot(mg_ref[...], wg_ref[...].astype(BF16))
          + _dot(md_ref[...], wd_ref[...].astype(BF16)))
    h2_ref[...] = h2
    ms = jnp.mean(h2 * h2, axis=-1, keepdims=True)
    n2_ref[...] = (h2 * lax.rsqrt(ms + NORM_EPS) * gain_ref[...]).astype(BF16)


def _out_proj(mix_g, mix_d, w_out, h, gain, seq):
    tm = _pick(seq, (512, 128))
    return pl.pallas_call(
        _out_proj_kernel,
        grid=(seq // tm,),
        in_specs=[
            pl.BlockSpec((tm, GDN_WIDTH), lambda i: (i, 0)),
            pl.BlockSpec((tm, DIFF_WIDTH), lambda i: (i, 0)),
            pl.BlockSpec((GDN_WIDTH, D_MODEL), lambda i: (0, 0), pipeline_mode=pl.Buffered(1)),
            pl.BlockSpec((DIFF_WIDTH, D_MODEL), lambda i: (1, 0), pipeline_mode=pl.Buffered(1)),
            pl.BlockSpec((tm, D_MODEL), lambda i: (i, 0)),
            pl.BlockSpec((1, D_MODEL), lambda i: (0, 0)),
        ],
        out_specs=[
            pl.BlockSpec((tm, D_MODEL), lambda i: (i, 0)),
            pl.BlockSpec((tm, D_MODEL), lambda i: (i, 0)),
        ],
        out_shape=[
            jax.ShapeDtypeStruct((seq, D_MODEL), F32),
            jax.ShapeDtypeStruct((seq, D_MODEL), BF16),
        ],
        compiler_params=_params(("parallel",)),
        name="out_proj",
    )(mix_g, mix_d, w_out, w_out, h, gain)


def _gate_up_kernel(n_ref, wg_ref, wu_ref, a_ref):
    n = n_ref[...]
    g = _dot(n, wg_ref[...].astype(BF16))
    u = _dot(n, wu_ref[...].astype(BF16))
    a_ref[...] = (_silu(g) * u).astype(BF16)


def _gate_up(n2, w_gu):
    seq = n2.shape[0]
    tm = _pick(seq, (2048, 1024, 128))
    tn = 512
    nt = D_FF // tn
    return pl.pallas_call(
        _gate_up_kernel,
        grid=(nt, seq // tm),
        in_specs=[
            pl.BlockSpec((tm, D_MODEL), lambda j, i: (i, 0)),
            pl.BlockSpec((D_MODEL, tn), lambda j, i: (0, j)),
            pl.BlockSpec((D_MODEL, tn), lambda j, i: (0, j + nt)),
        ],
        out_specs=pl.BlockSpec((tm, tn), lambda j, i: (i, j)),
        out_shape=jax.ShapeDtypeStruct((seq, D_FF), BF16),
        compiler_params=_params(("parallel", "parallel")),
        name="ffn_gate_up",
    )(n2, w_gu, w_gu)


def _down_kernel(a_ref, w_ref, h_ref, o_ref):
    o_ref[...] = h_ref[...] + _dot(a_ref[...], w_ref[...].astype(BF16))


def _down(act, w_down, h2):
    seq = act.shape[0]
    tm = _pick(seq, (512, 128))
    tn = 512
    return pl.pallas_call(
        _down_kernel,
        grid=(D_MODEL // tn, seq // tm),
        in_specs=[
            pl.BlockSpec((tm, D_FF), lambda j, i: (i, 0)),
            pl.BlockSpec((D_FF, tn), lambda j, i: (0, j)),
            pl.BlockSpec((tm, tn), lambda j, i: (i, j)),
        ],
        out_specs=pl.BlockSpec((tm, tn), lambda j, i: (i, j)),
        out_shape=jax.ShapeDtypeStruct((seq, D_MODEL), F32),
        compiler_params=_params(("parallel", "parallel")),
        name="ffn_down",
    )(act, w_down, h2)


def _rope_tables(seq):
    half = DIFF_DIM // 2
    pos = jnp.concatenate([jnp.arange(seq) + N_META, jnp.zeros((META_BLOCK - N_META,), jnp.int32),
                           jnp.arange(N_META)]).astype(F32)
    inv_freq = ROPE_THETA ** (-jnp.arange(half, dtype=F32) / half)
    ang = pos[:, None] * inv_freq[None, :]
    cos = jnp.tile(jnp.cos(ang), (1, LANES // half))
    sin = jnp.sin(ang)
    sin = jnp.tile(jnp.concatenate([-sin, sin], axis=1), (1, LANES // DIFF_DIM))
    return cos, sin


def _lane_pad(v, offset):
    return jnp.zeros((1, GATE_LANES), F32).at[0, offset:offset + v.shape[0]].set(v.astype(F32))


def kernel(x, meta_tokens, attn_norm, w_in, conv_w, a_log, dt_bias, gdn_norm, q_norm, k_norm,
           lambda_q1, lambda_k1, lambda_q2, lambda_k2, diff_norm, w_out, ffn_norm, w_gate_up, w_down):
    assert x.shape[0] == 1 and x.shape[2] == D_MODEL
    seq = x.shape[1]
    assert seq % META_BLOCK == 0
    xs = x[0]
    meta_block = jnp.concatenate([jnp.zeros((META_BLOCK - N_META, D_MODEL), xs.dtype),
                                  meta_tokens.astype(xs.dtype)], axis=0)

    gdn_cols = 4 * GDN_WIDTH
    wt_in = w_in[0].T
    wt_diff = wt_in[gdn_cols + 2 * GDN_HEADS:]
    wt_ba = jnp.pad(wt_in[gdn_cols:gdn_cols + 2 * GDN_HEADS],
                    ((0, GATE_LANES - 2 * GDN_HEADS), (0, 0))).astype(BF16)

    n1, gcol, grow = _prenorm_gate(xs, meta_block, attn_norm, wt_ba, _lane_pad(a_log[0], GDN_HEADS),
                                   _lane_pad(dt_bias[0], GDN_HEADS))
    proj = _in_proj(n1, wt_in, wt_diff, conv_w[0])

    w, u, qg, kd, attn, egl = _gdn_local(proj, gcol, grow)
    mix_g = _gdn_state(w, u, qg, kd, attn, egl, proj, gdn_norm, seq)

    cos, sin = _rope_tables(seq)
    tile2 = lambda g: jnp.tile(g.astype(F32), (1, LANES // DIFF_DIM))
    lane = np.arange(LANES)
    gsum = jnp.asarray((lane[:, None] // DIFF_DIM) == (lane[None, :] // DIFF_DIM), BF16)
    q2, kr, vt = _attn_prep(proj, cos, sin, tile2(q_norm), tile2(k_norm), gsum)
    lam_params = jnp.concatenate([lambda_q1, lambda_k1, lambda_q2, lambda_k2], axis=0).astype(F32)
    mix_d = _diff_attn(q2, kr, vt, lam_params, diff_norm.astype(F32).reshape(DIFF_VDIM, 1), seq)

    h2, n2 = _out_proj(mix_g, mix_d, w_out[0], xs, ffn_norm, seq)
    act = _gate_up(n2, w_gate_up[0])
    out = _down(act, w_down[0], h2)
    return out[None]
```

```python
import functools
import math

import jax
import jax.numpy as jnp
import numpy as np
from jax import lax
from jax.experimental import pallas as pl
from jax.experimental.pallas import tpu as pltpu

F32 = jnp.float32
BF16 = jnp.bfloat16

D_MODEL = 2048
N_META = 16
GDN_HEADS = 8
GDN_DIM = 128
GDN_WIDTH = GDN_HEADS * GDN_DIM
CONV_WIDTH = 4
CHUNK = 64
DIFF_HEADS = 8
DIFF_DIM = 64
DIFF_VDIM = 2 * DIFF_DIM
DIFF_WIDTH = DIFF_HEADS * DIFF_VDIM
ROPE_THETA = 10000.0
D_FF = 5632
NORM_EPS = 1e-6
MASK_VALUE = -1e30
LAMBDA_INIT = 0.8 - 0.6 * math.exp(-0.3 * 0)

LANES = 128
META_BLOCK = 512
GATE_LANES = 128
VMEM_LIMIT = 56 * 1024 * 1024


def _pick(n, candidates):
    for c in candidates:
        if n % c == 0:
            return c
    raise ValueError(f"no tile in {candidates} divides {n}")


def _params(sem, vmem=VMEM_LIMIT):
    return pltpu.CompilerParams(dimension_semantics=sem, vmem_limit_bytes=vmem)


def _dot(a, b):
    return jnp.dot(a, b, preferred_element_type=F32)


def _dot_nt(a, b):
    return lax.dot_general(a, b, (((1,), (1,)), ((), ())), preferred_element_type=F32)


def _dot_tn(a, b):
    return lax.dot_general(a, b, (((0,), (0,)), ((), ())), preferred_element_type=F32)


def _softplus(x):
    return jnp.maximum(x, 0.0) + jnp.log1p(jnp.exp(-jnp.abs(x)))


def _silu(x):
    return x * jax.nn.sigmoid(x)


def _prenorm_gate_kernel(x_ref, mb_ref, gain_ref, wba_ref, alog_ref, dtb_ref, n_ref, gcol_ref, grow_ref):
    h = jnp.where(pl.program_id(0) < pl.num_programs(0) - 1, x_ref[...], mb_ref[...])
    ms = jnp.mean(h * h, axis=-1, keepdims=True)
    n = (h * lax.rsqrt(ms + NORM_EPS) * gain_ref[...]).astype(BF16)
    n_ref[...] = n
    ba = _dot_nt(n, wba_ref[...])
    beta = jax.nn.sigmoid(ba)
    g = -jnp.exp(alog_ref[...]) * _softplus(ba + dtb_ref[...])
    row = lax.broadcasted_iota(jnp.int32, ba.shape, 0) % CHUNK
    gc = g
    for d in (1, 2, 4, 8, 16, 32):
        gc = gc + jnp.where(row >= d, pltpu.roll(gc, d, axis=0), 0.0)
    lane = lax.broadcasted_iota(jnp.int32, ba.shape, 1)
    out = jnp.where(lane < GDN_HEADS, beta, gc)
    gcol_ref[...] = out
    grow_ref[...] = out.T[: 2 * GDN_HEADS]


def _prenorm_gate(x, meta_block, gain, wba, alog, dtb):
    tm = META_BLOCK
    nx = x.shape[0] // tm
    rows = x.shape[0] + tm
    return pl.pallas_call(
        _prenorm_gate_kernel,
        grid=(nx + 1,),
        in_specs=[
            pl.BlockSpec((tm, D_MODEL), lambda i: (jnp.minimum(i, nx - 1), 0)),
            pl.BlockSpec((tm, D_MODEL), lambda i: (0, 0)),
            pl.BlockSpec((1, D_MODEL), lambda i: (0, 0)),
            pl.BlockSpec((GATE_LANES, D_MODEL), lambda i: (0, 0)),
            pl.BlockSpec((1, GATE_LANES), lambda i: (0, 0)),
            pl.BlockSpec((1, GATE_LANES), lambda i: (0, 0)),
        ],
        out_specs=[
            pl.BlockSpec((tm, D_MODEL), lambda i: (i, 0)),
            pl.BlockSpec((tm, GATE_LANES), lambda i: (i, 0)),
            pl.BlockSpec((2 * GDN_HEADS, tm), lambda i: (0, i)),
        ],
        out_shape=[
            jax.ShapeDtypeStruct((rows, D_MODEL), BF16),
            jax.ShapeDtypeStruct((rows, GATE_LANES), F32),
            jax.ShapeDtypeStruct((2 * GDN_HEADS, rows), F32),
        ],
        compiler_params=_params(("parallel",)),
        name="prenorm_gate",
    )(x, meta_block, gain, wba, alog, dtb)


IN_PROJ_TN = 1024
GDN_COL_TILES = 4 * GDN_WIDTH // IN_PROJ_TN


def _causal_conv_silu(x, prev, w):
    r8 = lax.broadcasted_iota(jnp.int32, prev.shape, 0)
    y = x * w[CONV_WIDTH - 1:CONV_WIDTH]
    for d in range(1, CONV_WIDTH):
        shifted = pltpu.roll(x, d, axis=0)
        top = jnp.where(r8 < d, pltpu.roll(prev, d, axis=0), shifted[:8])
        shifted = jnp.concatenate([top, shifted[8:]], axis=0)
        y = y + shifted * w[CONV_WIDTH - 1 - d:CONV_WIDTH - d]
    return _silu(y)


def _in_proj_kernel(a_ref, wg_ref, wd_ref, cw_ref, o_ref, tail_ref, raw_ref, w16_ref):
    j = pl.program_id(0)
    i = pl.program_id(1)
    tm = a_ref.shape[0]

    @pl.when(i == 0)
    def _():
        tail_ref[...] = jnp.zeros_like(tail_ref)

    @pl.when((i == 0) & (j < GDN_COL_TILES))
    def _():
        w16_ref[...] = wg_ref[...].T.astype(BF16)

    @pl.when((i == 0) & (j >= GDN_COL_TILES))
    def _():
        w16_ref[...] = wd_ref[...].T.astype(BF16)

    def gdn_qkv(l2_scale):
        a = a_ref[...]
        for pair in range(GDN_HEADS // 2):
            cs = slice(pair * 2 * GDN_DIM, (pair + 1) * 2 * GDN_DIM)
            raw_ref[pair] = _dot(a, w16_ref[:, cs])
        for pair in range(GDN_HEADS // 2):
            cs = slice(pair * 2 * GDN_DIM, (pair + 1) * 2 * GDN_DIM)
            raw = raw_ref[pair]
            y = _causal_conv_silu(raw, tail_ref[:, cs], cw_ref[:, cs])
            tail_ref[:, cs] = raw[tm - 8:]
            for half in range(2):
                ls = slice(half * GDN_DIM, (half + 1) * GDN_DIM)
                yh = y[:, ls]
                if l2_scale is not None:
                    yh = yh * (lax.rsqrt(jnp.sum(yh * yh, axis=-1, keepdims=True) + NORM_EPS) * l2_scale)
                o_ref[:, pair * 2 * GDN_DIM + half * GDN_DIM:pair * 2 * GDN_DIM + (half + 1) * GDN_DIM] = yh

    pl.when(j == 0)(functools.partial(gdn_qkv, GDN_DIM ** -0.5))
    pl.when(j == 1)(functools.partial(gdn_qkv, 1.0))
    pl.when(j == 2)(functools.partial(gdn_qkv, None))

    @pl.when(j >= GDN_COL_TILES - 1)
    def _():
        o_ref[...] = _dot(a_ref[...], w16_ref[...])


def _in_proj(n1, wt_all, wt_diff, conv_w):
    m = n1.shape[0]
    tm, tn = META_BLOCK, IN_PROJ_TN
    nm = m // tm
    n = GDN_COL_TILES * tn + wt_diff.shape[0]
    seq_order = lambda i: (i + nm - 1) % nm
    return pl.pallas_call(
        _in_proj_kernel,
        grid=(n // tn, nm),
        in_specs=[
            pl.BlockSpec((tm, D_MODEL), lambda j, i: (seq_order(i), 0)),
            pl.BlockSpec((tn, D_MODEL), lambda j, i: (jnp.minimum(j, GDN_COL_TILES - 1), 0)),
            pl.BlockSpec((tn, D_MODEL), lambda j, i: (jnp.maximum(j - GDN_COL_TILES, 0), 0)),
            pl.BlockSpec((CONV_WIDTH, tn), lambda j, i: (0, jnp.minimum(j, 2))),
        ],
        out_specs=pl.BlockSpec((tm, tn), lambda j, i: (seq_order(i), j)),
        out_shape=jax.ShapeDtypeStruct((m, n), F32),
        scratch_shapes=[pltpu.VMEM((8, tn), F32), pltpu.VMEM((GDN_HEADS // 2, tm, 2 * GDN_DIM), F32),
                        pltpu.VMEM((D_MODEL, tn), BF16)],
        compiler_params=_params(("parallel", "arbitrary")),
        name="in_proj",
    )(n1, wt_all, wt_diff, conv_w)


GDN_LOCAL_CHUNKS = 4


def _gdn_local_kernel(q_ref, k_ref, v_ref, gcol_ref, grow_ref,
                      w_ref, u_ref, qg_ref, kd_ref, attn_ref, egl_ref):
    q_all = q_ref[...]
    k_all = k_ref[...]
    v_all = v_ref[...]
    gcol = gcol_ref[...]
    grow = grow_ref[...]
    ii = lax.broadcasted_iota(jnp.int32, (CHUNK, CHUNK), 0)
    jj = lax.broadcasted_iota(jnp.int32, (CHUNK, CHUNK), 1)
    units = [(c, h) for c in range(GDN_LOCAL_CHUNKS) for h in range(GDN_HEADS)]
    rows_of = lambda c: slice(c * CHUNK, (c + 1) * CHUNK)
    lanes_of = lambda h: slice(h * GDN_DIM, (h + 1) * GDN_DIM)
    qs, ks, kbs, vbs, egcs, gcs, kqs = [], [], [], [], [], [], []
    for c, h in units:
        rs, ls = rows_of(c), lanes_of(h)
        q = q_all[rs, ls]
        k = k_all[rs, ls]
        beta_c = gcol[rs, h:h + 1]
        gc_c = gcol[rs, GDN_HEADS + h:GDN_HEADS + h + 1]
        kb = k * beta_c
        qs.append(q)
        ks.append(k)
        kbs.append(kb)
        vbs.append(v_all[rs, ls] * beta_c)
        gcs.append(gc_c)
        egcs.append(jnp.exp(gc_c))
        kqs.append(_dot_nt(jnp.concatenate([kb.astype(BF16), q.astype(BF16)], axis=0), k.astype(BF16)))
    lms, attns = [], []
    for (c, h), kq, gc_c in zip(units, kqs, gcs):
        gc_r = grow[GDN_HEADS + h:GDN_HEADS + h + 1, rows_of(c)]
        decay = jnp.exp(jnp.where(ii >= jj, gc_c - gc_r, MASK_VALUE))
        lms.append(jnp.where(ii > jj, kq[:CHUNK] * decay, 0.0))
        attns.append(kq[CHUNK:] * decay)
    xor = ii ^ jj
    eye = jnp.where(ii == jj, 1.0, 0.0)
    xs = [eye - jnp.where(xor == 1, lm, 0.0) for lm in lms]
    level = 1
    while (2 << level) <= CHUNK:
        sel = (xor >> level) == 1
        ys = [_dot(jnp.where(sel, lm, 0.0).astype(BF16), x.astype(BF16)) for lm, x in zip(lms, xs)]
        xs = [x - _dot(x.astype(BF16), y.astype(BF16)) for x, y in zip(xs, ys)]
        level += 1
    uws = [_dot(x.astype(BF16), jnp.concatenate([vb.astype(BF16), (kb * egc).astype(BF16)], axis=1))
           for x, vb, kb, egc in zip(xs, vbs, kbs, egcs)]
    for (c, h), uw, q, k, egc, gc_c, attn in zip(units, uws, qs, ks, egcs, gcs, attns):
        rs, ls = rows_of(c), lanes_of(h)
        u_ref[rs, ls] = uw[:, :GDN_DIM]
        w_ref[rs, ls] = uw[:, GDN_DIM:].astype(BF16)
        qg_ref[rs, ls] = (q * egc).astype(BF16)
        gc_last = gc_c[CHUNK - 1:CHUNK]
        kd_ref[rs, ls] = (k * jnp.exp(gc_last - gc_c)).astype(BF16)
        attn_ref[h, rs, :] = attn.astype(BF16)
        egl_ref[c, h:h + 1, :] = jnp.broadcast_to(jnp.exp(gc_last), (1, GDN_DIM))


def _gdn_local(proj, gcol, grow):
    rows = proj.shape[0]
    rb = GDN_LOCAL_CHUNKS * CHUNK
    bidx = lambda i: i
    blk = lambda col: pl.BlockSpec((rb, GDN_WIDTH), lambda i, col=col: (bidx(i), col))
    row_out = lambda dt: jax.ShapeDtypeStruct((rows, GDN_WIDTH), dt)
    return pl.pallas_call(
        _gdn_local_kernel,
        grid=(rows // rb,),
        in_specs=[
            blk(0), blk(1), blk(2),
            pl.BlockSpec((rb, GATE_LANES), lambda i: (bidx(i), 0)),
            pl.BlockSpec((2 * GDN_HEADS, rb), lambda i: (0, bidx(i))),
        ],
        out_specs=[
            pl.BlockSpec((rb, GDN_WIDTH), lambda i: (bidx(i), 0)),
            pl.BlockSpec((rb, GDN_WIDTH), lambda i: (bidx(i), 0)),
            pl.BlockSpec((rb, GDN_WIDTH), lambda i: (bidx(i), 0)),
            pl.BlockSpec((rb, GDN_WIDTH), lambda i: (bidx(i), 0)),
            pl.BlockSpec((GDN_HEADS, rb, CHUNK), lambda i: (0, bidx(i), 0)),
            pl.BlockSpec((GDN_LOCAL_CHUNKS, GDN_HEADS, GDN_DIM), lambda i: (bidx(i), 0, 0)),
        ],
        out_shape=[
            row_out(BF16),
            row_out(F32),
            row_out(BF16),
            row_out(BF16),
            jax.ShapeDtypeStruct((GDN_HEADS, rows, CHUNK), BF16),
            jax.ShapeDtypeStruct((rows // CHUNK, GDN_HEADS, GDN_DIM), F32),
        ],
        compiler_params=_params(("parallel",)),
        name="gdn_local",
    )(proj, proj, proj, gcol, grow)


GDN_STATE_CHUNKS = 4


def _gdn_state_kernel(w_ref, u_ref, qg_ref, kd_ref, attn_ref, egl_ref, z_ref, gain_ref, o_ref, s_ref):
    @pl.when(pl.program_id(0) == 0)
    def _():
        s_ref[...] = jnp.zeros_like(s_ref)

    gain = gain_ref[...]
    heads = range(GDN_HEADS)
    lanes = [slice(h * GDN_DIM, (h + 1) * GDN_DIM) for h in heads]
    ss = [s_ref[h] for h in heads]
    for ck in range(GDN_STATE_CHUNKS):
        rw = slice(ck * CHUNK, (ck + 1) * CHUNK)
        rs = [_dot(jnp.concatenate([w_ref[rw, ls], qg_ref[rw, ls]], axis=0), s.astype(BF16))
              for ls, s in zip(lanes, ss)]
        vns = [(u_ref[rw, ls] - r[:CHUNK]).astype(BF16) for ls, r in zip(lanes, rs)]
        os_ = [r[CHUNK:] + _dot(attn_ref[h, rw, :], vn) for h, r, vn in zip(heads, rs, vns)]
        ss = [s * egl_ref[ck, h:h + 1, :] + _dot_tn(kd_ref[rw, ls], vn)
              for h, ls, s, vn in zip(heads, lanes, ss, vns)]
        for ls, o in zip(lanes, os_):
            on = o * lax.rsqrt(jnp.mean(o * o, axis=-1, keepdims=True) + NORM_EPS) * gain
            o_ref[rw, ls] = (on * _silu(z_ref[rw, ls])).astype(BF16)
    for h, s in zip(heads, ss):
        s_ref[h] = s


def _gdn_state(w, u, qg, kd, attn, egl, proj, gain, seq):
    rows = w.shape[0]
    rb = GDN_STATE_CHUNKS * CHUNK
    nblocks = rows // rb
    phys = lambda c: (c + nblocks - 1) % nblocks
    rowblk = lambda col: pl.BlockSpec((rb, GDN_WIDTH), lambda c, col=col: (phys(c), col))
    return pl.pallas_call(
        _gdn_state_kernel,
        grid=(seq // rb + 1,),
        in_specs=[
            rowblk(0), rowblk(0), rowblk(0), rowblk(0),
            pl.BlockSpec((GDN_HEADS, rb, CHUNK), lambda c: (0, phys(c), 0)),
            pl.BlockSpec((GDN_STATE_CHUNKS, GDN_HEADS, GDN_DIM), lambda c: (phys(c), 0, 0)),
            rowblk(3),
            pl.BlockSpec((1, GDN_DIM), lambda c: (0, 0)),
        ],
        out_specs=pl.BlockSpec((rb, GDN_WIDTH), lambda c: (jnp.maximum(c - 1, 0), 0)),
        out_shape=jax.ShapeDtypeStruct((seq, GDN_WIDTH), BF16),
        scratch_shapes=[pltpu.VMEM((GDN_HEADS, GDN_DIM, GDN_DIM), F32)],
        compiler_params=_params(("arbitrary",)),
        name="gdn_state",
    )(w, u, qg, kd, attn, egl, proj, gain)


KV_TILE = 128
VT_ROWS = DIFF_VDIM + 16
Q_SCALE = DIFF_DIM ** -0.5 * math.log2(math.e)


PREP_ROWS = 2 * KV_TILE


def _attn_prep_kernel(q_ref, k_ref, v_ref, cos_ref, sin_ref, qg_ref, kg_ref, gsum_ref, eye_ref,
                      q2_ref, kr_ref, vt_ref):
    cos = cos_ref[...]
    sin = sin_ref[...]
    gsum = gsum_ref[...]
    eye = eye_ref[...]
    lane = lax.broadcasted_iota(jnp.int32, cos.shape, 1)
    first_half = (lane % DIFF_DIM) < (DIFF_DIM // 2)
    low_map = lane < DIFF_DIM

    def norm_rope(x, gain):
        ms = _dot((x * x).astype(BF16), gsum) * (1.0 / DIFF_DIM)
        xn = x * lax.rsqrt(ms + NORM_EPS) * gain
        rot = jnp.where(first_half, pltpu.roll(xn, LANES - DIFF_DIM // 2, axis=1),
                        pltpu.roll(xn, DIFF_DIM // 2, axis=1))
        return xn * cos + rot * sin

    for h in range(DIFF_HEADS):
        ls = slice(h * DIFF_VDIM, (h + 1) * DIFF_VDIM)
        q = norm_rope(q_ref[:, ls], qg_ref[...]) * Q_SCALE
        q2_ref[0, :, ls] = jnp.where(low_map, q, 0.0).astype(BF16)
        q2_ref[1, :, ls] = jnp.where(low_map, 0.0, q).astype(BF16)
        kr_ref[:, ls] = norm_rope(k_ref[:, ls], kg_ref[...]).astype(BF16)
        for t in range(PREP_ROWS // KV_TILE):
            v = v_ref[t * KV_TILE:(t + 1) * KV_TILE, ls].astype(BF16)
            vt_ref[t, h * VT_ROWS:h * VT_ROWS + DIFF_VDIM, :] = _dot_nt(eye, v).astype(BF16)
            vt_ref[t, h * VT_ROWS + DIFF_VDIM:(h + 1) * VT_ROWS, :] = jnp.ones(
                (VT_ROWS - DIFF_VDIM, KV_TILE), BF16)


def _attn_prep(proj, cos, sin, qgain, kgain, gsum, eye):
    rows = proj.shape[0]
    tm = PREP_ROWS
    bidx = lambda i: i
    col = lambda c: pl.BlockSpec((tm, DIFF_WIDTH), lambda i, c=c: (bidx(i), c))
    small = lambda shape: pl.BlockSpec(shape, lambda i: (0,) * len(shape))
    return pl.pallas_call(
        _attn_prep_kernel,
        grid=(rows // tm,),
        in_specs=[
            col(4), col(5), col(6),
            pl.BlockSpec((tm, LANES), lambda i: (bidx(i), 0)),
            pl.BlockSpec((tm, LANES), lambda i: (bidx(i), 0)),
            small((1, LANES)), small((1, LANES)), small((LANES, LANES)), small((LANES, LANES)),
        ],
        out_specs=[
            pl.BlockSpec((2, tm, DIFF_WIDTH), lambda i: (0, bidx(i), 0)),
            pl.BlockSpec((tm, DIFF_WIDTH), lambda i: (bidx(i), 0)),
            pl.BlockSpec((tm // KV_TILE, DIFF_HEADS * VT_ROWS, KV_TILE), lambda i: (bidx(i), 0, 0)),
        ],
        out_shape=[
            jax.ShapeDtypeStruct((2, rows, DIFF_WIDTH), BF16),
            jax.ShapeDtypeStruct((rows, DIFF_WIDTH), BF16),
            jax.ShapeDtypeStruct((rows // KV_TILE, DIFF_HEADS * VT_ROWS, KV_TILE), BF16),
        ],
        compiler_params=_params(("parallel",)),
        name="attn_prep",
    )(proj, proj, proj, cos, sin, qgain, kgain, gsum, eye)


ATTN_BLOCK = 1024
ATTN_QSUB = 256

def _diff_attn_kernel(q_ref, k_ref, vt_ref, lam_ref, gain_ref, o_ref, acc_ref, st_ref, qt_ref):
    i = pl.program_id(1)
    rows = k_ref.shape[0]
    bk = ATTN_QSUB
    nsub = q_ref.shape[1] // ATTN_QSUB
    chains = [(mp, sb) for mp in range(2) for sb in range(nsub)]
    every = list(range(len(chains)))
    qs = [q_ref[mp, sb * ATTN_QSUB:(sb + 1) * ATTN_QSUB, :] for mp, sb in chains]
    eye = jnp.where(lax.broadcasted_iota(jnp.int32, (DIFF_VDIM, DIFF_VDIM), 0)
                    == lax.broadcasted_iota(jnp.int32, (DIFF_VDIM, DIFF_VDIM), 1), 1.0, 0.0).astype(BF16)
    for c, q in enumerate(qs):
        qt_ref[c] = _dot_nt(eye, q).astype(BF16)
    kv_tiles = bk // KV_TILE

    def update(sts, vt, ms, which):
        ms = list(ms)
        first = ms[which[0]] is None
        cms = [jnp.max(st, axis=0, keepdims=True) for st in sts]
        m_new = cms if first else [jnp.maximum(ms[c], cm) for c, cm in zip(which, cms)]
        ps = [jnp.exp2(st - mn).astype(BF16) for st, mn in zip(sts, m_new)]
        pvs = [_dot(vt, p) for p in ps]
        for n, c in enumerate(which):
            if first:
                acc_ref[c] = pvs[n]
            else:
                acc_ref[c] = jnp.exp2(ms[c] - m_new[n]) * acc_ref[c] + pvs[n]
            ms[c] = m_new[n]
        return ms

    def store_scores(j, slot, which):
        start = j * bk if isinstance(j, int) else pl.multiple_of(j * bk, bk)
        k_c = k_ref[pl.ds(start, bk), :]
        for c in which:
            st_ref[slot, c] = _dot(k_c, qt_ref[c])

    def values_t(j):
        return jnp.concatenate([vt_ref[j * kv_tiles + t] for t in range(kv_tiles)], axis=1)

    k_meta = k_ref[rows - KV_TILE:rows, :]
    key = lax.broadcasted_iota(jnp.int32, (KV_TILE, ATTN_QSUB), 0)
    sts = [jnp.where(key >= KV_TILE - N_META, _dot(k_meta, qt_ref[c]), MASK_VALUE) for c in every]
    store_scores(0, 0, every)
    ms = update(sts, vt_ref[rows // KV_TILE - 1], [None] * len(chains), every)

    def full_blocks(t, ms):
        for n in range(nsub):
            j = nsub * t + n
            store_scores(j + 1, (n + 1) % 2, every)
            ms = update([st_ref[n % 2, c] for c in every], values_t(j), ms, every)
        return tuple(ms)

    ms = lax.fori_loop(0, i, full_blocks, tuple(ms))
    tri = (lax.broadcasted_iota(jnp.int32, (bk, ATTN_QSUB), 0)
           <= lax.broadcasted_iota(jnp.int32, (bk, ATTN_QSUB), 1))
    for d in range(nsub):
        if d + 1 < nsub:
            store_scores(i * nsub + d + 1, (d + 1) % 2, [c for c in every if chains[c][1] > d])
        which = [c for c in every if chains[c][1] >= d]
        sts = [jnp.where(tri, st_ref[d % 2, c], MASK_VALUE) if chains[c][1] == d else st_ref[d % 2, c]
               for c in which]
        ms = update(sts, values_t(i * nsub + d), ms, which)

    lp = lam_ref[...]
    lam = (jnp.exp(jnp.sum(lp[0:1] * lp[1:2], axis=-1, keepdims=True))
           - jnp.exp(jnp.sum(lp[2:3] * lp[3:4], axis=-1, keepdims=True)) + LAMBDA_INIT)
    gain = gain_ref[...]
    for sb in range(nsub):
        num1, num2 = acc_ref[sb, :DIFF_VDIM, :], acc_ref[nsub + sb, :DIFF_VDIM, :]
        l1 = acc_ref[sb, DIFF_VDIM:DIFF_VDIM + 1, :]
        l2 = acc_ref[nsub + sb, DIFF_VDIM:DIFF_VDIM + 1, :]
        ot = num1 * (1.0 / l1) - num2 * (lam / l2)
        ot = ot * lax.rsqrt(jnp.mean(ot * ot, axis=0, keepdims=True) + NORM_EPS) * gain
        o_ref[sb * ATTN_QSUB:(sb + 1) * ATTN_QSUB, :] = (ot * (1.0 - LAMBDA_INIT)).T.astype(BF16)


def _diff_attn(q2, kr, vt, lam_params, gain_col, seq):
    rows = kr.shape[0]
    bq = ATTN_BLOCK
    nchains = 2 * (bq // ATTN_QSUB)
    assert seq % bq == 0 and bq % (2 * ATTN_QSUB) == 0
    return pl.pallas_call(
        _diff_attn_kernel,
        grid=(DIFF_HEADS, seq // bq),
        in_specs=[
            pl.BlockSpec((2, bq, DIFF_VDIM), lambda h, i: (0, i, h)),
            pl.BlockSpec((rows, DIFF_VDIM), lambda h, i: (0, h)),
            pl.BlockSpec((rows // KV_TILE, VT_ROWS, KV_TILE), lambda h, i: (0, h, 0)),
            pl.BlockSpec((4, DIFF_DIM), lambda h, i: (0, 0)),
            pl.BlockSpec((DIFF_VDIM, 1), lambda h, i: (0, 0)),
        ],
        out_specs=pl.BlockSpec((bq, DIFF_VDIM), lambda h, i: (i, h)),
        out_shape=jax.ShapeDtypeStruct((seq, DIFF_WIDTH), BF16),
        scratch_shapes=[pltpu.VMEM((nchains, VT_ROWS, ATTN_QSUB), F32),
                        pltpu.VMEM((2, nchains, ATTN_QSUB, ATTN_QSUB), F32),
                        pltpu.VMEM((nchains, DIFF_VDIM, ATTN_QSUB), BF16)],
        compiler_params=_params(("parallel", "arbitrary")),
        name="diff_attn",
    )(q2, kr, vt, lam_params, gain_col)


def _out_proj_kernel(mg_ref, md_ref, wg_ref, wd_ref, h_ref, gain_ref, h2_ref, n2_ref):
    h2 = (h_ref[...] + _dot(mg_ref[...], wg_ref[...].astype(BF16))
          + _dot(md_ref[...], wd_ref[...].astype(BF16)))
    h2_ref[...] = h2
    ms = jnp.mean(h2 * h2, axis=-1, keepdims=True)
    n2_ref[...] = (h2 * lax.rsqrt(ms + NORM_EPS) * gain_ref[...]).astype(BF16)


def _out_proj(mix_g, mix_d, w_out, h, gain, seq):
    tm = _pick(seq, (512, 128))
    return pl.pallas_call(
        _out_proj_kernel,
        grid=(seq // tm,),
        in_specs=[
            pl.BlockSpec((tm, GDN_WIDTH), lambda i: (i, 0)),
            pl.BlockSpec((tm, DIFF_WIDTH), lambda i: (i, 0)),
            pl.BlockSpec((GDN_WIDTH, D_MODEL), lambda i: (0, 0), pipeline_mode=pl.Buffered(1)),
            pl.BlockSpec((DIFF_WIDTH, D_MODEL), lambda i: (1, 0), pipeline_mode=pl.Buffered(1)),
            pl.BlockSpec((tm, D_MODEL), lambda i: (i, 0)),
            pl.BlockSpec((1, D_MODEL), lambda i: (0, 0)),
        ],
        out_specs=[
            pl.BlockSpec((tm, D_MODEL), lambda i: (i, 0)),
            pl.BlockSpec((tm, D_MODEL), lambda i: (i, 0)),
        ],
        out_shape=[
            jax.ShapeDtypeStruct((seq, D_MODEL), F32),
            jax.ShapeDtypeStruct((seq, D_MODEL), BF16),
        ],
        compiler_params=_params(("parallel",)),
        name="out_proj",
    )(mix_g, mix_d, w_out, w_out, h, gain)


def _gate_up_kernel(n_ref, wg_ref, wu_ref, a_ref):
    n = n_ref[...]
    g = _dot(n, wg_ref[...].astype(BF16))
    u = _dot(n, wu_ref[...].astype(BF16))
    a_ref[...] = (_silu(g) * u).astype(BF16)


def _gate_up(n2, w_gu):
    seq = n2.shape[0]
    tm = _pick(seq, (2048, 1024, 128))
    tn = 512
    nt = D_FF // tn
    return pl.pallas_call(
        _gate_up_kernel,
        grid=(nt, seq // tm),
        in_specs=[
            pl.BlockSpec((tm, D_MODEL), lambda j, i: (i, 0)),
            pl.BlockSpec((D_MODEL, tn), lambda j, i: (0, j)),
            pl.BlockSpec((D_MODEL, tn), lambda j, i: (0, j + nt)),
        ],
        out_specs=pl.BlockSpec((tm, tn), lambda j, i: (i, j)),
        out_shape=jax.ShapeDtypeStruct((seq, D_FF), BF16),
        compiler_params=_params(("parallel", "parallel")),
        name="ffn_gate_up",
    )(n2, w_gu, w_gu)


def _down_kernel(a_ref, w_ref, h_ref, o_ref):
    o_ref[...] = h_ref[...] + _dot(a_ref[...], w_ref[...].astype(BF16))


def _down(act, w_down, h2):
    seq = act.shape[0]
    tm = _pick(seq, (512, 128))
    tn = 1024
    return pl.pallas_call(
        _down_kernel,
        grid=(D_MODEL // tn, seq // tm),
        in_specs=[
            pl.BlockSpec((tm, D_FF), lambda j, i: (i, 0)),
            pl.BlockSpec((D_FF, tn), lambda j, i: (0, j), pipeline_mode=pl.Buffered(1)),
            pl.BlockSpec((tm, tn), lambda j, i: (i, j)),
        ],
        out_specs=pl.BlockSpec((tm, tn), lambda j, i: (i, j)),
        out_shape=jax.ShapeDtypeStruct((seq, D_MODEL), F32),
        compiler_params=_params(("parallel", "parallel")),
        name="ffn_down",
    )(act, w_down, h2)


def _rope_tables(seq):
    half = DIFF_DIM // 2
    pos = jnp.concatenate([jnp.arange(seq) + N_META, jnp.zeros((META_BLOCK - N_META,), jnp.int32),
                           jnp.arange(N_META)]).astype(F32)
    inv_freq = ROPE_THETA ** (-jnp.arange(half, dtype=F32) / half)
    ang = pos[:, None] * inv_freq[None, :]
    cos = jnp.tile(jnp.cos(ang), (1, LANES // half))
    sin = jnp.sin(ang)
    sin = jnp.tile(jnp.concatenate([-sin, sin], axis=1), (1, LANES // DIFF_DIM))
    return cos, sin


def _lane_pad(v, offset):
    return jnp.zeros((1, GATE_LANES), F32).at[0, offset:offset + v.shape[0]].set(v.astype(F32))


def kernel(x, meta_tokens, attn_norm, w_in, conv_w, a_log, dt_bias, gdn_norm, q_norm, k_norm,
           lambda_q1, lambda_k1, lambda_q2, lambda_k2, diff_norm, w_out, ffn_norm, w_gate_up, w_down):
    assert x.shape[0] == 1 and x.shape[2] == D_MODEL
    seq = x.shape[1]
    assert seq % META_BLOCK == 0
    xs = x[0]
    meta_block = jnp.concatenate([jnp.zeros((META_BLOCK - N_META, D_MODEL), xs.dtype),
                                  meta_tokens.astype(xs.dtype)], axis=0)

    gdn_cols = 4 * GDN_WIDTH
    wt_in = w_in[0].T
    wt_diff = wt_in[gdn_cols + 2 * GDN_HEADS:]
    wt_ba = jnp.pad(wt_in[gdn_cols:gdn_cols + 2 * GDN_HEADS],
                    ((0, GATE_LANES - 2 * GDN_HEADS), (0, 0))).astype(BF16)

    n1, gcol, grow = _prenorm_gate(xs, meta_block, attn_norm, wt_ba, _lane_pad(a_log[0], GDN_HEADS),
                                   _lane_pad(dt_bias[0], GDN_HEADS))
    proj = _in_proj(n1, wt_in, wt_diff, conv_w[0])

    w, u, qg, kd, attn, egl = _gdn_local(proj, gcol, grow)
    mix_g = _gdn_state(w, u, qg, kd, attn, egl, proj, gdn_norm, seq)

    cos, sin = _rope_tables(seq)
    tile2 = lambda g: jnp.tile(g.astype(F32), (1, LANES // DIFF_DIM))
    lane = np.arange(LANES)
    gsum = jnp.asarray((lane[:, None] // DIFF_DIM) == (lane[None, :] // DIFF_DIM), BF16)
    eye = jnp.asarray(lane[:, None] == lane[None, :], BF16)
    q2, kr, vt = _attn_prep(proj, cos, sin, tile2(q_norm), tile2(k_norm), gsum, eye)
    lam_params = jnp.concatenate([lambda_q1, lambda_k1, lambda_q2, lambda_k2], axis=0).astype(F32)
    mix_d = _diff_attn(q2, kr, vt, lam_params, diff_norm.astype(F32).reshape(DIFF_VDIM, 1), seq)

    h2, n2 = _out_proj(mix_g, mix_d, w_out[0], xs, ffn_norm, seq)
    act = _gate_up(n2, w_gate_up[0])
    out = _down(act, w_down[0], h2)
    return out[None]
```

```python
import functools
import math

import jax
import jax.numpy as jnp
import numpy as np
from jax import lax
from jax.experimental import pallas as pl
from jax.experimental.pallas import tpu as pltpu

F32 = jnp.float32
BF16 = jnp.bfloat16

D_MODEL = 2048
N_META = 16
GDN_HEADS = 8
GDN_DIM = 128
GDN_WIDTH = GDN_HEADS * GDN_DIM
CONV_WIDTH = 4
CHUNK = 64
DIFF_HEADS = 8
DIFF_DIM = 64
DIFF_VDIM = 2 * DIFF_DIM
DIFF_WIDTH = DIFF_HEADS * DIFF_VDIM
ROPE_THETA = 10000.0
D_FF = 5632
NORM_EPS = 1e-6
MASK_VALUE = -1e30
LAMBDA_INIT = 0.8 - 0.6 * math.exp(-0.3 * 0)

LANES = 128
META_BLOCK = 512
GATE_LANES = 128
VMEM_LIMIT = 56 * 1024 * 1024


def _pick(n, candidates):
    for c in candidates:
        if n % c == 0:
            return c
    raise ValueError(f"no tile in {candidates} divides {n}")


def _params(sem, vmem=VMEM_LIMIT):
    return pltpu.CompilerParams(dimension_semantics=sem, vmem_limit_bytes=vmem)


def _dot(a, b):
    return jnp.dot(a, b, preferred_element_type=F32)


def _dot_nt(a, b):
    return lax.dot_general(a, b, (((1,), (1,)), ((), ())), preferred_element_type=F32)


def _dot_tn(a, b):
    return lax.dot_general(a, b, (((0,), (0,)), ((), ())), preferred_element_type=F32)


def _softplus(x):
    return jnp.maximum(x, 0.0) + jnp.log1p(jnp.exp(-jnp.abs(x)))


def _silu(x):
    return x * jax.nn.sigmoid(x)


def _prenorm_gate_kernel(x_ref, mb_ref, gain_ref, wba_ref, alog_ref, dtb_ref, n_ref, gcol_ref, grow_ref):
    h = jnp.where(pl.program_id(0) < pl.num_programs(0) - 1, x_ref[...], mb_ref[...])
    ms = jnp.mean(h * h, axis=-1, keepdims=True)
    n = (h * lax.rsqrt(ms + NORM_EPS) * gain_ref[...]).astype(BF16)
    n_ref[...] = n
    ba = _dot_nt(n, wba_ref[...])
    beta = jax.nn.sigmoid(ba)
    g = -jnp.exp(alog_ref[...]) * _softplus(ba + dtb_ref[...])
    row = lax.broadcasted_iota(jnp.int32, ba.shape, 0) % CHUNK
    gc = g
    for d in (1, 2, 4, 8, 16, 32):
        gc = gc + jnp.where(row >= d, pltpu.roll(gc, d, axis=0), 0.0)
    lane = lax.broadcasted_iota(jnp.int32, ba.shape, 1)
    out = jnp.where(lane < GDN_HEADS, beta, gc)
    gcol_ref[...] = out
    grow_ref[...] = out.T[: 2 * GDN_HEADS]


def _prenorm_gate(x, meta_block, gain, wba, alog, dtb):
    tm = META_BLOCK
    nx = x.shape[0] // tm
    rows = x.shape[0] + tm
    return pl.pallas_call(
        _prenorm_gate_kernel,
        grid=(nx + 1,),
        in_specs=[
            pl.BlockSpec((tm, D_MODEL), lambda i: (jnp.minimum(i, nx - 1), 0)),
            pl.BlockSpec((tm, D_MODEL), lambda i: (0, 0)),
            pl.BlockSpec((1, D_MODEL), lambda i: (0, 0)),
            pl.BlockSpec((GATE_LANES, D_MODEL), lambda i: (0, 0)),
            pl.BlockSpec((1, GATE_LANES), lambda i: (0, 0)),
            pl.BlockSpec((1, GATE_LANES), lambda i: (0, 0)),
        ],
        out_specs=[
            pl.BlockSpec((tm, D_MODEL), lambda i: (i, 0)),
            pl.BlockSpec((tm, GATE_LANES), lambda i: (i, 0)),
            pl.BlockSpec((2 * GDN_HEADS, tm), lambda i: (0, i)),
        ],
        out_shape=[
            jax.ShapeDtypeStruct((rows, D_MODEL), BF16),
            jax.ShapeDtypeStruct((rows, GATE_LANES), F32),
            jax.ShapeDtypeStruct((2 * GDN_HEADS, rows), F32),
        ],
        compiler_params=_params(("parallel",)),
        name="prenorm_gate",
    )(x, meta_block, gain, wba, alog, dtb)


IN_PROJ_TN = 1024
GDN_COL_TILES = 4 * GDN_WIDTH // IN_PROJ_TN


def _causal_conv_silu(x, prev, w):
    r8 = lax.broadcasted_iota(jnp.int32, prev.shape, 0)
    y = x * w[CONV_WIDTH - 1:CONV_WIDTH]
    for d in range(1, CONV_WIDTH):
        shifted = pltpu.roll(x, d, axis=0)
        top = jnp.where(r8 < d, pltpu.roll(prev, d, axis=0), shifted[:8])
        shifted = jnp.concatenate([top, shifted[8:]], axis=0)
        y = y + shifted * w[CONV_WIDTH - 1 - d:CONV_WIDTH - d]
    return _silu(y)


def _in_proj_kernel(a_ref, wg_ref, wd_ref, cw_ref, o_ref, tail_ref, raw_ref, w16_ref):
    j = pl.program_id(0)
    i = pl.program_id(1)
    tm = a_ref.shape[0]

    @pl.when(i == 0)
    def _():
        tail_ref[...] = jnp.zeros_like(tail_ref)

    @pl.when((i == 0) & (j < GDN_COL_TILES))
    def _():
        w16_ref[...] = wg_ref[...].T.astype(BF16)

    @pl.when((i == 0) & (j >= GDN_COL_TILES))
    def _():
        w16_ref[...] = wd_ref[...].T.astype(BF16)

    def gdn_qkv(l2_scale):
        a = a_ref[...]
        for pair in range(GDN_HEADS // 2):
            cs = slice(pair * 2 * GDN_DIM, (pair + 1) * 2 * GDN_DIM)
            raw_ref[pair] = _dot(a, w16_ref[:, cs])
        for pair in range(GDN_HEADS // 2):
            cs = slice(pair * 2 * GDN_DIM, (pair + 1) * 2 * GDN_DIM)
            raw = raw_ref[pair]
            y = _causal_conv_silu(raw, tail_ref[:, cs], cw_ref[:, cs])
            tail_ref[:, cs] = raw[tm - 8:]
            for half in range(2):
                ls = slice(half * GDN_DIM, (half + 1) * GDN_DIM)
                yh = y[:, ls]
                if l2_scale is not None:
                    yh = yh * (lax.rsqrt(jnp.sum(yh * yh, axis=-1, keepdims=True) + NORM_EPS) * l2_scale)
                o_ref[:, pair * 2 * GDN_DIM + half * GDN_DIM:pair * 2 * GDN_DIM + (half + 1) * GDN_DIM] = yh

    pl.when(j == 0)(functools.partial(gdn_qkv, GDN_DIM ** -0.5))
    pl.when(j == 1)(functools.partial(gdn_qkv, 1.0))
    pl.when(j == 2)(functools.partial(gdn_qkv, None))

    @pl.when(j >= GDN_COL_TILES - 1)
    def _():
        o_ref[...] = _dot(a_ref[...], w16_ref[...])


def _in_proj(n1, wt_all, wt_diff, conv_w):
    m = n1.shape[0]
    tm, tn = META_BLOCK, IN_PROJ_TN
    nm = m // tm
    n = GDN_COL_TILES * tn + wt_diff.shape[0]
    seq_order = lambda i: (i + nm - 1) % nm
    return pl.pallas_call(
        _in_proj_kernel,
        grid=(n // tn, nm),
        in_specs=[
            pl.BlockSpec((tm, D_MODEL), lambda j, i: (seq_order(i), 0)),
            pl.BlockSpec((tn, D_MODEL), lambda j, i: (jnp.minimum(j, GDN_COL_TILES - 1), 0)),
            pl.BlockSpec((tn, D_MODEL), lambda j, i: (jnp.maximum(j - GDN_COL_TILES, 0), 0)),
            pl.BlockSpec((CONV_WIDTH, tn), lambda j, i: (0, jnp.minimum(j, 2))),
        ],
        out_specs=pl.BlockSpec((tm, tn), lambda j, i: (seq_order(i), j)),
        out_shape=jax.ShapeDtypeStruct((m, n), F32),
        scratch_shapes=[pltpu.VMEM((8, tn), F32), pltpu.VMEM((GDN_HEADS // 2, tm, 2 * GDN_DIM), F32),
                        pltpu.VMEM((D_MODEL, tn), BF16)],
        compiler_params=_params(("parallel", "arbitrary")),
        name="in_proj",
    )(n1, wt_all, wt_diff, conv_w)


GDN_LOCAL_CHUNKS = 4


def _gdn_local_kernel(q_ref, k_ref, v_ref, gcol_ref, grow_ref,
                      w_ref, u_ref, qg_ref, kd_ref, attn_ref, egl_ref):
    q_all = q_ref[...]
    k_all = k_ref[...]
    v_all = v_ref[...]
    gcol = gcol_ref[...]
    grow = grow_ref[...]
    ii = lax.broadcasted_iota(jnp.int32, (CHUNK, CHUNK), 0)
    jj = lax.broadcasted_iota(jnp.int32, (CHUNK, CHUNK), 1)
    units = [(c, h) for c in range(GDN_LOCAL_CHUNKS) for h in range(GDN_HEADS)]
    rows_of = lambda c: slice(c * CHUNK, (c + 1) * CHUNK)
    lanes_of = lambda h: slice(h * GDN_DIM, (h + 1) * GDN_DIM)
    qs, ks, kbs, vbs, egcs, gcs, kqs = [], [], [], [], [], [], []
    for c, h in units:
        rs, ls = rows_of(c), lanes_of(h)
        q = q_all[rs, ls]
        k = k_all[rs, ls]
        beta_c = gcol[rs, h:h + 1]
        gc_c = gcol[rs, GDN_HEADS + h:GDN_HEADS + h + 1]
        kb = k * beta_c
        qs.append(q)
        ks.append(k)
        kbs.append(kb)
        vbs.append(v_all[rs, ls] * beta_c)
        gcs.append(gc_c)
        egcs.append(jnp.exp(gc_c))
        kqs.append(_dot_nt(jnp.concatenate([kb.astype(BF16), q.astype(BF16)], axis=0), k.astype(BF16)))
    lms, attns = [], []
    for (c, h), kq, gc_c in zip(units, kqs, gcs):
        gc_r = grow[GDN_HEADS + h:GDN_HEADS + h + 1, rows_of(c)]
        decay = jnp.exp(jnp.where(ii >= jj, gc_c - gc_r, MASK_VALUE))
        lms.append(jnp.where(ii > jj, kq[:CHUNK] * decay, 0.0))
        attns.append(kq[CHUNK:] * decay)
    xor = ii ^ jj
    eye = jnp.where(ii == jj, 1.0, 0.0)
    xs = [eye - jnp.where(xor == 1, lm, 0.0) for lm in lms]
    level = 1
    while (2 << level) <= CHUNK:
        sel = (xor >> level) == 1
        ys = [_dot(jnp.where(sel, lm, 0.0).astype(BF16), x.astype(BF16)) for lm, x in zip(lms, xs)]
        xs = [x - _dot(x.astype(BF16), y.astype(BF16)) for x, y in zip(xs, ys)]
        level += 1
    uws = [_dot(x.astype(BF16), jnp.concatenate([vb.astype(BF16), (kb * egc).astype(BF16)], axis=1))
           for x, vb, kb, egc in zip(xs, vbs, kbs, egcs)]
    for (c, h), uw, q, k, egc, gc_c, attn in zip(units, uws, qs, ks, egcs, gcs, attns):
        rs, ls = rows_of(c), lanes_of(h)
        u_ref[rs, ls] = uw[:, :GDN_DIM]
        w_ref[rs, ls] = uw[:, GDN_DIM:].astype(BF16)
        qg_ref[rs, ls] = (q * egc).astype(BF16)
        gc_last = gc_c[CHUNK - 1:CHUNK]
        kd_ref[rs, ls] = (k * jnp.exp(gc_last - gc_c)).astype(BF16)
        attn_ref[h, rs, :] = attn.astype(BF16)
        egl_ref[c, h:h + 1, :] = jnp.broadcast_to(jnp.exp(gc_last), (1, GDN_DIM))


def _gdn_local(proj, gcol, grow):
    rows = proj.shape[0]
    rb = GDN_LOCAL_CHUNKS * CHUNK
    bidx = lambda i: i
    blk = lambda col: pl.BlockSpec((rb, GDN_WIDTH), lambda i, col=col: (bidx(i), col))
    row_out = lambda dt: jax.ShapeDtypeStruct((rows, GDN_WIDTH), dt)
    return pl.pallas_call(
        _gdn_local_kernel,
        grid=(rows // rb,),
        in_specs=[
            blk(0), blk(1), blk(2),
            pl.BlockSpec((rb, GATE_LANES), lambda i: (bidx(i), 0)),
            pl.BlockSpec((2 * GDN_HEADS, rb), lambda i: (0, bidx(i))),
        ],
        out_specs=[
            pl.BlockSpec((rb, GDN_WIDTH), lambda i: (bidx(i), 0)),
            pl.BlockSpec((rb, GDN_WIDTH), lambda i: (bidx(i), 0)),
            pl.BlockSpec((rb, GDN_WIDTH), lambda i: (bidx(i), 0)),
            pl.BlockSpec((rb, GDN_WIDTH), lambda i: (bidx(i), 0)),
            pl.BlockSpec((GDN_HEADS, rb, CHUNK), lambda i: (0, bidx(i), 0)),
            pl.BlockSpec((GDN_LOCAL_CHUNKS, GDN_HEADS, GDN_DIM), lambda i: (bidx(i), 0, 0)),
        ],
        out_shape=[
            row_out(BF16),
            row_out(F32),
            row_out(BF16),
            row_out(BF16),
            jax.ShapeDtypeStruct((GDN_HEADS, rows, CHUNK), BF16),
            jax.ShapeDtypeStruct((rows // CHUNK, GDN_HEADS, GDN_DIM), F32),
        ],
        compiler_params=_params(("parallel",)),
        name="gdn_local",
    )(proj, proj, proj, gcol, grow)


GDN_STATE_CHUNKS = 8


def _gdn_state_kernel(w_ref, u_ref, qg_ref, kd_ref, attn_ref, egl_ref, z_ref, gain_ref, o_ref, s_ref):
    @pl.when(pl.program_id(0) == 0)
    def _():
        s_ref[...] = jnp.zeros_like(s_ref)

    gain = gain_ref[...]
    heads = range(GDN_HEADS)
    lanes = [slice(h * GDN_DIM, (h + 1) * GDN_DIM) for h in heads]
    ss = [s_ref[h] for h in heads]
    for ck in range(GDN_STATE_CHUNKS):
        rw = slice(ck * CHUNK, (ck + 1) * CHUNK)
        rs = [_dot(jnp.concatenate([w_ref[rw, ls], qg_ref[rw, ls]], axis=0), s.astype(BF16))
              for ls, s in zip(lanes, ss)]
        vns = [(u_ref[rw, ls] - r[:CHUNK]).astype(BF16) for ls, r in zip(lanes, rs)]
        os_ = [r[CHUNK:] + _dot(attn_ref[h, rw, :], vn) for h, r, vn in zip(heads, rs, vns)]
        ss = [s * egl_ref[ck, h:h + 1, :] + _dot_tn(kd_ref[rw, ls], vn)
              for h, ls, s, vn in zip(heads, lanes, ss, vns)]
        for ls, o in zip(lanes, os_):
            on = o * lax.rsqrt(jnp.mean(o * o, axis=-1, keepdims=True) + NORM_EPS) * gain
            o_ref[rw, ls] = (on * _silu(z_ref[rw, ls])).astype(BF16)
    for h, s in zip(heads, ss):
        s_ref[h] = s


def _gdn_state(w, u, qg, kd, attn, egl, proj, gain, seq):
    rows = w.shape[0]
    rb = GDN_STATE_CHUNKS * CHUNK
    nblocks = rows // rb
    phys = lambda c: (c + nblocks - 1) % nblocks
    rowblk = lambda col: pl.BlockSpec((rb, GDN_WIDTH), lambda c, col=col: (phys(c), col))
    return pl.pallas_call(
        _gdn_state_kernel,
        grid=(seq // rb + 1,),
        in_specs=[
            rowblk(0), rowblk(0), rowblk(0), rowblk(0),
            pl.BlockSpec((GDN_HEADS, rb, CHUNK), lambda c: (0, phys(c), 0)),
            pl.BlockSpec((GDN_STATE_CHUNKS, GDN_HEADS, GDN_DIM), lambda c: (phys(c), 0, 0)),
            rowblk(3),
            pl.BlockSpec((1, GDN_DIM), lambda c: (0, 0)),
        ],
        out_specs=pl.BlockSpec((rb, GDN_WIDTH), lambda c: (jnp.maximum(c - 1, 0), 0)),
        out_shape=jax.ShapeDtypeStruct((seq, GDN_WIDTH), BF16),
        scratch_shapes=[pltpu.VMEM((GDN_HEADS, GDN_DIM, GDN_DIM), F32)],
        compiler_params=_params(("arbitrary",)),
        name="gdn_state",
    )(w, u, qg, kd, attn, egl, proj, gain)


KV_TILE = 128
VT_ROWS = DIFF_VDIM + 16
Q_SCALE = DIFF_DIM ** -0.5 * math.log2(math.e)


PREP_ROWS = 2 * KV_TILE


def _attn_prep_kernel(q_ref, k_ref, v_ref, cos_ref, sin_ref, qg_ref, kg_ref, gsum_ref, eye_ref,
                      q2_ref, kr_ref, vt_ref):
    cos = cos_ref[...]
    sin = sin_ref[...]
    gsum = gsum_ref[...]
    eye = eye_ref[...]
    lane = lax.broadcasted_iota(jnp.int32, cos.shape, 1)
    first_half = (lane % DIFF_DIM) < (DIFF_DIM // 2)
    low_map = lane < DIFF_DIM

    def norm_rope(x, gain):
        ms = _dot((x * x).astype(BF16), gsum) * (1.0 / DIFF_DIM)
        xn = x * lax.rsqrt(ms + NORM_EPS) * gain
        rot = jnp.where(first_half, pltpu.roll(xn, LANES - DIFF_DIM // 2, axis=1),
                        pltpu.roll(xn, DIFF_DIM // 2, axis=1))
        return xn * cos + rot * sin

    for h in range(DIFF_HEADS):
        ls = slice(h * DIFF_VDIM, (h + 1) * DIFF_VDIM)
        q = norm_rope(q_ref[:, ls], qg_ref[...]) * Q_SCALE
        q2_ref[0, :, ls] = jnp.where(low_map, q, 0.0).astype(BF16)
        q2_ref[1, :, ls] = jnp.where(low_map, 0.0, q).astype(BF16)
        kr_ref[:, ls] = norm_rope(k_ref[:, ls], kg_ref[...]).astype(BF16)
        for t in range(PREP_ROWS // KV_TILE):
            v = v_ref[t * KV_TILE:(t + 1) * KV_TILE, ls].astype(BF16)
            vt_ref[t, h * VT_ROWS:h * VT_ROWS + DIFF_VDIM, :] = _dot_nt(eye, v).astype(BF16)
            vt_ref[t, h * VT_ROWS + DIFF_VDIM:(h + 1) * VT_ROWS, :] = jnp.ones(
                (VT_ROWS - DIFF_VDIM, KV_TILE), BF16)


def _attn_prep(proj, cos, sin, qgain, kgain, gsum, eye):
    rows = proj.shape[0]
    tm = PREP_ROWS
    bidx = lambda i: i
    col = lambda c: pl.BlockSpec((tm, DIFF_WIDTH), lambda i, c=c: (bidx(i), c))
    small = lambda shape: pl.BlockSpec(shape, lambda i: (0,) * len(shape))
    return pl.pallas_call(
        _attn_prep_kernel,
        grid=(rows // tm,),
        in_specs=[
            col(4), col(5), col(6),
            pl.BlockSpec((tm, LANES), lambda i: (bidx(i), 0)),
            pl.BlockSpec((tm, LANES), lambda i: (bidx(i), 0)),
            small((1, LANES)), small((1, LANES)), small((LANES, LANES)), small((LANES, LANES)),
        ],
        out_specs=[
            pl.BlockSpec((2, tm, DIFF_WIDTH), lambda i: (0, bidx(i), 0)),
            pl.BlockSpec((tm, DIFF_WIDTH), lambda i: (bidx(i), 0)),
            pl.BlockSpec((tm // KV_TILE, DIFF_HEADS * VT_ROWS, KV_TILE), lambda i: (bidx(i), 0, 0)),
        ],
        out_shape=[
            jax.ShapeDtypeStruct((2, rows, DIFF_WIDTH), BF16),
            jax.ShapeDtypeStruct((rows, DIFF_WIDTH), BF16),
            jax.ShapeDtypeStruct((rows // KV_TILE, DIFF_HEADS * VT_ROWS, KV_TILE), BF16),
        ],
        compiler_params=_params(("parallel",)),
        name="attn_prep",
    )(proj, proj, proj, cos, sin, qgain, kgain, gsum, eye)


ATTN_BLOCK = 1024
ATTN_QSUB = 256

def _diff_attn_kernel(q_ref, k_ref, vt_ref, lam_ref, gain_ref, o_ref, acc_ref, st_ref):
    i = pl.program_id(1)
    rows = k_ref.shape[0]
    bk = ATTN_QSUB
    nsub = q_ref.shape[1] // ATTN_QSUB
    chains = [(mp, sb) for mp in range(2) for sb in range(nsub)]
    every = list(range(len(chains)))
    qs = [q_ref[mp, sb * ATTN_QSUB:(sb + 1) * ATTN_QSUB, :] for mp, sb in chains]
    kv_tiles = bk // KV_TILE

    def update(sts, vt, ms, which):
        ms = list(ms)
        first = ms[which[0]] is None
        cms = [jnp.max(st, axis=0, keepdims=True) for st in sts]
        m_new = cms if first else [jnp.maximum(ms[c], cm) for c, cm in zip(which, cms)]
        ps = [jnp.exp2(st - mn).astype(BF16) for st, mn in zip(sts, m_new)]
        pvs = [_dot(vt, p) for p in ps]
        for n, c in enumerate(which):
            if first:
                acc_ref[c] = pvs[n]
            else:
                acc_ref[c] = jnp.exp2(ms[c] - m_new[n]) * acc_ref[c] + pvs[n]
            ms[c] = m_new[n]
        return ms

    def store_scores(j, slot, which):
        start = j * bk if isinstance(j, int) else pl.multiple_of(j * bk, bk)
        k_c = k_ref[pl.ds(start, bk), :]
        for c in which:
            st_ref[slot, c] = _dot_nt(k_c, qs[c])

    def values_t(j):
        return jnp.concatenate([vt_ref[j * kv_tiles + t] for t in range(kv_tiles)], axis=1)

    k_meta = k_ref[rows - N_META:rows, :]
    sts = [_dot_nt(k_meta, q) for q in qs]
    store_scores(0, 0, every)
    vt_meta = vt_ref[rows // KV_TILE - 1][:, KV_TILE - N_META:]
    ms = update(sts, vt_meta, [None] * len(chains), every)

    def full_blocks(t, ms):
        for n in range(nsub):
            j = nsub * t + n
            store_scores(j + 1, (n + 1) % 2, every)
            ms = update([st_ref[n % 2, c] for c in every], values_t(j), ms, every)
        return tuple(ms)

    ms = lax.fori_loop(0, i, full_blocks, tuple(ms))
    tri = (lax.broadcasted_iota(jnp.int32, (bk, ATTN_QSUB), 0)
           <= lax.broadcasted_iota(jnp.int32, (bk, ATTN_QSUB), 1))
    for d in range(nsub):
        if d + 1 < nsub:
            store_scores(i * nsub + d + 1, (d + 1) % 2, [c for c in every if chains[c][1] > d])
        which = [c for c in every if chains[c][1] >= d]
        sts = [jnp.where(tri, st_ref[d % 2, c], MASK_VALUE) if chains[c][1] == d else st_ref[d % 2, c]
               for c in which]
        ms = update(sts, values_t(i * nsub + d), ms, which)

    lp = lam_ref[...]
    lam = (jnp.exp(jnp.sum(lp[0:1] * lp[1:2], axis=-1, keepdims=True))
           - jnp.exp(jnp.sum(lp[2:3] * lp[3:4], axis=-1, keepdims=True)) + LAMBDA_INIT)
    gain = gain_ref[...]
    for sb in range(nsub):
        num1, num2 = acc_ref[sb, :DIFF_VDIM, :], acc_ref[nsub + sb, :DIFF_VDIM, :]
        l1 = acc_ref[sb, DIFF_VDIM:DIFF_VDIM + 1, :]
        l2 = acc_ref[nsub + sb, DIFF_VDIM:DIFF_VDIM + 1, :]
        ot = num1 * (1.0 / l1) - num2 * (lam / l2)
        ot = ot * lax.rsqrt(jnp.mean(ot * ot, axis=0, keepdims=True) + NORM_EPS) * gain
        o_ref[sb * ATTN_QSUB:(sb + 1) * ATTN_QSUB, :] = (ot * (1.0 - LAMBDA_INIT)).T.astype(BF16)


def _diff_attn(q2, kr, vt, lam_params, gain_col, seq):
    rows = kr.shape[0]
    bq = ATTN_BLOCK
    nchains = 2 * (bq // ATTN_QSUB)
    assert seq % bq == 0 and bq % (2 * ATTN_QSUB) == 0
    return pl.pallas_call(
        _diff_attn_kernel,
        grid=(DIFF_HEADS, seq // bq),
        in_specs=[
            pl.BlockSpec((2, bq, DIFF_VDIM), lambda h, i: (0, i, h)),
            pl.BlockSpec((rows, DIFF_VDIM), lambda h, i: (0, h)),
            pl.BlockSpec((rows // KV_TILE, VT_ROWS, KV_TILE), lambda h, i: (0, h, 0)),
            pl.BlockSpec((4, DIFF_DIM), lambda h, i: (0, 0)),
            pl.BlockSpec((DIFF_VDIM, 1), lambda h, i: (0, 0)),
        ],
        out_specs=pl.BlockSpec((bq, DIFF_VDIM), lambda h, i: (i, h)),
        out_shape=jax.ShapeDtypeStruct((seq, DIFF_WIDTH), BF16),
        scratch_shapes=[pltpu.VMEM((nchains, VT_ROWS, ATTN_QSUB), F32),
                        pltpu.VMEM((2, nchains, ATTN_QSUB, ATTN_QSUB), F32)],
        compiler_params=_params(("parallel", "arbitrary")),
        name="diff_attn",
    )(q2, kr, vt, lam_params, gain_col)


def _out_proj_kernel(mg_ref, md_ref, wg_ref, wd_ref, h_ref, gain_ref, h2_ref, n2_ref):
    h2 = (h_ref[...] + _dot(mg_ref[...], wg_ref[...].astype(BF16))
          + _dot(md_ref[...], wd_ref[...].astype(BF16)))
    h2_ref[...] = h2
    ms = jnp.mean(h2 * h2, axis=-1, keepdims=True)
    n2_ref[...] = (h2 * lax.rsqrt(ms + NORM_EPS) * gain_ref[...]).astype(BF16)


def _out_proj(mix_g, mix_d, w_out, h, gain, seq):
    tm = _pick(seq, (512, 128))
    return pl.pallas_call(
        _out_proj_kernel,
        grid=(seq // tm,),
        in_specs=[
            pl.BlockSpec((tm, GDN_WIDTH), lambda i: (i, 0)),
            pl.BlockSpec((tm, DIFF_WIDTH), lambda i: (i, 0)),
            pl.BlockSpec((GDN_WIDTH, D_MODEL), lambda i: (0, 0), pipeline_mode=pl.Buffered(1)),
            pl.BlockSpec((DIFF_WIDTH, D_MODEL), lambda i: (1, 0), pipeline_mode=pl.Buffered(1)),
            pl.BlockSpec((tm, D_MODEL), lambda i: (i, 0)),
            pl.BlockSpec((1, D_MODEL), lambda i: (0, 0)),
        ],
        out_specs=[
            pl.BlockSpec((tm, D_MODEL), lambda i: (i, 0)),
            pl.BlockSpec((tm, D_MODEL), lambda i: (i, 0)),
        ],
        out_shape=[
            jax.ShapeDtypeStruct((seq, D_MODEL), F32),
            jax.ShapeDtypeStruct((seq, D_MODEL), BF16),
        ],
        compiler_params=_params(("parallel",)),
        name="out_proj",
    )(mix_g, mix_d, w_out, w_out, h, gain)


def _gate_up_kernel(n_ref, wg_ref, wu_ref, a_ref):
    n = n_ref[...]
    g = _dot(n, wg_ref[...].astype(BF16))
    u = _dot(n, wu_ref[...].astype(BF16))
    a_ref[...] = (_silu(g) * u).astype(BF16)


def _gate_up(n2, w_gu):
    seq = n2.shape[0]
    tm = _pick(seq, (2048, 1024, 128))
    tn = 512
    nt = D_FF // tn
    return pl.pallas_call(
        _gate_up_kernel,
        grid=(nt, seq // tm),
        in_specs=[
            pl.BlockSpec((tm, D_MODEL), lambda j, i: (i, 0)),
            pl.BlockSpec((D_MODEL, tn), lambda j, i: (0, j)),
            pl.BlockSpec((D_MODEL, tn), lambda j, i: (0, j + nt)),
        ],
        out_specs=pl.BlockSpec((tm, tn), lambda j, i: (i, j)),
        out_shape=jax.ShapeDtypeStruct((seq, D_FF), BF16),
        compiler_params=_params(("parallel", "parallel")),
        name="ffn_gate_up",
    )(n2, w_gu, w_gu)


def _down_kernel(a_ref, w_ref, h_ref, o_ref):
    o_ref[...] = h_ref[...] + _dot(a_ref[...], w_ref[...].astype(BF16))


def _down(act, w_down, h2):
    seq = act.shape[0]
    tm = _pick(seq, (512, 128))
    tn = 1024
    return pl.pallas_call(
        _down_kernel,
        grid=(D_MODEL // tn, seq // tm),
        in_specs=[
            pl.BlockSpec((tm, D_FF), lambda j, i: (i, 0)),
            pl.BlockSpec((D_FF, tn), lambda j, i: (0, j), pipeline_mode=pl.Buffered(1)),
            pl.BlockSpec((tm, tn), lambda j, i: (i, j)),
        ],
        out_specs=pl.BlockSpec((tm, tn), lambda j, i: (i, j)),
        out_shape=jax.ShapeDtypeStruct((seq, D_MODEL), F32),
        compiler_params=_params(("parallel", "parallel")),
        name="ffn_down",
    )(act, w_down, h2)


def _rope_tables(seq):
    half = DIFF_DIM // 2
    pos = jnp.concatenate([jnp.arange(seq) + N_META, jnp.zeros((META_BLOCK - N_META,), jnp.int32),
                           jnp.arange(N_META)]).astype(F32)
    inv_freq = ROPE_THETA ** (-jnp.arange(half, dtype=F32) / half)
    ang = pos[:, None] * inv_freq[None, :]
    cos = jnp.tile(jnp.cos(ang), (1, LANES // half))
    sin = jnp.sin(ang)
    sin = jnp.tile(jnp.concatenate([-sin, sin], axis=1), (1, LANES // DIFF_DIM))
    return cos, sin


def _lane_pad(v, offset):
    return jnp.zeros((1, GATE_LANES), F32).at[0, offset:offset + v.shape[0]].set(v.astype(F32))


def kernel(x, meta_tokens, attn_norm, w_in, conv_w, a_log, dt_bias, gdn_norm, q_norm, k_norm,
           lambda_q1, lambda_k1, lambda_q2, lambda_k2, diff_norm, w_out, ffn_norm, w_gate_up, w_down):
    assert x.shape[0] == 1 and x.shape[2] == D_MODEL
    seq = x.shape[1]
    assert seq % META_BLOCK == 0
    xs = x[0]
    meta_block = jnp.concatenate([jnp.zeros((META_BLOCK - N_META, D_MODEL), xs.dtype),
                                  meta_tokens.astype(xs.dtype)], axis=0)

    gdn_cols = 4 * GDN_WIDTH
    wt_in = w_in[0].T
    wt_diff = wt_in[gdn_cols + 2 * GDN_HEADS:]
    wt_ba = jnp.pad(wt_in[gdn_cols:gdn_cols + 2 * GDN_HEADS],
                    ((0, GATE_LANES - 2 * GDN_HEADS), (0, 0))).astype(BF16)

    n1, gcol, grow = _prenorm_gate(xs, meta_block, attn_norm, wt_ba, _lane_pad(a_log[0], GDN_HEADS),
                                   _lane_pad(dt_bias[0], GDN_HEADS))
    proj = _in_proj(n1, wt_in, wt_diff, conv_w[0])

    w, u, qg, kd, attn, egl = _gdn_local(proj, gcol, grow)
    mix_g = _gdn_state(w, u, qg, kd, attn, egl, proj, gdn_norm, seq)

    cos, sin = _rope_tables(seq)
    tile2 = lambda g: jnp.tile(g.astype(F32), (1, LANES // DIFF_DIM))
    lane = np.arange(LANES)
    gsum = jnp.asarray((lane[:, None] // DIFF_DIM) == (lane[None, :] // DIFF_DIM), BF16)
    eye = jnp.asarray(lane[:, None] == lane[None, :], BF16)
    q2, kr, vt = _attn_prep(proj, cos, sin, tile2(q_norm), tile2(k_norm), gsum, eye)
    lam_params = jnp.concatenate([lambda_q1, lambda_k1, lambda_q2, lambda_k2], axis=0).astype(F32)
    mix_d = _diff_attn(q2, kr, vt, lam_params, diff_norm.astype(F32).reshape(DIFF_VDIM, 1), seq)

    h2, n2 = _out_proj(mix_g, mix_d, w_out[0], xs, ffn_norm, seq)
    act = _gate_up(n2, w_gate_up[0])
    out = _down(act, w_down[0], h2)
    return out[None]
```

```python
import functools
import math

import jax
import jax.numpy as jnp
import numpy as np
from jax import lax
from jax.experimental import pallas as pl
from jax.experimental.pallas import tpu as pltpu

F32 = jnp.float32
BF16 = jnp.bfloat16

D_MODEL = 2048
N_META = 16
GDN_HEADS = 8
GDN_DIM = 128
GDN_WIDTH = GDN_HEADS * GDN_DIM
CONV_WIDTH = 4
CHUNK = 64
DIFF_HEADS = 8
DIFF_DIM = 64
DIFF_VDIM = 2 * DIFF_DIM
DIFF_WIDTH = DIFF_HEADS * DIFF_VDIM
ROPE_THETA = 10000.0
D_FF = 5632
NORM_EPS = 1e-6
MASK_VALUE = -1e30
LAMBDA_INIT = 0.8 - 0.6 * math.exp(-0.3 * 0)

LANES = 128
META_BLOCK = 512
GATE_LANES = 128
VMEM_LIMIT = 56 * 1024 * 1024


def _pick(n, candidates):
    for c in candidates:
        if n % c == 0:
            return c
    raise ValueError(f"no tile in {candidates} divides {n}")


def _params(sem, vmem=VMEM_LIMIT):
    return pltpu.CompilerParams(dimension_semantics=sem, vmem_limit_bytes=vmem)


def _dot(a, b):
    return jnp.dot(a, b, preferred_element_type=F32)


def _dot_nt(a, b):
    return lax.dot_general(a, b, (((1,), (1,)), ((), ())), preferred_element_type=F32)


def _dot_tn(a, b):
    return lax.dot_general(a, b, (((0,), (0,)), ((), ())), preferred_element_type=F32)


def _softplus(x):
    return jnp.maximum(x, 0.0) + jnp.log1p(jnp.exp(-jnp.abs(x)))


def _silu(x):
    return x * jax.nn.sigmoid(x)


def _prenorm_gate_kernel(x_ref, mb_ref, gain_ref, wba_ref, alog_ref, dtb_ref, n_ref, gcol_ref, grow_ref):
    h = jnp.where(pl.program_id(0) < pl.num_programs(0) - 1, x_ref[...], mb_ref[...])
    ms = jnp.mean(h * h, axis=-1, keepdims=True)
    n = (h * lax.rsqrt(ms + NORM_EPS) * gain_ref[...]).astype(BF16)
    n_ref[...] = n
    ba = _dot_nt(n, wba_ref[...])
    beta = jax.nn.sigmoid(ba)
    g = -jnp.exp(alog_ref[...]) * _softplus(ba + dtb_ref[...])
    row = lax.broadcasted_iota(jnp.int32, ba.shape, 0) % CHUNK
    gc = g
    for d in (1, 2, 4, 8, 16, 32):
        gc = gc + jnp.where(row >= d, pltpu.roll(gc, d, axis=0), 0.0)
    lane = lax.broadcasted_iota(jnp.int32, ba.shape, 1)
    out = jnp.where(lane < GDN_HEADS, beta, gc)
    gcol_ref[...] = out
    grow_ref[...] = out.T[: 2 * GDN_HEADS]


def _prenorm_gate(x, meta_block, gain, wba, alog, dtb):
    tm = META_BLOCK
    nx = x.shape[0] // tm
    rows = x.shape[0] + tm
    return pl.pallas_call(
        _prenorm_gate_kernel,
        grid=(nx + 1,),
        in_specs=[
            pl.BlockSpec((tm, D_MODEL), lambda i: (jnp.minimum(i, nx - 1), 0)),
            pl.BlockSpec((tm, D_MODEL), lambda i: (0, 0)),
            pl.BlockSpec((1, D_MODEL), lambda i: (0, 0)),
            pl.BlockSpec((GATE_LANES, D_MODEL), lambda i: (0, 0)),
            pl.BlockSpec((1, GATE_LANES), lambda i: (0, 0)),
            pl.BlockSpec((1, GATE_LANES), lambda i: (0, 0)),
        ],
        out_specs=[
            pl.BlockSpec((tm, D_MODEL), lambda i: (i, 0)),
            pl.BlockSpec((tm, GATE_LANES), lambda i: (i, 0)),
            pl.BlockSpec((2 * GDN_HEADS, tm), lambda i: (0, i)),
        ],
        out_shape=[
            jax.ShapeDtypeStruct((rows, D_MODEL), BF16),
            jax.ShapeDtypeStruct((rows, GATE_LANES), F32),
            jax.ShapeDtypeStruct((2 * GDN_HEADS, rows), F32),
        ],
        compiler_params=_params(("parallel",)),
        name="prenorm_gate",
    )(x, meta_block, gain, wba, alog, dtb)


IN_PROJ_TN = 1024
GDN_COL_TILES = 4 * GDN_WIDTH // IN_PROJ_TN


def _causal_conv_silu(x, prev, w):
    r8 = lax.broadcasted_iota(jnp.int32, prev.shape, 0)
    y = x * w[CONV_WIDTH - 1:CONV_WIDTH]
    for d in range(1, CONV_WIDTH):
        shifted = pltpu.roll(x, d, axis=0)
        top = jnp.where(r8 < d, pltpu.roll(prev, d, axis=0), shifted[:8])
        shifted = jnp.concatenate([top, shifted[8:]], axis=0)
        y = y + shifted * w[CONV_WIDTH - 1 - d:CONV_WIDTH - d]
    return _silu(y)


def _in_proj_kernel(a_ref, wg_ref, wd_ref, cw_ref, o_ref, tail_ref, raw_ref, w16_ref):
    j = pl.program_id(0)
    i = pl.program_id(1)
    tm = a_ref.shape[0]

    @pl.when(i == 0)
    def _():
        tail_ref[...] = jnp.zeros_like(tail_ref)

    @pl.when((i == 0) & (j < GDN_COL_TILES))
    def _():
        w16_ref[...] = wg_ref[...].T.astype(BF16)

    @pl.when((i == 0) & (j >= GDN_COL_TILES))
    def _():
        w16_ref[...] = wd_ref[...].T.astype(BF16)

    def gdn_qkv(l2_scale):
        a = a_ref[...]
        for pair in range(GDN_HEADS // 2):
            cs = slice(pair * 2 * GDN_DIM, (pair + 1) * 2 * GDN_DIM)
            raw_ref[pair] = _dot(a, w16_ref[:, cs])
        for pair in range(GDN_HEADS // 2):
            cs = slice(pair * 2 * GDN_DIM, (pair + 1) * 2 * GDN_DIM)
            raw = raw_ref[pair]
            y = _causal_conv_silu(raw, tail_ref[:, cs], cw_ref[:, cs])
            tail_ref[:, cs] = raw[tm - 8:]
            for half in range(2):
                ls = slice(half * GDN_DIM, (half + 1) * GDN_DIM)
                yh = y[:, ls]
                if l2_scale is not None:
                    yh = yh * (lax.rsqrt(jnp.sum(yh * yh, axis=-1, keepdims=True) + NORM_EPS) * l2_scale)
                o_ref[:, pair * 2 * GDN_DIM + half * GDN_DIM:pair * 2 * GDN_DIM + (half + 1) * GDN_DIM] = yh

    pl.when(j == 0)(functools.partial(gdn_qkv, GDN_DIM ** -0.5))
    pl.when(j == 1)(functools.partial(gdn_qkv, 1.0))
    pl.when(j == 2)(functools.partial(gdn_qkv, None))

    @pl.when(j >= GDN_COL_TILES - 1)
    def _():
        o_ref[...] = _dot(a_ref[...], w16_ref[...])


def _in_proj(n1, wt_all, wt_diff, conv_w):
    m = n1.shape[0]
    tm, tn = META_BLOCK, IN_PROJ_TN
    nm = m // tm
    n = GDN_COL_TILES * tn + wt_diff.shape[0]
    seq_order = lambda i: (i + nm - 1) % nm
    return pl.pallas_call(
        _in_proj_kernel,
        grid=(n // tn, nm),
        in_specs=[
            pl.BlockSpec((tm, D_MODEL), lambda j, i: (seq_order(i), 0)),
            pl.BlockSpec((tn, D_MODEL), lambda j, i: (jnp.minimum(j, GDN_COL_TILES - 1), 0)),
            pl.BlockSpec((tn, D_MODEL), lambda j, i: (jnp.maximum(j - GDN_COL_TILES, 0), 0)),
            pl.BlockSpec((CONV_WIDTH, tn), lambda j, i: (0, jnp.minimum(j, 2))),
        ],
        out_specs=pl.BlockSpec((tm, tn), lambda j, i: (seq_order(i), j)),
        out_shape=jax.ShapeDtypeStruct((m, n), F32),
        scratch_shapes=[pltpu.VMEM((8, tn), F32), pltpu.VMEM((GDN_HEADS // 2, tm, 2 * GDN_DIM), F32),
                        pltpu.VMEM((D_MODEL, tn), BF16)],
        compiler_params=_params(("parallel", "arbitrary")),
        name="in_proj",
    )(n1, wt_all, wt_diff, conv_w)


GDN_LOCAL_CHUNKS = 4


def _gdn_local_kernel(q_ref, k_ref, v_ref, gcol_ref, grow_ref,
                      w_ref, u_ref, qg_ref, kd_ref, attn_ref, egl_ref):
    q_all = q_ref[...]
    k_all = k_ref[...]
    v_all = v_ref[...]
    gcol = gcol_ref[...]
    grow = grow_ref[...]
    ii = lax.broadcasted_iota(jnp.int32, (CHUNK, CHUNK), 0)
    jj = lax.broadcasted_iota(jnp.int32, (CHUNK, CHUNK), 1)
    units = [(c, h) for c in range(GDN_LOCAL_CHUNKS) for h in range(GDN_HEADS)]
    rows_of = lambda c: slice(c * CHUNK, (c + 1) * CHUNK)
    lanes_of = lambda h: slice(h * GDN_DIM, (h + 1) * GDN_DIM)
    qs, ks, kbs, vbs, egcs, gcs, kqs = [], [], [], [], [], [], []
    for c, h in units:
        rs, ls = rows_of(c), lanes_of(h)
        q = q_all[rs, ls]
        k = k_all[rs, ls]
        beta_c = gcol[rs, h:h + 1]
        gc_c = gcol[rs, GDN_HEADS + h:GDN_HEADS + h + 1]
        kb = k * beta_c
        qs.append(q)
        ks.append(k)
        kbs.append(kb)
        vbs.append(v_all[rs, ls] * beta_c)
        gcs.append(gc_c)
        egcs.append(jnp.exp(gc_c))
        kqs.append(_dot_nt(jnp.concatenate([kb.astype(BF16), q.astype(BF16)], axis=0), k.astype(BF16)))
    lms, attns = [], []
    for (c, h), kq, gc_c in zip(units, kqs, gcs):
        gc_r = grow[GDN_HEADS + h:GDN_HEADS + h + 1, rows_of(c)]
        decay = jnp.exp(jnp.where(ii >= jj, gc_c - gc_r, MASK_VALUE))
        lms.append(jnp.where(ii > jj, kq[:CHUNK] * decay, 0.0))
        attns.append(kq[CHUNK:] * decay)
    xor = ii ^ jj
    eye = jnp.where(ii == jj, 1.0, 0.0)
    xs = [eye - jnp.where(xor == 1, lm, 0.0) for lm in lms]
    level = 1
    while (2 << level) <= CHUNK:
        sel = (xor >> level) == 1
        ys = [_dot(jnp.where(sel, lm, 0.0).astype(BF16), x.astype(BF16)) for lm, x in zip(lms, xs)]
        xs = [x - _dot(x.astype(BF16), y.astype(BF16)) for x, y in zip(xs, ys)]
        level += 1
    uws = [_dot(x.astype(BF16), jnp.concatenate([vb.astype(BF16), (kb * egc).astype(BF16)], axis=1))
           for x, vb, kb, egc in zip(xs, vbs, kbs, egcs)]
    for (c, h), uw, q, k, egc, gc_c, attn in zip(units, uws, qs, ks, egcs, gcs, attns):
        rs, ls = rows_of(c), lanes_of(h)
        u_ref[rs, ls] = uw[:, :GDN_DIM]
        w_ref[rs, ls] = uw[:, GDN_DIM:].astype(BF16)
        qg_ref[rs, ls] = (q * egc).astype(BF16)
        gc_last = gc_c[CHUNK - 1:CHUNK]
        kd_ref[rs, ls] = (k * jnp.exp(gc_last - gc_c)).astype(BF16)
        attn_ref[h, rs, :] = attn.astype(BF16)
        egl_ref[c, h:h + 1, :] = jnp.broadcast_to(jnp.exp(gc_last), (1, GDN_DIM))


def _gdn_local_specs(proj, gcol, grow):
    rows = proj.shape[0]
    rb = GDN_LOCAL_CHUNKS * CHUNK
    blk = lambda col: pl.BlockSpec((rb, GDN_WIDTH), lambda i, col=col: (i, col))
    row_out = lambda dt: jax.ShapeDtypeStruct((rows, GDN_WIDTH), dt)
    in_specs = [
        blk(0), blk(1), blk(2),
        pl.BlockSpec((rb, GATE_LANES), lambda i: (i, 0)),
        pl.BlockSpec((2 * GDN_HEADS, rb), lambda i: (0, i)),
    ]
    out_specs = [
        blk(0), blk(0), blk(0), blk(0),
        pl.BlockSpec((GDN_HEADS, rb, CHUNK), lambda i: (0, i, 0)),
        pl.BlockSpec((GDN_LOCAL_CHUNKS, GDN_HEADS, GDN_DIM), lambda i: (i, 0, 0)),
    ]
    out_shape = [
        row_out(BF16),
        row_out(F32),
        row_out(BF16),
        row_out(BF16),
        jax.ShapeDtypeStruct((GDN_HEADS, rows, CHUNK), BF16),
        jax.ShapeDtypeStruct((rows // CHUNK, GDN_HEADS, GDN_DIM), F32),
    ]
    return (proj, proj, proj, gcol, grow), in_specs, out_specs, out_shape


GDN_STATE_CHUNKS = 2


def _gdn_state_chunks(chunks, w_ref, u_ref, qg_ref, kd_ref, attn_ref, egl_ref, s_ref,
                      z_ref=None, gain_ref=None, o_ref=None):
    heads = range(GDN_HEADS)
    lanes = [slice(h * GDN_DIM, (h + 1) * GDN_DIM) for h in heads]
    ss = [s_ref[h] for h in heads]
    for ck in chunks:
        rw = slice(ck * CHUNK, (ck + 1) * CHUNK)
        rs = [_dot(jnp.concatenate([w_ref[rw, ls], qg_ref[rw, ls]], axis=0), s.astype(BF16))
              for ls, s in zip(lanes, ss)]
        vns = [(u_ref[rw, ls] - r[:CHUNK]).astype(BF16) for ls, r in zip(lanes, rs)]
        if o_ref is not None:
            os_ = [r[CHUNK:] + _dot(attn_ref[h, rw, :], vn) for h, r, vn in zip(heads, rs, vns)]
        ss = [s * egl_ref[ck, h:h + 1, :] + _dot_tn(kd_ref[rw, ls], vn)
              for h, ls, s, vn in zip(heads, lanes, ss, vns)]
        if o_ref is not None:
            for ls, o in zip(lanes, os_):
                on = o * lax.rsqrt(jnp.mean(o * o, axis=-1, keepdims=True) + NORM_EPS) * gain_ref[...]
                o_ref[rw, ls] = (on * _silu(z_ref[rw, ls])).astype(BF16)
    for h, s in zip(heads, ss):
        s_ref[h] = s


KV_TILE = 128
VT_ROWS = DIFF_VDIM + 16
Q_SCALE = DIFF_DIM ** -0.5 * math.log2(math.e)


PREP_ROWS = 2 * KV_TILE


def _attn_prep_kernel(q_ref, k_ref, v_ref, cos_ref, sin_ref, qg_ref, kg_ref, gsum_ref, eye_ref,
                      q2_ref, kr_ref, vt_ref):
    cos = cos_ref[...]
    sin = sin_ref[...]
    gsum = gsum_ref[...]
    eye = eye_ref[...]
    lane = lax.broadcasted_iota(jnp.int32, cos.shape, 1)
    first_half = (lane % DIFF_DIM) < (DIFF_DIM // 2)
    low_map = lane < DIFF_DIM

    def norm_rope(x, gain):
        ms = _dot((x * x).astype(BF16), gsum) * (1.0 / DIFF_DIM)
        xn = x * lax.rsqrt(ms + NORM_EPS) * gain
        rot = jnp.where(first_half, pltpu.roll(xn, LANES - DIFF_DIM // 2, axis=1),
                        pltpu.roll(xn, DIFF_DIM // 2, axis=1))
        return xn * cos + rot * sin

    for h in range(DIFF_HEADS):
        ls = slice(h * DIFF_VDIM, (h + 1) * DIFF_VDIM)
        q = norm_rope(q_ref[:, ls], qg_ref[...]) * Q_SCALE
        q2_ref[0, :, ls] = jnp.where(low_map, q, 0.0).astype(BF16)
        q2_ref[1, :, ls] = jnp.where(low_map, 0.0, q).astype(BF16)
        kr_ref[:, ls] = norm_rope(k_ref[:, ls], kg_ref[...]).astype(BF16)
        for t in range(PREP_ROWS // KV_TILE):
            v = v_ref[t * KV_TILE:(t + 1) * KV_TILE, ls].astype(BF16)
            vt_ref[t, h * VT_ROWS:h * VT_ROWS + DIFF_VDIM, :] = _dot_nt(eye, v).astype(BF16)
            vt_ref[t, h * VT_ROWS + DIFF_VDIM:(h + 1) * VT_ROWS, :] = jnp.ones(
                (VT_ROWS - DIFF_VDIM, KV_TILE), BF16)


def _attn_prep_specs(proj, cos, sin, qgain, kgain, gsum, eye):
    rows = proj.shape[0]
    tm = PREP_ROWS
    col = lambda c: pl.BlockSpec((tm, DIFF_WIDTH), lambda i, c=c: (i, c))
    small = lambda shape: pl.BlockSpec(shape, lambda i: (0,) * len(shape))
    in_specs = [
        col(4), col(5), col(6),
        pl.BlockSpec((tm, LANES), lambda i: (i, 0)),
        pl.BlockSpec((tm, LANES), lambda i: (i, 0)),
        small((1, LANES)), small((1, LANES)), small((LANES, LANES)), small((LANES, LANES)),
    ]
    out_specs = [
        pl.BlockSpec((2, tm, DIFF_WIDTH), lambda i: (0, i, 0)),
        pl.BlockSpec((tm, DIFF_WIDTH), lambda i: (i, 0)),
        pl.BlockSpec((tm // KV_TILE, DIFF_HEADS * VT_ROWS, KV_TILE), lambda i: (i, 0, 0)),
    ]
    out_shape = [
        jax.ShapeDtypeStruct((2, rows, DIFF_WIDTH), BF16),
        jax.ShapeDtypeStruct((rows, DIFF_WIDTH), BF16),
        jax.ShapeDtypeStruct((rows // KV_TILE, DIFF_HEADS * VT_ROWS, KV_TILE), BF16),
    ]
    return (proj, proj, proj, cos, sin, qgain, kgain, gsum, eye), in_specs, out_specs, out_shape


def _gdn_local_attn_prep(gdn_part, prep_part):
    g_ops, g_in, g_out, g_shape = gdn_part
    p_ops, p_in, p_out, p_shape = prep_part
    assert GDN_LOCAL_CHUNKS * CHUNK == PREP_ROWS
    rows = g_ops[0].shape[0]

    def body(*refs):
        g_i, rest = refs[:len(g_in)], refs[len(g_in):]
        p_i, rest = rest[:len(p_in)], rest[len(p_in):]
        g_o, p_o = rest[:len(g_out)], rest[len(g_out):]
        _attn_prep_kernel(*p_i, *p_o)
        _gdn_local_kernel(*g_i, *g_o)

    outs = pl.pallas_call(
        body,
        grid=(rows // PREP_ROWS,),
        in_specs=g_in + p_in,
        out_specs=g_out + p_out,
        out_shape=g_shape + p_shape,
        compiler_params=_params(("parallel",)),
        name="gdn_local_attn_prep",
    )(*g_ops, *p_ops)
    return outs[:len(g_out)], outs[len(g_out):]


ATTN_BLOCK = 1024
ATTN_QSUB = 256

def _diff_attn_kernel(q_ref, k_ref, vt_ref, lam_ref, gain_ref,
                      gw_ref, gu_ref, gqg_ref, gkd_ref, gattn_ref, gegl_ref, gz_ref,
                      mw_ref, mu_ref, mqg_ref, mkd_ref, mattn_ref, megl_ref, ggain_ref,
                      o_ref, og_ref, acc_ref, st_ref, s_ref):
    i = pl.program_id(1)
    step = pl.program_id(0) * pl.num_programs(1) + i

    @pl.when(step == 0)
    def _():
        s_ref[...] = jnp.zeros_like(s_ref)
        _gdn_state_chunks(range(1), mw_ref, mu_ref, mqg_ref, mkd_ref, mattn_ref, megl_ref, s_ref)
    rows = k_ref.shape[0]
    bk = ATTN_QSUB
    nsub = q_ref.shape[1] // ATTN_QSUB
    chains = [(mp, sb) for mp in range(2) for sb in range(nsub)]
    every = list(range(len(chains)))
    qs = [q_ref[mp, sb * ATTN_QSUB:(sb + 1) * ATTN_QSUB, :] for mp, sb in chains]
    kv_tiles = bk // KV_TILE

    def update(sts, vt, ms, which):
        ms = list(ms)
        first = ms[which[0]] is None
        cms = [jnp.max(st, axis=0, keepdims=True) for st in sts]
        m_new = cms if first else [jnp.maximum(ms[c], cm) for c, cm in zip(which, cms)]
        ps = [jnp.exp2(st - mn).astype(BF16) for st, mn in zip(sts, m_new)]
        pvs = [_dot(vt, p) for p in ps]
        for n, c in enumerate(which):
            if first:
                acc_ref[c] = pvs[n]
            else:
                acc_ref[c] = jnp.exp2(ms[c] - m_new[n]) * acc_ref[c] + pvs[n]
            ms[c] = m_new[n]
        return ms

    def store_scores(j, slot, which):
        start = j * bk if isinstance(j, int) else pl.multiple_of(j * bk, bk)
        k_c = k_ref[pl.ds(start, bk), :]
        for c in which:
            st_ref[slot, c] = _dot_nt(k_c, qs[c])

    def values_t(j):
        return jnp.concatenate([vt_ref[j * kv_tiles + t] for t in range(kv_tiles)], axis=1)

    gdn_x = functools.partial(_gdn_state_chunks, w_ref=gw_ref, u_ref=gu_ref, qg_ref=gqg_ref, kd_ref=gkd_ref,
                              attn_ref=gattn_ref, egl_ref=gegl_ref, s_ref=s_ref, z_ref=gz_ref,
                              gain_ref=ggain_ref, o_ref=og_ref)
    gdn_x(range(GDN_STATE_CHUNKS // 2))

    k_meta = k_ref[rows - N_META:rows, :]
    sts = [_dot_nt(k_meta, q) for q in qs]
    store_scores(0, 0, every)
    vt_meta = vt_ref[rows // KV_TILE - 1][:, KV_TILE - N_META:]
    ms = update(sts, vt_meta, [None] * len(chains), every)

    def full_blocks(t, ms):
        for n in range(nsub):
            j = nsub * t + n
            store_scores(j + 1, (n + 1) % 2, every)
            ms = update([st_ref[n % 2, c] for c in every], values_t(j), ms, every)
        return tuple(ms)

    ms = lax.fori_loop(0, i, full_blocks, tuple(ms))
    tri = (lax.broadcasted_iota(jnp.int32, (bk, ATTN_QSUB), 0)
           <= lax.broadcasted_iota(jnp.int32, (bk, ATTN_QSUB), 1))
    for d in range(nsub):
        if d + 1 < nsub:
            store_scores(i * nsub + d + 1, (d + 1) % 2, [c for c in every if chains[c][1] > d])
        which = [c for c in every if chains[c][1] >= d]
        sts = [jnp.where(tri, st_ref[d % 2, c], MASK_VALUE) if chains[c][1] == d else st_ref[d % 2, c]
               for c in which]
        ms = update(sts, values_t(i * nsub + d), ms, which)

    gdn_x(range(GDN_STATE_CHUNKS // 2, GDN_STATE_CHUNKS))

    lp = lam_ref[...]
    lam = (jnp.exp(jnp.sum(lp[0:1] * lp[1:2], axis=-1, keepdims=True))
           - jnp.exp(jnp.sum(lp[2:3] * lp[3:4], axis=-1, keepdims=True)) + LAMBDA_INIT)
    gain = gain_ref[...]
    for sb in range(nsub):
        num1, num2 = acc_ref[sb, :DIFF_VDIM, :], acc_ref[nsub + sb, :DIFF_VDIM, :]
        l1 = acc_ref[sb, DIFF_VDIM:DIFF_VDIM + 1, :]
        l2 = acc_ref[nsub + sb, DIFF_VDIM:DIFF_VDIM + 1, :]
        ot = num1 * (1.0 / l1) - num2 * (lam / l2)
        ot = ot * lax.rsqrt(jnp.mean(ot * ot, axis=0, keepdims=True) + NORM_EPS) * gain
        o_ref[sb * ATTN_QSUB:(sb + 1) * ATTN_QSUB, :] = (ot * (1.0 - LAMBDA_INIT)).T.astype(BF16)


def _diff_attn_gdn_state(q2, kr, vt, lam_params, gain_col, gdn_local_out, proj, gdn_gain, seq):
    rows = kr.shape[0]
    bq = ATTN_BLOCK
    nq = seq // bq
    nchains = 2 * (bq // ATTN_QSUB)
    gb = GDN_STATE_CHUNKS * CHUNK
    last_chunk = rows // CHUNK - 1
    assert seq % bq == 0 and bq % (2 * ATTN_QSUB) == 0 and DIFF_HEADS * nq * gb == seq
    w, u, qg, kd, attn, egl = gdn_local_out
    step = lambda h, i: h * nq + i
    xrows = lambda col: pl.BlockSpec((gb, GDN_WIDTH), lambda h, i, col=col: (step(h, i), col))
    mrows = pl.BlockSpec((CHUNK, GDN_WIDTH), lambda h, i: (last_chunk, 0))
    return pl.pallas_call(
        _diff_attn_kernel,
        grid=(DIFF_HEADS, nq),
        in_specs=[
            pl.BlockSpec((2, bq, DIFF_VDIM), lambda h, i: (0, i, h)),
            pl.BlockSpec((rows, DIFF_VDIM), lambda h, i: (0, h)),
            pl.BlockSpec((rows // KV_TILE, VT_ROWS, KV_TILE), lambda h, i: (0, h, 0)),
            pl.BlockSpec((4, DIFF_DIM), lambda h, i: (0, 0)),
            pl.BlockSpec((DIFF_VDIM, 1), lambda h, i: (0, 0)),
            xrows(0), xrows(0), xrows(0), xrows(0),
            pl.BlockSpec((GDN_HEADS, gb, CHUNK), lambda h, i: (0, step(h, i), 0)),
            pl.BlockSpec((GDN_STATE_CHUNKS, GDN_HEADS, GDN_DIM), lambda h, i: (step(h, i), 0, 0)),
            xrows(3),
            mrows, mrows, mrows, mrows,
            pl.BlockSpec((GDN_HEADS, CHUNK, CHUNK), lambda h, i: (0, last_chunk, 0)),
            pl.BlockSpec((1, GDN_HEADS, GDN_DIM), lambda h, i: (last_chunk, 0, 0)),
            pl.BlockSpec((1, GDN_DIM), lambda h, i: (0, 0)),
        ],
        out_specs=[
            pl.BlockSpec((bq, DIFF_VDIM), lambda h, i: (i, h)),
            pl.BlockSpec((gb, GDN_WIDTH), lambda h, i: (step(h, i), 0)),
        ],
        out_shape=[
            jax.ShapeDtypeStruct((seq, DIFF_WIDTH), BF16),
            jax.ShapeDtypeStruct((seq, GDN_WIDTH), BF16),
        ],
        scratch_shapes=[pltpu.VMEM((nchains, VT_ROWS, ATTN_QSUB), F32),
                        pltpu.VMEM((2, nchains, ATTN_QSUB, ATTN_QSUB), F32),
                        pltpu.VMEM((GDN_HEADS, GDN_DIM, GDN_DIM), F32)],
        compiler_params=_params(("arbitrary", "arbitrary")),
        name="diff_attn_gdn_state",
    )(q2, kr, vt, lam_params, gain_col, w, u, qg, kd, attn, egl, proj, w, u, qg, kd, attn, egl, gdn_gain)


def _out_proj_kernel(mg_ref, md_ref, wg_ref, wd_ref, h_ref, gain_ref, h2_ref, n2_ref):
    h2 = (h_ref[...] + _dot(mg_ref[...], wg_ref[...].astype(BF16))
          + _dot(md_ref[...], wd_ref[...].astype(BF16)))
    h2_ref[...] = h2
    ms = jnp.mean(h2 * h2, axis=-1, keepdims=True)
    n2_ref[...] = (h2 * lax.rsqrt(ms + NORM_EPS) * gain_ref[...]).astype(BF16)


def _out_proj(mix_g, mix_d, w_out, h, gain, seq):
    tm = _pick(seq, (512, 128))
    return pl.pallas_call(
        _out_proj_kernel,
        grid=(seq // tm,),
        in_specs=[
            pl.BlockSpec((tm, GDN_WIDTH), lambda i: (i, 0)),
            pl.BlockSpec((tm, DIFF_WIDTH), lambda i: (i, 0)),
            pl.BlockSpec((GDN_WIDTH, D_MODEL), lambda i: (0, 0), pipeline_mode=pl.Buffered(1)),
            pl.BlockSpec((DIFF_WIDTH, D_MODEL), lambda i: (1, 0), pipeline_mode=pl.Buffered(1)),
            pl.BlockSpec((tm, D_MODEL), lambda i: (i, 0)),
            pl.BlockSpec((1, D_MODEL), lambda i: (0, 0)),
        ],
        out_specs=[
            pl.BlockSpec((tm, D_MODEL), lambda i: (i, 0)),
            pl.BlockSpec((tm, D_MODEL), lambda i: (i, 0)),
        ],
        out_shape=[
            jax.ShapeDtypeStruct((seq, D_MODEL), F32),
            jax.ShapeDtypeStruct((seq, D_MODEL), BF16),
        ],
        compiler_params=_params(("parallel",)),
        name="out_proj",
    )(mix_g, mix_d, w_out, w_out, h, gain)


def _gate_up_kernel(n_ref, wg_ref, wu_ref, a_ref):
    n = n_ref[...]
    g = _dot(n, wg_ref[...].astype(BF16))
    u = _dot(n, wu_ref[...].astype(BF16))
    a_ref[...] = (_silu(g) * u).astype(BF16)


def _gate_up(n2, w_gu):
    seq = n2.shape[0]
    tm = _pick(seq, (2048, 1024, 128))
    tn = 512
    nt = D_FF // tn
    return pl.pallas_call(
        _gate_up_kernel,
        grid=(nt, seq // tm),
        in_specs=[
            pl.BlockSpec((tm, D_MODEL), lambda j, i: (i, 0)),
            pl.BlockSpec((D_MODEL, tn), lambda j, i: (0, j)),
            pl.BlockSpec((D_MODEL, tn), lambda j, i: (0, j + nt)),
        ],
        out_specs=pl.BlockSpec((tm, tn), lambda j, i: (i, j)),
        out_shape=jax.ShapeDtypeStruct((seq, D_FF), BF16),
        compiler_params=_params(("parallel", "parallel")),
        name="ffn_gate_up",
    )(n2, w_gu, w_gu)


def _down_kernel(a_ref, w_ref, h_ref, o_ref):
    o_ref[...] = h_ref[...] + _dot(a_ref[...], w_ref[...].astype(BF16))


def _down(act, w_down, h2):
    seq = act.shape[0]
    tm = _pick(seq, (512, 128))
    tn = 1024
    return pl.pallas_call(
        _down_kernel,
        grid=(D_MODEL // tn, seq // tm),
        in_specs=[
            pl.BlockSpec((tm, D_FF), lambda j, i: (i, 0)),
            pl.BlockSpec((D_FF, tn), lambda j, i: (0, j), pipeline_mode=pl.Buffered(1)),
            pl.BlockSpec((tm, tn), lambda j, i: (i, j)),
        ],
        out_specs=pl.BlockSpec((tm, tn), lambda j, i: (i, j)),
        out_shape=jax.ShapeDtypeStruct((seq, D_MODEL), F32),
        compiler_params=_params(("parallel", "parallel")),
        name="ffn_down",
    )(act, w_down, h2)


def _rope_tables(seq):
    half = DIFF_DIM // 2
    pos = jnp.concatenate([jnp.arange(seq) + N_META, jnp.zeros((META_BLOCK - N_META,), jnp.int32),
                           jnp.arange(N_META)]).astype(F32)
    inv_freq = ROPE_THETA ** (-jnp.arange(half, dtype=F32) / half)
    ang = pos[:, None] * inv_freq[None, :]
    cos = jnp.tile(jnp.cos(ang), (1, LANES // half))
    sin = jnp.sin(ang)
    sin = jnp.tile(jnp.concatenate([-sin, sin], axis=1), (1, LANES // DIFF_DIM))
    return cos, sin


def _lane_pad(v, offset):
    return jnp.zeros((1, GATE_LANES), F32).at[0, offset:offset + v.shape[0]].set(v.astype(F32))


def kernel(x, meta_tokens, attn_norm, w_in, conv_w, a_log, dt_bias, gdn_norm, q_norm, k_norm,
           lambda_q1, lambda_k1, lambda_q2, lambda_k2, diff_norm, w_out, ffn_norm, w_gate_up, w_down):
    assert x.shape[0] == 1 and x.shape[2] == D_MODEL
    seq = x.shape[1]
    assert seq % META_BLOCK == 0
    xs = x[0]
    meta_block = jnp.concatenate([jnp.zeros((META_BLOCK - N_META, D_MODEL), xs.dtype),
                                  meta_tokens.astype(xs.dtype)], axis=0)

    gdn_cols = 4 * GDN_WIDTH
    wt_in = w_in[0].T
    wt_diff = wt_in[gdn_cols + 2 * GDN_HEADS:]
    wt_ba = jnp.pad(wt_in[gdn_cols:gdn_cols + 2 * GDN_HEADS],
                    ((0, GATE_LANES - 2 * GDN_HEADS), (0, 0))).astype(BF16)

    n1, gcol, grow = _prenorm_gate(xs, meta_block, attn_norm, wt_ba, _lane_pad(a_log[0], GDN_HEADS),
                                   _lane_pad(dt_bias[0], GDN_HEADS))
    proj = _in_proj(n1, wt_in, wt_diff, conv_w[0])

    cos, sin = _rope_tables(seq)
    tile2 = lambda g: jnp.tile(g.astype(F32), (1, LANES // DIFF_DIM))
    lane = np.arange(LANES)
    gsum = jnp.asarray((lane[:, None] // DIFF_DIM) == (lane[None, :] // DIFF_DIM), BF16)
    eye = jnp.asarray(lane[:, None] == lane[None, :], BF16)
    gdn_local_out, (q2, kr, vt) = _gdn_local_attn_prep(
        _gdn_local_specs(proj, gcol, grow),
        _attn_prep_specs(proj, cos, sin, tile2(q_norm), tile2(k_norm), gsum, eye))
    lam_params = jnp.concatenate([lambda_q1, lambda_k1, lambda_q2, lambda_k2], axis=0).astype(F32)
    mix_d, mix_g = _diff_attn_gdn_state(q2, kr, vt, lam_params, diff_norm.astype(F32).reshape(DIFF_VDIM, 1),
                                        gdn_local_out, proj, gdn_norm, seq)

    h2, n2 = _out_proj(mix_g, mix_d, w_out[0], xs, ffn_norm, seq)
    act = _gate_up(n2, w_gate_up[0])
    out = _down(act, w_down[0], h2)
    return out[None]
```

```python
import functools
import math

import jax
import jax.numpy as jnp
import numpy as np
from jax import lax
from jax.experimental import pallas as pl
from jax.experimental.pallas import tpu as pltpu

F32 = jnp.float32
BF16 = jnp.bfloat16

D_MODEL = 2048
N_META = 16
GDN_HEADS = 8
GDN_DIM = 128
GDN_WIDTH = GDN_HEADS * GDN_DIM
CONV_WIDTH = 4
CHUNK = 64
DIFF_HEADS = 8
DIFF_DIM = 64
DIFF_VDIM = 2 * DIFF_DIM
DIFF_WIDTH = DIFF_HEADS * DIFF_VDIM
ROPE_THETA = 10000.0
D_FF = 5632
NORM_EPS = 1e-6
MASK_VALUE = -1e30
LAMBDA_INIT = 0.8 - 0.6 * math.exp(-0.3 * 0)

LANES = 128
META_BLOCK = 512
GATE_LANES = 128
VMEM_LIMIT = 56 * 1024 * 1024


def _pick(n, candidates):
    for c in candidates:
        if n % c == 0:
            return c
    raise ValueError(f"no tile in {candidates} divides {n}")


def _params(sem, vmem=VMEM_LIMIT):
    return pltpu.CompilerParams(dimension_semantics=sem, vmem_limit_bytes=vmem)


def _dot(a, b):
    return jnp.dot(a, b, preferred_element_type=F32)


def _dot_nt(a, b):
    return lax.dot_general(a, b, (((1,), (1,)), ((), ())), preferred_element_type=F32)


def _dot_tn(a, b):
    return lax.dot_general(a, b, (((0,), (0,)), ((), ())), preferred_element_type=F32)


def _softplus(x):
    return jnp.maximum(x, 0.0) + jnp.log1p(jnp.exp(-jnp.abs(x)))


def _silu(x):
    return x * jax.nn.sigmoid(x)


def _silu_tanh(x):
    h = 0.5 * x
    return h + h * jnp.tanh(h)


def _prenorm_gate_kernel(x_ref, mb_ref, gain_ref, wba_ref, alog_ref, dtb_ref, n_ref, gcol_ref, grow_ref):
    h = jnp.where(pl.program_id(0) < pl.num_programs(0) - 1, x_ref[...], mb_ref[...])
    ms = jnp.mean(h * h, axis=-1, keepdims=True)
    n = (h * lax.rsqrt(ms + NORM_EPS) * gain_ref[...]).astype(BF16)
    n_ref[...] = n
    ba = _dot_nt(n, wba_ref[...])
    beta = jax.nn.sigmoid(ba)
    g = -jnp.exp(alog_ref[...]) * _softplus(ba + dtb_ref[...])
    row = lax.broadcasted_iota(jnp.int32, ba.shape, 0) % CHUNK
    gc = g
    for d in (1, 2, 4, 8, 16, 32):
        gc = gc + jnp.where(row >= d, pltpu.roll(gc, d, axis=0), 0.0)
    lane = lax.broadcasted_iota(jnp.int32, ba.shape, 1)
    out = jnp.where(lane < GDN_HEADS, beta, gc)
    gcol_ref[...] = out
    grow_ref[...] = out.T[: 2 * GDN_HEADS]


def _prenorm_gate(x, meta_block, gain, wba, alog, dtb):
    tm = META_BLOCK
    nx = x.shape[0] // tm
    rows = x.shape[0] + tm
    return pl.pallas_call(
        _prenorm_gate_kernel,
        grid=(nx + 1,),
        in_specs=[
            pl.BlockSpec((tm, D_MODEL), lambda i: (jnp.minimum(i, nx - 1), 0)),
            pl.BlockSpec((tm, D_MODEL), lambda i: (0, 0)),
            pl.BlockSpec((1, D_MODEL), lambda i: (0, 0)),
            pl.BlockSpec((GATE_LANES, D_MODEL), lambda i: (0, 0)),
            pl.BlockSpec((1, GATE_LANES), lambda i: (0, 0)),
            pl.BlockSpec((1, GATE_LANES), lambda i: (0, 0)),
        ],
        out_specs=[
            pl.BlockSpec((tm, D_MODEL), lambda i: (i, 0)),
            pl.BlockSpec((tm, GATE_LANES), lambda i: (i, 0)),
            pl.BlockSpec((2 * GDN_HEADS, tm), lambda i: (0, i)),
        ],
        out_shape=[
            jax.ShapeDtypeStruct((rows, D_MODEL), BF16),
            jax.ShapeDtypeStruct((rows, GATE_LANES), F32),
            jax.ShapeDtypeStruct((2 * GDN_HEADS, rows), F32),
        ],
        compiler_params=_params(("parallel",)),
        name="prenorm_gate",
    )(x, meta_block, gain, wba, alog, dtb)


IN_PROJ_TN = 1024
GDN_COL_TILES = 4 * GDN_WIDTH // IN_PROJ_TN


def _shift_rows(x, prev, d):
    r8 = lax.broadcasted_iota(jnp.int32, prev.shape, 0)
    shifted = pltpu.roll(x, d, axis=0)
    top = jnp.where(r8 < d, pltpu.roll(prev, d, axis=0), shifted[:8])
    return jnp.concatenate([top, shifted[8:]], axis=0)


def _causal_conv_silu(x, prev, w):
    assert CONV_WIDTH == 4
    w0, w1, w2, w3 = (w[t:t + 1] for t in range(CONV_WIDTH))
    x1 = _shift_rows(x, prev, 1)
    b = x * w1 + x1 * w0
    b_prev = prev * w1 + pltpu.roll(prev, 1, axis=0) * w0
    return _silu_tanh(x * w3 + x1 * w2 + _shift_rows(b, b_prev, 2))


def _in_proj_kernel(a_ref, wg_ref, wd_ref, cw_ref, o_ref, tail_ref, raw_ref, w16_ref):
    j = pl.program_id(0)
    i = pl.program_id(1)
    tm = a_ref.shape[0]

    @pl.when(i == 0)
    def _():
        tail_ref[...] = jnp.zeros_like(tail_ref)

    @pl.when((i == 0) & (j < GDN_COL_TILES))
    def _():
        w16_ref[...] = wg_ref[...].T.astype(BF16)

    @pl.when((i == 0) & (j >= GDN_COL_TILES))
    def _():
        w16_ref[...] = wd_ref[...].T.astype(BF16)

    def gdn_qkv(l2_scale):
        a = a_ref[...]
        for pair in range(GDN_HEADS // 2):
            cs = slice(pair * 2 * GDN_DIM, (pair + 1) * 2 * GDN_DIM)
            raw_ref[pair] = _dot(a, w16_ref[:, cs])
        for pair in range(GDN_HEADS // 2):
            cs = slice(pair * 2 * GDN_DIM, (pair + 1) * 2 * GDN_DIM)
            raw = raw_ref[pair]
            y = _causal_conv_silu(raw, tail_ref[:, cs], cw_ref[:, cs])
            tail_ref[:, cs] = raw[tm - 8:]
            for half in range(2):
                ls = slice(half * GDN_DIM, (half + 1) * GDN_DIM)
                yh = y[:, ls]
                if l2_scale is not None:
                    yh = yh * (lax.rsqrt(jnp.sum(yh * yh, axis=-1, keepdims=True) + NORM_EPS) * l2_scale)
                o_ref[:, pair * 2 * GDN_DIM + half * GDN_DIM:pair * 2 * GDN_DIM + (half + 1) * GDN_DIM] = yh

    pl.when(j == 0)(functools.partial(gdn_qkv, GDN_DIM ** -0.5))
    pl.when(j == 1)(functools.partial(gdn_qkv, 1.0))
    pl.when(j == 2)(functools.partial(gdn_qkv, None))

    @pl.when(j >= GDN_COL_TILES - 1)
    def _():
        o_ref[...] = _dot(a_ref[...], w16_ref[...])


def _in_proj(n1, wt_all, wt_diff, conv_w):
    m = n1.shape[0]
    tm, tn = META_BLOCK, IN_PROJ_TN
    nm = m // tm
    n = GDN_COL_TILES * tn + wt_diff.shape[0]
    seq_order = lambda i: (i + nm - 1) % nm
    return pl.pallas_call(
        _in_proj_kernel,
        grid=(n // tn, nm),
        in_specs=[
            pl.BlockSpec((tm, D_MODEL), lambda j, i: (seq_order(i), 0)),
            pl.BlockSpec((tn, D_MODEL), lambda j, i: (jnp.minimum(j, GDN_COL_TILES - 1), 0)),
            pl.BlockSpec((tn, D_MODEL), lambda j, i: (jnp.maximum(j - GDN_COL_TILES, 0), 0)),
            pl.BlockSpec((CONV_WIDTH, tn), lambda j, i: (0, jnp.minimum(j, 2))),
        ],
        out_specs=pl.BlockSpec((tm, tn), lambda j, i: (seq_order(i), j)),
        out_shape=jax.ShapeDtypeStruct((m, n), F32),
        scratch_shapes=[pltpu.VMEM((8, tn), F32), pltpu.VMEM((GDN_HEADS // 2, tm, 2 * GDN_DIM), F32),
                        pltpu.VMEM((D_MODEL, tn), BF16)],
        compiler_params=_params(("parallel", "arbitrary")),
        name="in_proj",
    )(n1, wt_all, wt_diff, conv_w)


GDN_LOCAL_CHUNKS = 4


def _gdn_local_kernel(q_ref, k_ref, v_ref, gcol_ref, grow_ref,
                      w_ref, u_ref, qg_ref, kd_ref, attn_ref, egl_ref):
    q_all = q_ref[...]
    k_all = k_ref[...]
    v_all = v_ref[...]
    gcol = gcol_ref[...]
    grow = grow_ref[...]
    ii = lax.broadcasted_iota(jnp.int32, (CHUNK, CHUNK), 0)
    jj = lax.broadcasted_iota(jnp.int32, (CHUNK, CHUNK), 1)
    units = [(c, h) for c in range(GDN_LOCAL_CHUNKS) for h in range(GDN_HEADS)]
    rows_of = lambda c: slice(c * CHUNK, (c + 1) * CHUNK)
    lanes_of = lambda h: slice(h * GDN_DIM, (h + 1) * GDN_DIM)
    qs, ks, kbs, vbs, egcs, gcs, kqs = [], [], [], [], [], [], []
    for c, h in units:
        rs, ls = rows_of(c), lanes_of(h)
        q = q_all[rs, ls]
        k = k_all[rs, ls]
        beta_c = gcol[rs, h:h + 1]
        gc_c = gcol[rs, GDN_HEADS + h:GDN_HEADS + h + 1]
        kb = k * beta_c
        qs.append(q)
        ks.append(k)
        kbs.append(kb)
        vbs.append(v_all[rs, ls] * beta_c)
        gcs.append(gc_c)
        egcs.append(jnp.exp(gc_c))
        kqs.append(_dot_nt(jnp.concatenate([kb.astype(BF16), q.astype(BF16)], axis=0), k.astype(BF16)))
    lms, attns = [], []
    for (c, h), kq, gc_c in zip(units, kqs, gcs):
        gc_r = grow[GDN_HEADS + h:GDN_HEADS + h + 1, rows_of(c)]
        decay = jnp.exp(jnp.where(ii >= jj, gc_c - gc_r, MASK_VALUE))
        lms.append(jnp.where(ii > jj, kq[:CHUNK] * decay, 0.0))
        attns.append(kq[CHUNK:] * decay)
    xor = ii ^ jj
    eye = jnp.where(ii == jj, 1.0, 0.0)
    xs = [eye - jnp.where(xor == 1, lm, 0.0) for lm in lms]
    level = 1
    while (2 << level) <= CHUNK:
        sel = (xor >> level) == 1
        ys = [_dot(jnp.where(sel, lm, 0.0).astype(BF16), x.astype(BF16)) for lm, x in zip(lms, xs)]
        xs = [x - _dot(x.astype(BF16), y.astype(BF16)) for x, y in zip(xs, ys)]
        level += 1
    uws = [_dot(x.astype(BF16), jnp.concatenate([vb.astype(BF16), (kb * egc).astype(BF16)], axis=1))
           for x, vb, kb, egc in zip(xs, vbs, kbs, egcs)]
    for (c, h), uw, q, k, egc, gc_c, attn in zip(units, uws, qs, ks, egcs, gcs, attns):
        rs, ls = rows_of(c), lanes_of(h)
        u_ref[rs, ls] = uw[:, :GDN_DIM]
        w_ref[rs, ls] = uw[:, GDN_DIM:].astype(BF16)
        qg_ref[rs, ls] = (q * egc).astype(BF16)
        gc_last = gc_c[CHUNK - 1:CHUNK]
        kd_ref[rs, ls] = (k * jnp.exp(gc_last - gc_c)).astype(BF16)
        attn_ref[h, rs, :] = attn.astype(BF16)
        egl_ref[c, h:h + 1, :] = jnp.broadcast_to(jnp.exp(gc_last), (1, GDN_DIM))


def _gdn_local_specs(proj, gcol, grow):
    rows = proj.shape[0]
    rb = GDN_LOCAL_CHUNKS * CHUNK
    blk = lambda col: pl.BlockSpec((rb, GDN_WIDTH), lambda i, col=col: (i, col))
    row_out = lambda dt: jax.ShapeDtypeStruct((rows, GDN_WIDTH), dt)
    in_specs = [
        blk(0), blk(1), blk(2),
        pl.BlockSpec((rb, GATE_LANES), lambda i: (i, 0)),
        pl.BlockSpec((2 * GDN_HEADS, rb), lambda i: (0, i)),
    ]
    out_specs = [
        blk(0), blk(0), blk(0), blk(0),
        pl.BlockSpec((GDN_HEADS, rb, CHUNK), lambda i: (0, i, 0)),
        pl.BlockSpec((GDN_LOCAL_CHUNKS, GDN_HEADS, GDN_DIM), lambda i: (i, 0, 0)),
    ]
    out_shape = [
        row_out(BF16),
        row_out(F32),
        row_out(BF16),
        row_out(BF16),
        jax.ShapeDtypeStruct((GDN_HEADS, rows, CHUNK), BF16),
        jax.ShapeDtypeStruct((rows // CHUNK, GDN_HEADS, GDN_DIM), F32),
    ]
    return (proj, proj, proj, gcol, grow), in_specs, out_specs, out_shape


GDN_STATE_CHUNKS = 2


def _gdn_state_chunks(chunks, w_ref, u_ref, qg_ref, kd_ref, attn_ref, egl_ref, s_ref,
                      z_ref=None, gain_ref=None, o_ref=None):
    heads = range(GDN_HEADS)
    lanes = [slice(h * GDN_DIM, (h + 1) * GDN_DIM) for h in heads]
    ss = [s_ref[h] for h in heads]
    for ck in chunks:
        rw = slice(ck * CHUNK, (ck + 1) * CHUNK)
        rs = [_dot(jnp.concatenate([w_ref[rw, ls], qg_ref[rw, ls]], axis=0), s.astype(BF16))
              for ls, s in zip(lanes, ss)]
        vns = [(u_ref[rw, ls] - r[:CHUNK]).astype(BF16) for ls, r in zip(lanes, rs)]
        if o_ref is not None:
            os_ = [r[CHUNK:] + _dot(attn_ref[h, rw, :], vn) for h, r, vn in zip(heads, rs, vns)]
        ss = [s * egl_ref[ck, h:h + 1, :] + _dot_tn(kd_ref[rw, ls], vn)
              for h, ls, s, vn in zip(heads, lanes, ss, vns)]
        if o_ref is not None:
            for ls, o in zip(lanes, os_):
                on = o * lax.rsqrt(jnp.mean(o * o, axis=-1, keepdims=True) + NORM_EPS) * gain_ref[...]
                o_ref[rw, ls] = (on * _silu(z_ref[rw, ls])).astype(BF16)
    for h, s in zip(heads, ss):
        s_ref[h] = s


KV_TILE = 128
VT_ROWS = DIFF_VDIM + 16
Q_SCALE = DIFF_DIM ** -0.5 * math.log2(math.e)


PREP_ROWS = 2 * KV_TILE


def _attn_prep_kernel(q_ref, k_ref, v_ref, cos_ref, sin_ref, qg_ref, kg_ref, gsum_ref, eye_ref,
                      q2_ref, kr_ref, vt_ref):
    cos = cos_ref[...]
    sin = sin_ref[...]
    gsum = gsum_ref[...]
    eye = eye_ref[...]
    lane = lax.broadcasted_iota(jnp.int32, cos.shape, 1)
    first_half = (lane % DIFF_DIM) < (DIFF_DIM // 2)
    low_map = lane < DIFF_DIM

    def norm_rope(x, gain):
        ms = _dot((x * x).astype(BF16), gsum) * (1.0 / DIFF_DIM)
        xn = x * lax.rsqrt(ms + NORM_EPS) * gain
        rot = jnp.where(first_half, pltpu.roll(xn, LANES - DIFF_DIM // 2, axis=1),
                        pltpu.roll(xn, DIFF_DIM // 2, axis=1))
        return xn * cos + rot * sin

    for h in range(DIFF_HEADS):
        ls = slice(h * DIFF_VDIM, (h + 1) * DIFF_VDIM)
        q = norm_rope(q_ref[:, ls], qg_ref[...]) * Q_SCALE
        q2_ref[0, :, ls] = jnp.where(low_map, q, 0.0).astype(BF16)
        q2_ref[1, :, ls] = jnp.where(low_map, 0.0, q).astype(BF16)
        kr_ref[:, ls] = norm_rope(k_ref[:, ls], kg_ref[...]).astype(BF16)
        for t in range(PREP_ROWS // KV_TILE):
            v = v_ref[t * KV_TILE:(t + 1) * KV_TILE, ls].astype(BF16)
            vt_ref[t, h * VT_ROWS:h * VT_ROWS + DIFF_VDIM, :] = _dot_nt(eye, v).astype(BF16)
            vt_ref[t, h * VT_ROWS + DIFF_VDIM:(h + 1) * VT_ROWS, :] = jnp.ones(
                (VT_ROWS - DIFF_VDIM, KV_TILE), BF16)


def _attn_prep_specs(proj, cos, sin, qgain, kgain, gsum, eye):
    rows = proj.shape[0]
    tm = PREP_ROWS
    col = lambda c: pl.BlockSpec((tm, DIFF_WIDTH), lambda i, c=c: (i, c))
    small = lambda shape: pl.BlockSpec(shape, lambda i: (0,) * len(shape))
    in_specs = [
        col(4), col(5), col(6),
        pl.BlockSpec((tm, LANES), lambda i: (i, 0)),
        pl.BlockSpec((tm, LANES), lambda i: (i, 0)),
        small((1, LANES)), small((1, LANES)), small((LANES, LANES)), small((LANES, LANES)),
    ]
    out_specs = [
        pl.BlockSpec((2, tm, DIFF_WIDTH), lambda i: (0, i, 0)),
        pl.BlockSpec((tm, DIFF_WIDTH), lambda i: (i, 0)),
        pl.BlockSpec((tm // KV_TILE, DIFF_HEADS * VT_ROWS, KV_TILE), lambda i: (i, 0, 0)),
    ]
    out_shape = [
        jax.ShapeDtypeStruct((2, rows, DIFF_WIDTH), BF16),
        jax.ShapeDtypeStruct((rows, DIFF_WIDTH), BF16),
        jax.ShapeDtypeStruct((rows // KV_TILE, DIFF_HEADS * VT_ROWS, KV_TILE), BF16),
    ]
    return (proj, proj, proj, cos, sin, qgain, kgain, gsum, eye), in_specs, out_specs, out_shape


def _gdn_local_attn_prep(gdn_part, prep_part):
    g_ops, g_in, g_out, g_shape = gdn_part
    p_ops, p_in, p_out, p_shape = prep_part
    assert GDN_LOCAL_CHUNKS * CHUNK == PREP_ROWS
    rows = g_ops[0].shape[0]

    def body(*refs):
        g_i, rest = refs[:len(g_in)], refs[len(g_in):]
        p_i, rest = rest[:len(p_in)], rest[len(p_in):]
        g_o, p_o = rest[:len(g_out)], rest[len(g_out):]
        _attn_prep_kernel(*p_i, *p_o)
        _gdn_local_kernel(*g_i, *g_o)

    outs = pl.pallas_call(
        body,
        grid=(rows // PREP_ROWS,),
        in_specs=g_in + p_in,
        out_specs=g_out + p_out,
        out_shape=g_shape + p_shape,
        compiler_params=_params(("parallel",)),
        name="gdn_local_attn_prep",
    )(*g_ops, *p_ops)
    return outs[:len(g_out)], outs[len(g_out):]


ATTN_BLOCK = 1024
ATTN_QSUB = 256

def _diff_attn_kernel(q_ref, k_ref, vt_ref, lam_ref, gain_ref,
                      gw_ref, gu_ref, gqg_ref, gkd_ref, gattn_ref, gegl_ref, gz_ref,
                      mw_ref, mu_ref, mqg_ref, mkd_ref, mattn_ref, megl_ref, ggain_ref,
                      o_ref, og_ref, acc_ref, st_ref, s_ref):
    i = pl.program_id(1)
    step = pl.program_id(0) * pl.num_programs(1) + i

    @pl.when(step == 0)
    def _():
        s_ref[...] = jnp.zeros_like(s_ref)
        _gdn_state_chunks(range(1), mw_ref, mu_ref, mqg_ref, mkd_ref, mattn_ref, megl_ref, s_ref)
    rows = k_ref.shape[0]
    bk = ATTN_QSUB
    nsub = q_ref.shape[1] // ATTN_QSUB
    chains = [(mp, sb) for mp in range(2) for sb in range(nsub)]
    every = list(range(len(chains)))
    qs = [q_ref[mp, sb * ATTN_QSUB:(sb + 1) * ATTN_QSUB, :] for mp, sb in chains]
    kv_tiles = bk // KV_TILE

    def update(sts, vt, ms, which):
        ms = list(ms)
        first = ms[which[0]] is None
        cms = [jnp.max(st, axis=0, keepdims=True) for st in sts]
        m_new = cms if first else [jnp.maximum(ms[c], cm) for c, cm in zip(which, cms)]
        ps = [jnp.exp2(st - mn).astype(BF16) for st, mn in zip(sts, m_new)]
        pvs = [_dot(vt, p) for p in ps]
        for n, c in enumerate(which):
            if first:
                acc_ref[c] = pvs[n]
            else:
                acc_ref[c] = jnp.exp2(ms[c] - m_new[n]) * acc_ref[c] + pvs[n]
            ms[c] = m_new[n]
        return ms

    def store_scores(j, slot, which):
        start = j * bk if isinstance(j, int) else pl.multiple_of(j * bk, bk)
        k_c = k_ref[pl.ds(start, bk), :]
        for c in which:
            st_ref[slot, c] = _dot_nt(k_c, qs[c])

    def values_t(j):
        return jnp.concatenate([vt_ref[j * kv_tiles + t] for t in range(kv_tiles)], axis=1)

    gdn_x = functools.partial(_gdn_state_chunks, w_ref=gw_ref, u_ref=gu_ref, qg_ref=gqg_ref, kd_ref=gkd_ref,
                              attn_ref=gattn_ref, egl_ref=gegl_ref, s_ref=s_ref, z_ref=gz_ref,
                              gain_ref=ggain_ref, o_ref=og_ref)
    gdn_x(range(GDN_STATE_CHUNKS // 2))

    k_meta = k_ref[rows - N_META:rows, :]
    sts = [_dot_nt(k_meta, q) for q in qs]
    store_scores(0, 0, every)
    vt_meta = vt_ref[rows // KV_TILE - 1][:, KV_TILE - N_META:]
    ms = update(sts, vt_meta, [None] * len(chains), every)

    def full_blocks(t, ms):
        for n in range(nsub):
            j = nsub * t + n
            store_scores(j + 1, (n + 1) % 2, every)
            ms = update([st_ref[n % 2, c] for c in every], values_t(j), ms, every)
        return tuple(ms)

    ms = lax.fori_loop(0, i, full_blocks, tuple(ms))
    tri = (lax.broadcasted_iota(jnp.int32, (bk, ATTN_QSUB), 0)
           <= lax.broadcasted_iota(jnp.int32, (bk, ATTN_QSUB), 1))
    for d in range(nsub):
        if d + 1 < nsub:
            store_scores(i * nsub + d + 1, (d + 1) % 2, [c for c in every if chains[c][1] > d])
        which = [c for c in every if chains[c][1] >= d]
        sts = [jnp.where(tri, st_ref[d % 2, c], MASK_VALUE) if chains[c][1] == d else st_ref[d % 2, c]
               for c in which]
        ms = update(sts, values_t(i * nsub + d), ms, which)

    gdn_x(range(GDN_STATE_CHUNKS // 2, GDN_STATE_CHUNKS))

    lp = lam_ref[...]
    lam = (jnp.exp(jnp.sum(lp[0:1] * lp[1:2], axis=-1, keepdims=True))
           - jnp.exp(jnp.sum(lp[2:3] * lp[3:4], axis=-1, keepdims=True)) + LAMBDA_INIT)
    gain = gain_ref[...]
    for sb in range(nsub):
        num1, num2 = acc_ref[sb, :DIFF_VDIM, :], acc_ref[nsub + sb, :DIFF_VDIM, :]
        l1 = acc_ref[sb, DIFF_VDIM:DIFF_VDIM + 1, :]
        l2 = acc_ref[nsub + sb, DIFF_VDIM:DIFF_VDIM + 1, :]
        ot = num1 * (1.0 / l1) - num2 * (lam / l2)
        ot = ot * lax.rsqrt(jnp.mean(ot * ot, axis=0, keepdims=True) + NORM_EPS) * gain
        o_ref[sb * ATTN_QSUB:(sb + 1) * ATTN_QSUB, :] = (ot * (1.0 - LAMBDA_INIT)).T.astype(BF16)


def _diff_attn_gdn_state(q2, kr, vt, lam_params, gain_col, gdn_local_out, proj, gdn_gain, seq):
    rows = kr.shape[0]
    bq = ATTN_BLOCK
    nq = seq // bq
    nchains = 2 * (bq // ATTN_QSUB)
    gb = GDN_STATE_CHUNKS * CHUNK
    last_chunk = rows // CHUNK - 1
    assert seq % bq == 0 and bq % (2 * ATTN_QSUB) == 0 and DIFF_HEADS * nq * gb == seq
    w, u, qg, kd, attn, egl = gdn_local_out
    step = lambda h, i: h * nq + i
    xrows = lambda col: pl.BlockSpec((gb, GDN_WIDTH), lambda h, i, col=col: (step(h, i), col))
    mrows = pl.BlockSpec((CHUNK, GDN_WIDTH), lambda h, i: (last_chunk, 0))
    return pl.pallas_call(
        _diff_attn_kernel,
        grid=(DIFF_HEADS, nq),
        in_specs=[
            pl.BlockSpec((2, bq, DIFF_VDIM), lambda h, i: (0, i, h)),
            pl.BlockSpec((rows, DIFF_VDIM), lambda h, i: (0, h)),
            pl.BlockSpec((rows // KV_TILE, VT_ROWS, KV_TILE), lambda h, i: (0, h, 0)),
            pl.BlockSpec((4, DIFF_DIM), lambda h, i: (0, 0)),
            pl.BlockSpec((DIFF_VDIM, 1), lambda h, i: (0, 0)),
            xrows(0), xrows(0), xrows(0), xrows(0),
            pl.BlockSpec((GDN_HEADS, gb, CHUNK), lambda h, i: (0, step(h, i), 0)),
            pl.BlockSpec((GDN_STATE_CHUNKS, GDN_HEADS, GDN_DIM), lambda h, i: (step(h, i), 0, 0)),
            xrows(3),
            mrows, mrows, mrows, mrows,
            pl.BlockSpec((GDN_HEADS, CHUNK, CHUNK), lambda h, i: (0, last_chunk, 0)),
            pl.BlockSpec((1, GDN_HEADS, GDN_DIM), lambda h, i: (last_chunk, 0, 0)),
            pl.BlockSpec((1, GDN_DIM), lambda h, i: (0, 0)),
        ],
        out_specs=[
            pl.BlockSpec((bq, DIFF_VDIM), lambda h, i: (i, h)),
            pl.BlockSpec((gb, GDN_WIDTH), lambda h, i: (step(h, i), 0)),
        ],
        out_shape=[
            jax.ShapeDtypeStruct((seq, DIFF_WIDTH), BF16),
            jax.ShapeDtypeStruct((seq, GDN_WIDTH), BF16),
        ],
        scratch_shapes=[pltpu.VMEM((nchains, VT_ROWS, ATTN_QSUB), F32),
                        pltpu.VMEM((2, nchains, ATTN_QSUB, ATTN_QSUB), F32),
                        pltpu.VMEM((GDN_HEADS, GDN_DIM, GDN_DIM), F32)],
        compiler_params=_params(("arbitrary", "arbitrary")),
        name="diff_attn_gdn_state",
    )(q2, kr, vt, lam_params, gain_col, w, u, qg, kd, attn, egl, proj, w, u, qg, kd, attn, egl, gdn_gain)


def _out_proj_kernel(mg_ref, md_ref, wg_ref, wd_ref, h_ref, gain_ref, h2_ref, n2_ref):
    h2 = (h_ref[...] + _dot(mg_ref[...], wg_ref[...].astype(BF16))
          + _dot(md_ref[...], wd_ref[...].astype(BF16)))
    h2_ref[...] = h2
    ms = jnp.mean(h2 * h2, axis=-1, keepdims=True)
    n2_ref[...] = (h2 * lax.rsqrt(ms + NORM_EPS) * gain_ref[...]).astype(BF16)


def _out_proj(mix_g, mix_d, w_out, h, gain, seq):
    tm = _pick(seq, (512, 128))
    return pl.pallas_call(
        _out_proj_kernel,
        grid=(seq // tm,),
        in_specs=[
            pl.BlockSpec((tm, GDN_WIDTH), lambda i: (i, 0)),
            pl.BlockSpec((tm, DIFF_WIDTH), lambda i: (i, 0)),
            pl.BlockSpec((GDN_WIDTH, D_MODEL), lambda i: (0, 0), pipeline_mode=pl.Buffered(1)),
            pl.BlockSpec((DIFF_WIDTH, D_MODEL), lambda i: (1, 0), pipeline_mode=pl.Buffered(1)),
            pl.BlockSpec((tm, D_MODEL), lambda i: (i, 0)),
            pl.BlockSpec((1, D_MODEL), lambda i: (0, 0)),
        ],
        out_specs=[
            pl.BlockSpec((tm, D_MODEL), lambda i: (i, 0)),
            pl.BlockSpec((tm, D_MODEL), lambda i: (i, 0)),
        ],
        out_shape=[
            jax.ShapeDtypeStruct((seq, D_MODEL), F32),
            jax.ShapeDtypeStruct((seq, D_MODEL), BF16),
        ],
        compiler_params=_params(("parallel",)),
        name="out_proj",
    )(mix_g, mix_d, w_out, w_out, h, gain)


def _gate_up_kernel(n_ref, wg_ref, wu_ref, a_ref):
    n = n_ref[...]
    g = _dot(n, wg_ref[...].astype(BF16))
    u = _dot(n, wu_ref[...].astype(BF16))
    a_ref[...] = (_silu(g) * u).astype(BF16)


def _gate_up(n2, w_gu):
    seq = n2.shape[0]
    tm = _pick(seq, (1024, 128))
    tn = 512
    nt = D_FF // tn
    return pl.pallas_call(
        _gate_up_kernel,
        grid=(nt, seq // tm),
        in_specs=[
            pl.BlockSpec((tm, D_MODEL), lambda j, i: (i, 0)),
            pl.BlockSpec((D_MODEL, tn), lambda j, i: (0, j)),
            pl.BlockSpec((D_MODEL, tn), lambda j, i: (0, j + nt)),
        ],
        out_specs=pl.BlockSpec((tm, tn), lambda j, i: (i, j)),
        out_shape=jax.ShapeDtypeStruct((seq, D_FF), BF16),
        compiler_params=_params(("parallel", "parallel")),
        name="ffn_gate_up",
    )(n2, w_gu, w_gu)


def _down_kernel(a_ref, w_ref, h_ref, o_ref):
    o_ref[...] = h_ref[...] + _dot(a_ref[...], w_ref[...].astype(BF16))


def _down(act, w_down, h2):
    seq = act.shape[0]
    tm = _pick(seq, (512, 128))
    tn = 1024
    return pl.pallas_call(
        _down_kernel,
        grid=(D_MODEL // tn, seq // tm),
        in_specs=[
            pl.BlockSpec((tm, D_FF), lambda j, i: (i, 0)),
            pl.BlockSpec((D_FF, tn), lambda j, i: (0, j), pipeline_mode=pl.Buffered(1)),
            pl.BlockSpec((tm, tn), lambda j, i: (i, j)),
        ],
        out_specs=pl.BlockSpec((tm, tn), lambda j, i: (i, j)),
        out_shape=jax.ShapeDtypeStruct((seq, D_MODEL), F32),
        compiler_params=_params(("parallel", "parallel")),
        name="ffn_down",
    )(act, w_down, h2)


def _rope_tables(seq):
    half = DIFF_DIM // 2
    pos = jnp.concatenate([jnp.arange(seq) + N_META, jnp.zeros((META_BLOCK - N_META,), jnp.int32),
                           jnp.arange(N_META)]).astype(F32)
    inv_freq = ROPE_THETA ** (-jnp.arange(half, dtype=F32) / half)
    ang = pos[:, None] * inv_freq[None, :]
    cos = jnp.tile(jnp.cos(ang), (1, LANES // half))
    sin = jnp.sin(ang)
    sin = jnp.tile(jnp.concatenate([-sin, sin], axis=1), (1, LANES // DIFF_DIM))
    return cos, sin


def _lane_pad(v, offset):
    return jnp.zeros((1, GATE_LANES), F32).at[0, offset:offset + v.shape[0]].set(v.astype(F32))


def kernel(x, meta_tokens, attn_norm, w_in, conv_w, a_log, dt_bias, gdn_norm, q_norm, k_norm,
           lambda_q1, lambda_k1, lambda_q2, lambda_k2, diff_norm, w_out, ffn_norm, w_gate_up, w_down):
    assert x.shape[0] == 1 and x.shape[2] == D_MODEL
    seq = x.shape[1]
    assert seq % META_BLOCK == 0
    xs = x[0]
    meta_block = jnp.concatenate([jnp.zeros((META_BLOCK - N_META, D_MODEL), xs.dtype),
                                  meta_tokens.astype(xs.dtype)], axis=0)

    gdn_cols = 4 * GDN_WIDTH
    wt_in = w_in[0].T
    wt_diff = wt_in[gdn_cols + 2 * GDN_HEADS:]
    wt_ba = jnp.pad(wt_in[gdn_cols:gdn_cols + 2 * GDN_HEADS],
                    ((0, GATE_LANES - 2 * GDN_HEADS), (0, 0))).astype(BF16)

    n1, gcol, grow = _prenorm_gate(xs, meta_block, attn_norm, wt_ba, _lane_pad(a_log[0], GDN_HEADS),
                                   _lane_pad(dt_bias[0], GDN_HEADS))
    proj = _in_proj(n1, wt_in, wt_diff, conv_w[0])

    cos, sin = _rope_tables(seq)
    tile2 = lambda g: jnp.tile(g.astype(F32), (1, LANES // DIFF_DIM))
    lane = np.arange(LANES)
    gsum = jnp.asarray((lane[:, None] // DIFF_DIM) == (lane[None, :] // DIFF_DIM), BF16)
    eye = jnp.asarray(lane[:, None] == lane[None, :], BF16)
    gdn_local_out, (q2, kr, vt) = _gdn_local_attn_prep(
        _gdn_local_specs(proj, gcol, grow),
        _attn_prep_specs(proj, cos, sin, tile2(q_norm), tile2(k_norm), gsum, eye))
    lam_params = jnp.concatenate([lambda_q1, lambda_k1, lambda_q2, lambda_k2], axis=0).astype(F32)
    mix_d, mix_g = _diff_attn_gdn_state(q2, kr, vt, lam_params, diff_norm.astype(F32).reshape(DIFF_VDIM, 1),
                                        gdn_local_out, proj, gdn_norm, seq)

    h2, n2 = _out_proj(mix_g, mix_d, w_out[0], xs, ffn_norm, seq)
    act = _gate_up(n2, w_gate_up[0])
    out = _down(act, w_down[0], h2)
    return out[None]
```

```python
import functools
import math

import jax
import jax.numpy as jnp
import numpy as np
from jax import lax
from jax.experimental import pallas as pl
from jax.experimental.pallas import tpu as pltpu

F32 = jnp.float32
BF16 = jnp.bfloat16

D_MODEL = 2048
N_META = 16
GDN_HEADS = 8
GDN_DIM = 128
GDN_WIDTH = GDN_HEADS * GDN_DIM
CONV_WIDTH = 4
CHUNK = 64
DIFF_HEADS = 8
DIFF_DIM = 64
DIFF_VDIM = 2 * DIFF_DIM
DIFF_WIDTH = DIFF_HEADS * DIFF_VDIM
ROPE_THETA = 10000.0
D_FF = 5632
NORM_EPS = 1e-6
MASK_VALUE = -1e30
LAMBDA_INIT = 0.8 - 0.6 * math.exp(-0.3 * 0)

LANES = 128
META_BLOCK = 512
GATE_LANES = 128
VMEM_LIMIT = 56 * 1024 * 1024


def _pick(n, candidates):
    for c in candidates:
        if n % c == 0:
            return c
    raise ValueError(f"no tile in {candidates} divides {n}")


def _params(sem, vmem=VMEM_LIMIT):
    return pltpu.CompilerParams(dimension_semantics=sem, vmem_limit_bytes=vmem)


def _dot(a, b):
    return jnp.dot(a, b, preferred_element_type=F32)


def _dot_nt(a, b):
    return lax.dot_general(a, b, (((1,), (1,)), ((), ())), preferred_element_type=F32)


def _dot_tn(a, b):
    return lax.dot_general(a, b, (((0,), (0,)), ((), ())), preferred_element_type=F32)


def _softplus(x):
    return jnp.maximum(x, 0.0) + jnp.log1p(jnp.exp(-jnp.abs(x)))


def _silu(x):
    return x * jax.nn.sigmoid(x)


def _silu_tanh(x):
    h = 0.5 * x
    return h + h * jnp.tanh(h)


def _prenorm_gate_kernel(x_ref, mb_ref, gain_ref, wba_ref, alog_ref, dtb_ref, n_ref, gcol_ref, grow_ref):
    h = jnp.where(pl.program_id(0) < pl.num_programs(0) - 1, x_ref[...], mb_ref[...])
    ms = jnp.mean(h * h, axis=-1, keepdims=True)
    n = (h * lax.rsqrt(ms + NORM_EPS) * gain_ref[...]).astype(BF16)
    n_ref[...] = n
    ba = _dot_nt(n, wba_ref[...])
    beta = jax.nn.sigmoid(ba)
    g = -jnp.exp(alog_ref[...]) * _softplus(ba + dtb_ref[...])
    row = lax.broadcasted_iota(jnp.int32, ba.shape, 0) % CHUNK
    gc = g
    for d in (1, 2, 4, 8, 16, 32):
        gc = gc + jnp.where(row >= d, pltpu.roll(gc, d, axis=0), 0.0)
    lane = lax.broadcasted_iota(jnp.int32, ba.shape, 1)
    out = jnp.where(lane < GDN_HEADS, beta, gc)
    gcol_ref[...] = out
    grow_ref[...] = out.T[: 2 * GDN_HEADS]


def _prenorm_gate(x, meta_block, gain, wba, alog, dtb):
    tm = META_BLOCK
    nx = x.shape[0] // tm
    rows = x.shape[0] + tm
    return pl.pallas_call(
        _prenorm_gate_kernel,
        grid=(nx + 1,),
        in_specs=[
            pl.BlockSpec((tm, D_MODEL), lambda i: (jnp.minimum(i, nx - 1), 0)),
            pl.BlockSpec((tm, D_MODEL), lambda i: (0, 0)),
            pl.BlockSpec((1, D_MODEL), lambda i: (0, 0)),
            pl.BlockSpec((GATE_LANES, D_MODEL), lambda i: (0, 0)),
            pl.BlockSpec((1, GATE_LANES), lambda i: (0, 0)),
            pl.BlockSpec((1, GATE_LANES), lambda i: (0, 0)),
        ],
        out_specs=[
            pl.BlockSpec((tm, D_MODEL), lambda i: (i, 0)),
            pl.BlockSpec((tm, GATE_LANES), lambda i: (i, 0)),
            pl.BlockSpec((2 * GDN_HEADS, tm), lambda i: (0, i)),
        ],
        out_shape=[
            jax.ShapeDtypeStruct((rows, D_MODEL), BF16),
            jax.ShapeDtypeStruct((rows, GATE_LANES), F32),
            jax.ShapeDtypeStruct((2 * GDN_HEADS, rows), F32),
        ],
        compiler_params=_params(("parallel",)),
        name="prenorm_gate",
    )(x, meta_block, gain, wba, alog, dtb)


IN_PROJ_TN = 1024
GDN_COL_TILES = 4 * GDN_WIDTH // IN_PROJ_TN


def _shift_rows(x, prev, d):
    r8 = lax.broadcasted_iota(jnp.int32, prev.shape, 0)
    shifted = pltpu.roll(x, d, axis=0)
    top = jnp.where(r8 < d, pltpu.roll(prev, d, axis=0), shifted[:8])
    return jnp.concatenate([top, shifted[8:]], axis=0)


def _causal_conv_silu(x, prev, w):
    assert CONV_WIDTH == 4
    w0, w1, w2, w3 = (w[t:t + 1] for t in range(CONV_WIDTH))
    x1 = _shift_rows(x, prev, 1)
    b = x * w1 + x1 * w0
    b_prev = prev * w1 + pltpu.roll(prev, 1, axis=0) * w0
    return _silu_tanh(x * w3 + x1 * w2 + _shift_rows(b, b_prev, 2))


def _in_proj_kernel(a_ref, wg_ref, wd_ref, cw_ref, o_ref, tail_ref, raw_ref, w16_ref):
    j = pl.program_id(0)
    i = pl.program_id(1)
    tm = a_ref.shape[0]

    @pl.when(i == 0)
    def _():
        tail_ref[...] = jnp.zeros_like(tail_ref)

    @pl.when((i == 0) & (j < GDN_COL_TILES))
    def _():
        w16_ref[...] = wg_ref[...].T.astype(BF16)

    @pl.when((i == 0) & (j >= GDN_COL_TILES))
    def _():
        w16_ref[...] = wd_ref[...].T.astype(BF16)

    def gdn_qkv(l2_scale):
        a = a_ref[...]
        for pair in range(GDN_HEADS // 2):
            cs = slice(pair * 2 * GDN_DIM, (pair + 1) * 2 * GDN_DIM)
            raw_ref[pair] = _dot(a, w16_ref[:, cs])
        for pair in range(GDN_HEADS // 2):
            cs = slice(pair * 2 * GDN_DIM, (pair + 1) * 2 * GDN_DIM)
            raw = raw_ref[pair]
            y = _causal_conv_silu(raw, tail_ref[:, cs], cw_ref[:, cs])
            tail_ref[:, cs] = raw[tm - 8:]
            for half in range(2):
                ls = slice(half * GDN_DIM, (half + 1) * GDN_DIM)
                yh = y[:, ls]
                if l2_scale is not None:
                    yh = yh * (lax.rsqrt(jnp.sum(yh * yh, axis=-1, keepdims=True) + NORM_EPS) * l2_scale)
                o_ref[:, pair * 2 * GDN_DIM + half * GDN_DIM:pair * 2 * GDN_DIM + (half + 1) * GDN_DIM] = yh

    pl.when(j == 0)(functools.partial(gdn_qkv, GDN_DIM ** -0.5))
    pl.when(j == 1)(functools.partial(gdn_qkv, 1.0))
    pl.when(j == 2)(functools.partial(gdn_qkv, None))

    @pl.when(j >= GDN_COL_TILES - 1)
    def _():
        o_ref[...] = _dot(a_ref[...], w16_ref[...])


def _in_proj(n1, wt_all, wt_diff, conv_w):
    m = n1.shape[0]
    tm, tn = META_BLOCK, IN_PROJ_TN
    nm = m // tm
    n = GDN_COL_TILES * tn + wt_diff.shape[0]
    seq_order = lambda i: (i + nm - 1) % nm
    return pl.pallas_call(
        _in_proj_kernel,
        grid=(n // tn, nm),
        in_specs=[
            pl.BlockSpec((tm, D_MODEL), lambda j, i: (seq_order(i), 0)),
            pl.BlockSpec((tn, D_MODEL), lambda j, i: (jnp.minimum(j, GDN_COL_TILES - 1), 0)),
            pl.BlockSpec((tn, D_MODEL), lambda j, i: (jnp.maximum(j - GDN_COL_TILES, 0), 0)),
            pl.BlockSpec((CONV_WIDTH, tn), lambda j, i: (0, jnp.minimum(j, 2))),
        ],
        out_specs=pl.BlockSpec((tm, tn), lambda j, i: (seq_order(i), j)),
        out_shape=jax.ShapeDtypeStruct((m, n), F32),
        scratch_shapes=[pltpu.VMEM((8, tn), F32), pltpu.VMEM((GDN_HEADS // 2, tm, 2 * GDN_DIM), F32),
                        pltpu.VMEM((D_MODEL, tn), BF16)],
        compiler_params=_params(("parallel", "arbitrary")),
        name="in_proj",
    )(n1, wt_all, wt_diff, conv_w)


GDN_LOCAL_CHUNKS = 4


def _gdn_local_kernel(q_ref, k_ref, v_ref, gcol_ref, grow_ref,
                      w_ref, u_ref, qg_ref, kd_ref, attn_ref, egl_ref):
    q_all = q_ref[...]
    k_all = k_ref[...]
    v_all = v_ref[...]
    gcol = gcol_ref[...]
    grow = grow_ref[...]
    ii = lax.broadcasted_iota(jnp.int32, (CHUNK, CHUNK), 0)
    jj = lax.broadcasted_iota(jnp.int32, (CHUNK, CHUNK), 1)
    units = [(c, h) for c in range(GDN_LOCAL_CHUNKS) for h in range(GDN_HEADS)]
    rows_of = lambda c: slice(c * CHUNK, (c + 1) * CHUNK)
    lanes_of = lambda h: slice(h * GDN_DIM, (h + 1) * GDN_DIM)
    qs, ks, kbs, vbs, egcs, gcs, kqs = [], [], [], [], [], [], []
    for c, h in units:
        rs, ls = rows_of(c), lanes_of(h)
        q = q_all[rs, ls]
        k = k_all[rs, ls]
        beta_c = gcol[rs, h:h + 1]
        gc_c = gcol[rs, GDN_HEADS + h:GDN_HEADS + h + 1]
        kb = k * beta_c
        qs.append(q)
        ks.append(k)
        kbs.append(kb)
        vbs.append(v_all[rs, ls] * beta_c)
        gcs.append(gc_c)
        egcs.append(jnp.exp(gc_c))
        kqs.append(_dot_nt(jnp.concatenate([kb.astype(BF16), q.astype(BF16)], axis=0), k.astype(BF16)))
    lms, attns = [], []
    for (c, h), kq, gc_c in zip(units, kqs, gcs):
        gc_r = grow[GDN_HEADS + h:GDN_HEADS + h + 1, rows_of(c)]
        decay = jnp.exp(jnp.where(ii >= jj, gc_c - gc_r, MASK_VALUE))
        lms.append(jnp.where(ii > jj, kq[:CHUNK] * decay, 0.0))
        attns.append(kq[CHUNK:] * decay)
    xor = ii ^ jj
    eye = jnp.where(ii == jj, 1.0, 0.0)
    xs = [eye - jnp.where(xor == 1, lm, 0.0) for lm in lms]
    level = 1
    while (2 << level) <= CHUNK:
        sel = (xor >> level) == 1
        ys = [_dot(jnp.where(sel, lm, 0.0).astype(BF16), x.astype(BF16)) for lm, x in zip(lms, xs)]
        xs = [x - _dot(x.astype(BF16), y.astype(BF16)) for x, y in zip(xs, ys)]
        level += 1
    uws = [_dot(x.astype(BF16), jnp.concatenate([vb.astype(BF16), (kb * egc).astype(BF16)], axis=1))
           for x, vb, kb, egc in zip(xs, vbs, kbs, egcs)]
    for (c, h), uw, q, k, egc, gc_c, attn in zip(units, uws, qs, ks, egcs, gcs, attns):
        rs, ls = rows_of(c), lanes_of(h)
        u_ref[rs, ls] = uw[:, :GDN_DIM]
        w_ref[rs, ls] = uw[:, GDN_DIM:].astype(BF16)
        qg_ref[rs, ls] = (q * egc).astype(BF16)
        gc_last = gc_c[CHUNK - 1:CHUNK]
        kd_ref[rs, ls] = (k * jnp.exp(gc_last - gc_c)).astype(BF16)
        attn_ref[h, rs, :] = attn.astype(BF16)
        egl_ref[c, h:h + 1, :] = jnp.broadcast_to(jnp.exp(gc_last), (1, GDN_DIM))


def _gdn_local_specs(proj, gcol, grow):
    rows = proj.shape[0]
    rb = GDN_LOCAL_CHUNKS * CHUNK
    blk = lambda col: pl.BlockSpec((rb, GDN_WIDTH), lambda i, col=col: (i, col))
    row_out = lambda dt: jax.ShapeDtypeStruct((rows, GDN_WIDTH), dt)
    in_specs = [
        blk(0), blk(1), blk(2),
        pl.BlockSpec((rb, GATE_LANES), lambda i: (i, 0)),
        pl.BlockSpec((2 * GDN_HEADS, rb), lambda i: (0, i)),
    ]
    out_specs = [
        blk(0), blk(0), blk(0), blk(0),
        pl.BlockSpec((GDN_HEADS, rb, CHUNK), lambda i: (0, i, 0)),
        pl.BlockSpec((GDN_LOCAL_CHUNKS, GDN_HEADS, GDN_DIM), lambda i: (i, 0, 0)),
    ]
    out_shape = [
        row_out(BF16),
        row_out(F32),
        row_out(BF16),
        row_out(BF16),
        jax.ShapeDtypeStruct((GDN_HEADS, rows, CHUNK), BF16),
        jax.ShapeDtypeStruct((rows // CHUNK, GDN_HEADS, GDN_DIM), F32),
    ]
    return (proj, proj, proj, gcol, grow), in_specs, out_specs, out_shape


GDN_STATE_CHUNKS = 2


def _gdn_state_chunks(chunks, w_ref, u_ref, qg_ref, kd_ref, attn_ref, egl_ref, s_ref,
                      z_ref=None, gain_ref=None, o_ref=None):
    heads = range(GDN_HEADS)
    lanes = [slice(h * GDN_DIM, (h + 1) * GDN_DIM) for h in heads]
    ss = [s_ref[h] for h in heads]
    for ck in chunks:
        rw = slice(ck * CHUNK, (ck + 1) * CHUNK)
        rs = [_dot(jnp.concatenate([w_ref[rw, ls], qg_ref[rw, ls]], axis=0), s.astype(BF16))
              for ls, s in zip(lanes, ss)]
        vns = [(u_ref[rw, ls] - r[:CHUNK]).astype(BF16) for ls, r in zip(lanes, rs)]
        if o_ref is not None:
            os_ = [r[CHUNK:] + _dot(attn_ref[h, rw, :], vn) for h, r, vn in zip(heads, rs, vns)]
        ss = [s * egl_ref[ck, h:h + 1, :] + _dot_tn(kd_ref[rw, ls], vn)
              for h, ls, s, vn in zip(heads, lanes, ss, vns)]
        if o_ref is not None:
            for ls, o in zip(lanes, os_):
                on = o * lax.rsqrt(jnp.mean(o * o, axis=-1, keepdims=True) + NORM_EPS) * gain_ref[...]
                o_ref[rw, ls] = (on * _silu(z_ref[rw, ls])).astype(BF16)
    for h, s in zip(heads, ss):
        s_ref[h] = s


KV_TILE = 128
VT_ROWS = DIFF_VDIM + 16
Q_SCALE = DIFF_DIM ** -0.5 * math.log2(math.e)


PREP_ROWS = 2 * KV_TILE


def _attn_prep_kernel(q_ref, k_ref, v_ref, cos_ref, sin_ref, qg_ref, kg_ref, gsum_ref, eye_ref,
                      q2_ref, kr_ref, vt_ref):
    cos = cos_ref[...]
    sin = sin_ref[...]
    gsum = gsum_ref[...]
    eye = eye_ref[...]
    lane = lax.broadcasted_iota(jnp.int32, cos.shape, 1)
    first_half = (lane % DIFF_DIM) < (DIFF_DIM // 2)
    low_map = lane < DIFF_DIM

    def norm_rope(x, gain):
        ms = _dot((x * x).astype(BF16), gsum) * (1.0 / DIFF_DIM)
        xn = x * lax.rsqrt(ms + NORM_EPS) * gain
        rot = jnp.where(first_half, pltpu.roll(xn, LANES - DIFF_DIM // 2, axis=1),
                        pltpu.roll(xn, DIFF_DIM // 2, axis=1))
        return xn * cos + rot * sin

    for h in range(DIFF_HEADS):
        ls = slice(h * DIFF_VDIM, (h + 1) * DIFF_VDIM)
        q = norm_rope(q_ref[:, ls], qg_ref[...]) * Q_SCALE
        q2_ref[0, :, ls] = jnp.where(low_map, q, 0.0).astype(BF16)
        q2_ref[1, :, ls] = jnp.where(low_map, 0.0, q).astype(BF16)
        kr_ref[:, ls] = norm_rope(k_ref[:, ls], kg_ref[...]).astype(BF16)
        for t in range(PREP_ROWS // KV_TILE):
            v = v_ref[t * KV_TILE:(t + 1) * KV_TILE, ls].astype(BF16)
            vt_ref[t, h * VT_ROWS:h * VT_ROWS + DIFF_VDIM, :] = _dot_nt(eye, v).astype(BF16)
            vt_ref[t, h * VT_ROWS + DIFF_VDIM:(h + 1) * VT_ROWS, :] = jnp.ones(
                (VT_ROWS - DIFF_VDIM, KV_TILE), BF16)


def _attn_prep_specs(proj, cos, sin, qgain, kgain, gsum, eye):
    rows = proj.shape[0]
    tm = PREP_ROWS
    col = lambda c: pl.BlockSpec((tm, DIFF_WIDTH), lambda i, c=c: (i, c))
    small = lambda shape: pl.BlockSpec(shape, lambda i: (0,) * len(shape))
    in_specs = [
        col(4), col(5), col(6),
        pl.BlockSpec((tm, LANES), lambda i: (i, 0)),
        pl.BlockSpec((tm, LANES), lambda i: (i, 0)),
        small((1, LANES)), small((1, LANES)), small((LANES, LANES)), small((LANES, LANES)),
    ]
    out_specs = [
        pl.BlockSpec((2, tm, DIFF_WIDTH), lambda i: (0, i, 0)),
        pl.BlockSpec((tm, DIFF_WIDTH), lambda i: (i, 0)),
        pl.BlockSpec((tm // KV_TILE, DIFF_HEADS * VT_ROWS, KV_TILE), lambda i: (i, 0, 0)),
    ]
    out_shape = [
        jax.ShapeDtypeStruct((2, rows, DIFF_WIDTH), BF16),
        jax.ShapeDtypeStruct((rows, DIFF_WIDTH), BF16),
        jax.ShapeDtypeStruct((rows // KV_TILE, DIFF_HEADS * VT_ROWS, KV_TILE), BF16),
    ]
    return (proj, proj, proj, cos, sin, qgain, kgain, gsum, eye), in_specs, out_specs, out_shape


def _gdn_local_attn_prep(gdn_part, prep_part):
    g_ops, g_in, g_out, g_shape = gdn_part
    p_ops, p_in, p_out, p_shape = prep_part
    assert GDN_LOCAL_CHUNKS * CHUNK == PREP_ROWS
    rows = g_ops[0].shape[0]

    def body(*refs):
        g_i, rest = refs[:len(g_in)], refs[len(g_in):]
        p_i, rest = rest[:len(p_in)], rest[len(p_in):]
        g_o, p_o = rest[:len(g_out)], rest[len(g_out):]
        _attn_prep_kernel(*p_i, *p_o)
        _gdn_local_kernel(*g_i, *g_o)

    outs = pl.pallas_call(
        body,
        grid=(rows // PREP_ROWS,),
        in_specs=g_in + p_in,
        out_specs=g_out + p_out,
        out_shape=g_shape + p_shape,
        compiler_params=_params(("parallel",)),
        name="gdn_local_attn_prep",
    )(*g_ops, *p_ops)
    return outs[:len(g_out)], outs[len(g_out):]


ATTN_BLOCK = 1024
ATTN_QSUB = 256

def _diff_attn_kernel(q_ref, k_ref, vt_ref, lam_ref, gain_ref,
                      gw_ref, gu_ref, gqg_ref, gkd_ref, gattn_ref, gegl_ref, gz_ref,
                      mw_ref, mu_ref, mqg_ref, mkd_ref, mattn_ref, megl_ref, ggain_ref,
                      o_ref, og_ref, acc_ref, st_ref, s_ref):
    i = pl.program_id(1)
    step = pl.program_id(0) * pl.num_programs(1) + i

    @pl.when(step == 0)
    def _():
        s_ref[...] = jnp.zeros_like(s_ref)
        _gdn_state_chunks(range(1), mw_ref, mu_ref, mqg_ref, mkd_ref, mattn_ref, megl_ref, s_ref)
    rows = k_ref.shape[0]
    bk = ATTN_QSUB
    nsub = q_ref.shape[1] // ATTN_QSUB
    chains = [(mp, sb) for mp in range(2) for sb in range(nsub)]
    every = list(range(len(chains)))
    qs = [q_ref[mp, sb * ATTN_QSUB:(sb + 1) * ATTN_QSUB, :] for mp, sb in chains]
    kv_tiles = bk // KV_TILE

    def update(sts, vt, ms, which):
        ms = list(ms)
        first = ms[which[0]] is None
        cms = [jnp.max(st, axis=0, keepdims=True) for st in sts]
        m_new = cms if first else [jnp.maximum(ms[c], cm) for c, cm in zip(which, cms)]
        ps = [jnp.exp2(st - mn).astype(BF16) for st, mn in zip(sts, m_new)]
        pvs = [_dot(vt, p) for p in ps]
        for n, c in enumerate(which):
            if first:
                acc_ref[c] = pvs[n]
            else:
                acc_ref[c] = jnp.exp2(ms[c] - m_new[n]) * acc_ref[c] + pvs[n]
            ms[c] = m_new[n]
        return ms

    def store_scores(j, slot, which):
        start = j * bk if isinstance(j, int) else pl.multiple_of(j * bk, bk)
        k_c = k_ref[pl.ds(start, bk), :]
        for c in which:
            st_ref[slot, c] = _dot_nt(k_c, qs[c])

    def values_t(j):
        return jnp.concatenate([vt_ref[j * kv_tiles + t] for t in range(kv_tiles)], axis=1)

    gdn_x = functools.partial(_gdn_state_chunks, w_ref=gw_ref, u_ref=gu_ref, qg_ref=gqg_ref, kd_ref=gkd_ref,
                              attn_ref=gattn_ref, egl_ref=gegl_ref, s_ref=s_ref, z_ref=gz_ref,
                              gain_ref=ggain_ref, o_ref=og_ref)
    gdn_x(range(GDN_STATE_CHUNKS // 2))

    k_meta = k_ref[rows - N_META:rows, :]
    sts = [_dot_nt(k_meta, q) for q in qs]
    store_scores(0, 0, every)
    vt_meta = vt_ref[rows // KV_TILE - 1][:, KV_TILE - N_META:]
    ms = update(sts, vt_meta, [None] * len(chains), every)

    def full_blocks(t, ms):
        for n in range(nsub):
            j = nsub * t + n
            store_scores(j + 1, (n + 1) % 2, every)
            ms = update([st_ref[n % 2, c] for c in every], values_t(j), ms, every)
        return tuple(ms)

    ms = lax.fori_loop(0, i, full_blocks, tuple(ms))
    tri = (lax.broadcasted_iota(jnp.int32, (bk, ATTN_QSUB), 0)
           <= lax.broadcasted_iota(jnp.int32, (bk, ATTN_QSUB), 1))
    for d in range(nsub):
        if d + 1 < nsub:
            store_scores(i * nsub + d + 1, (d + 1) % 2, [c for c in every if chains[c][1] > d])
        which = [c for c in every if chains[c][1] >= d]
        sts = [jnp.where(tri, st_ref[d % 2, c], MASK_VALUE) if chains[c][1] == d else st_ref[d % 2, c]
               for c in which]
        ms = update(sts, values_t(i * nsub + d), ms, which)

    gdn_x(range(GDN_STATE_CHUNKS // 2, GDN_STATE_CHUNKS))

    lp = lam_ref[...]
    lam = (jnp.exp(jnp.sum(lp[0:1] * lp[1:2], axis=-1, keepdims=True))
           - jnp.exp(jnp.sum(lp[2:3] * lp[3:4], axis=-1, keepdims=True)) + LAMBDA_INIT)
    gain = gain_ref[...]
    for sb in range(nsub):
        num1, num2 = acc_ref[sb, :DIFF_VDIM, :], acc_ref[nsub + sb, :DIFF_VDIM, :]
        l1 = acc_ref[sb, DIFF_VDIM:DIFF_VDIM + 1, :]
        l2 = acc_ref[nsub + sb, DIFF_VDIM:DIFF_VDIM + 1, :]
        ot = num1 * (1.0 / l1) - num2 * (lam / l2)
        ot = ot * lax.rsqrt(jnp.mean(ot * ot, axis=0, keepdims=True) + NORM_EPS) * gain
        o_ref[sb * ATTN_QSUB:(sb + 1) * ATTN_QSUB, :] = (ot * (1.0 - LAMBDA_INIT)).T.astype(BF16)


def _diff_attn_gdn_state(q2, kr, vt, lam_params, gain_col, gdn_local_out, proj, gdn_gain, seq):
    rows = kr.shape[0]
    bq = ATTN_BLOCK
    nq = seq // bq
    nchains = 2 * (bq // ATTN_QSUB)
    gb = GDN_STATE_CHUNKS * CHUNK
    last_chunk = rows // CHUNK - 1
    assert seq % bq == 0 and bq % (2 * ATTN_QSUB) == 0 and DIFF_HEADS * nq * gb == seq
    w, u, qg, kd, attn, egl = gdn_local_out
    step = lambda h, i: h * nq + i
    xrows = lambda col: pl.BlockSpec((gb, GDN_WIDTH), lambda h, i, col=col: (step(h, i), col))
    mrows = pl.BlockSpec((CHUNK, GDN_WIDTH), lambda h, i: (last_chunk, 0))
    return pl.pallas_call(
        _diff_attn_kernel,
        grid=(DIFF_HEADS, nq),
        in_specs=[
            pl.BlockSpec((2, bq, DIFF_VDIM), lambda h, i: (0, i, h)),
            pl.BlockSpec((rows, DIFF_VDIM), lambda h, i: (0, h)),
            pl.BlockSpec((rows // KV_TILE, VT_ROWS, KV_TILE), lambda h, i: (0, h, 0)),
            pl.BlockSpec((4, DIFF_DIM), lambda h, i: (0, 0)),
            pl.BlockSpec((DIFF_VDIM, 1), lambda h, i: (0, 0)),
            xrows(0), xrows(0), xrows(0), xrows(0),
            pl.BlockSpec((GDN_HEADS, gb, CHUNK), lambda h, i: (0, step(h, i), 0)),
            pl.BlockSpec((GDN_STATE_CHUNKS, GDN_HEADS, GDN_DIM), lambda h, i: (step(h, i), 0, 0)),
            xrows(3),
            mrows, mrows, mrows, mrows,
            pl.BlockSpec((GDN_HEADS, CHUNK, CHUNK), lambda h, i: (0, last_chunk, 0)),
            pl.BlockSpec((1, GDN_HEADS, GDN_DIM), lambda h, i: (last_chunk, 0, 0)),
            pl.BlockSpec((1, GDN_DIM), lambda h, i: (0, 0)),
        ],
        out_specs=[
            pl.BlockSpec((bq, DIFF_VDIM), lambda h, i: (i, h)),
            pl.BlockSpec((gb, GDN_WIDTH), lambda h, i: (step(h, i), 0)),
        ],
        out_shape=[
            jax.ShapeDtypeStruct((seq, DIFF_WIDTH), BF16),
            jax.ShapeDtypeStruct((seq, GDN_WIDTH), BF16),
        ],
        scratch_shapes=[pltpu.VMEM((nchains, VT_ROWS, ATTN_QSUB), F32),
                        pltpu.VMEM((2, nchains, ATTN_QSUB, ATTN_QSUB), F32),
                        pltpu.VMEM((GDN_HEADS, GDN_DIM, GDN_DIM), F32)],
        compiler_params=_params(("arbitrary", "arbitrary")),
        name="diff_attn_gdn_state",
    )(q2, kr, vt, lam_params, gain_col, w, u, qg, kd, attn, egl, proj, w, u, qg, kd, attn, egl, gdn_gain)


def _out_proj_kernel(mg_ref, md_ref, wg_ref, wd_ref, h_ref, gain_ref, h2_ref, n2_ref):
    h2 = (h_ref[...] + _dot(mg_ref[...], wg_ref[...].astype(BF16))
          + _dot(md_ref[...], wd_ref[...].astype(BF16)))
    h2_ref[...] = h2
    ms = jnp.mean(h2 * h2, axis=-1, keepdims=True)
    n2_ref[...] = (h2 * lax.rsqrt(ms + NORM_EPS) * gain_ref[...]).astype(BF16)


def _out_proj(mix_g, mix_d, w_out, h, gain, seq):
    tm = _pick(seq, (512, 128))
    return pl.pallas_call(
        _out_proj_kernel,
        grid=(seq // tm,),
        in_specs=[
            pl.BlockSpec((tm, GDN_WIDTH), lambda i: (i, 0)),
            pl.BlockSpec((tm, DIFF_WIDTH), lambda i: (i, 0)),
            pl.BlockSpec((GDN_WIDTH, D_MODEL), lambda i: (0, 0), pipeline_mode=pl.Buffered(1)),
            pl.BlockSpec((DIFF_WIDTH, D_MODEL), lambda i: (1, 0), pipeline_mode=pl.Buffered(1)),
            pl.BlockSpec((tm, D_MODEL), lambda i: (i, 0)),
            pl.BlockSpec((1, D_MODEL), lambda i: (0, 0)),
        ],
        out_specs=[
            pl.BlockSpec((tm, D_MODEL), lambda i: (i, 0)),
            pl.BlockSpec((tm, D_MODEL), lambda i: (i, 0)),
        ],
        out_shape=[
            jax.ShapeDtypeStruct((seq, D_MODEL), F32),
            jax.ShapeDtypeStruct((seq, D_MODEL), BF16),
        ],
        compiler_params=_params(("parallel",)),
        name="out_proj",
    )(mix_g, mix_d, w_out, w_out, h, gain)


def _gate_up_kernel(n_ref, wg_ref, wu_ref, a_ref):
    n = n_ref[...]
    half = wg_ref.shape[1] // 2
    for c in range(2):
        cs = slice(c * half, (c + 1) * half)
        g = _dot(n, wg_ref[:, cs].astype(BF16))
        u = _dot(n, wu_ref[:, cs].astype(BF16))
        a_ref[:, cs] = (_silu(g) * u).astype(BF16)


def _gate_up(n2, w_gu):
    seq = n2.shape[0]
    tm = _pick(seq, (1024, 128))
    tn = 512
    nt = D_FF // tn
    return pl.pallas_call(
        _gate_up_kernel,
        grid=(nt, seq // tm),
        in_specs=[
            pl.BlockSpec((tm, D_MODEL), lambda j, i: (i, 0)),
            pl.BlockSpec((D_MODEL, tn), lambda j, i: (0, j)),
            pl.BlockSpec((D_MODEL, tn), lambda j, i: (0, j + nt)),
        ],
        out_specs=pl.BlockSpec((tm, tn), lambda j, i: (i, j)),
        out_shape=jax.ShapeDtypeStruct((seq, D_FF), BF16),
        compiler_params=_params(("parallel", "parallel")),
        name="ffn_gate_up",
    )(n2, w_gu, w_gu)


def _down_kernel(a_ref, w_ref, h_ref, o_ref):
    o_ref[...] = h_ref[...] + _dot(a_ref[...], w_ref[...].astype(BF16))


def _down(act, w_down, h2):
    seq = act.shape[0]
    tm = _pick(seq, (512, 128))
    tn = 1024
    return pl.pallas_call(
        _down_kernel,
        grid=(D_MODEL // tn, seq // tm),
        in_specs=[
            pl.BlockSpec((tm, D_FF), lambda j, i: (i, 0)),
            pl.BlockSpec((D_FF, tn), lambda j, i: (0, j), pipeline_mode=pl.Buffered(1)),
            pl.BlockSpec((tm, tn), lambda j, i: (i, j)),
        ],
        out_specs=pl.BlockSpec((tm, tn), lambda j, i: (i, j)),
        out_shape=jax.ShapeDtypeStruct((seq, D_MODEL), F32),
        compiler_params=_params(("parallel", "parallel")),
        name="ffn_down",
    )(act, w_down, h2)


def _rope_tables(seq):
    half = DIFF_DIM // 2
    pos = jnp.concatenate([jnp.arange(seq) + N_META, jnp.zeros((META_BLOCK - N_META,), jnp.int32),
                           jnp.arange(N_META)]).astype(F32)
    inv_freq = ROPE_THETA ** (-jnp.arange(half, dtype=F32) / half)
    ang = pos[:, None] * inv_freq[None, :]
    cos = jnp.tile(jnp.cos(ang), (1, LANES // half))
    sin = jnp.sin(ang)
    sin = jnp.tile(jnp.concatenate([-sin, sin], axis=1), (1, LANES // DIFF_DIM))
    return cos, sin


def _lane_pad(v, offset):
    return jnp.zeros((1, GATE_LANES), F32).at[0, offset:offset + v.shape[0]].set(v.astype(F32))


def kernel(x, meta_tokens, attn_norm, w_in, conv_w, a_log, dt_bias, gdn_norm, q_norm, k_norm,
           lambda_q1, lambda_k1, lambda_q2, lambda_k2, diff_norm, w_out, ffn_norm, w_gate_up, w_down):
    assert x.shape[0] == 1 and x.shape[2] == D_MODEL
    seq = x.shape[1]
    assert seq % META_BLOCK == 0
    xs = x[0]
    meta_block = jnp.concatenate([jnp.zeros((META_BLOCK - N_META, D_MODEL), xs.dtype),
                                  meta_tokens.astype(xs.dtype)], axis=0)

    gdn_cols = 4 * GDN_WIDTH
    wt_in = w_in[0].T
    wt_diff = wt_in[gdn_cols + 2 * GDN_HEADS:]
    wt_ba = jnp.pad(wt_in[gdn_cols:gdn_cols + 2 * GDN_HEADS],
                    ((0, GATE_LANES - 2 * GDN_HEADS), (0, 0))).astype(BF16)

    n1, gcol, grow = _prenorm_gate(xs, meta_block, attn_norm, wt_ba, _lane_pad(a_log[0], GDN_HEADS),
                                   _lane_pad(dt_bias[0], GDN_HEADS))
    proj = _in_proj(n1, wt_in, wt_diff, conv_w[0])

    cos, sin = _rope_tables(seq)
    tile2 = lambda g: jnp.tile(g.astype(F32), (1, LANES // DIFF_DIM))
    lane = np.arange(LANES)
    gsum = jnp.asarray((lane[:, None] // DIFF_DIM) == (lane[None, :] // DIFF_DIM), BF16)
    eye = jnp.asarray(lane[:, None] == lane[None, :], BF16)
    gdn_local_out, (q2, kr, vt) = _gdn_local_attn_prep(
        _gdn_local_specs(proj, gcol, grow),
        _attn_prep_specs(proj, cos, sin, tile2(q_norm), tile2(k_norm), gsum, eye))
    lam_params = jnp.concatenate([lambda_q1, lambda_k1, lambda_q2, lambda_k2], axis=0).astype(F32)
    mix_d, mix_g = _diff_attn_gdn_state(q2, kr, vt, lam_params, diff_norm.astype(F32).reshape(DIFF_VDIM, 1),
                                        gdn_local_out, proj, gdn_norm, seq)

    h2, n2 = _out_proj(mix_g, mix_d, w_out[0], xs, ffn_norm, seq)
    act = _gate_up(n2, w_gate_up[0])
    out = _down(act, w_down[0], h2)
    return out[None]
```

```python
import functools
import math

import jax
import jax.numpy as jnp
import numpy as np
from jax import lax
from jax.experimental import pallas as pl
from jax.experimental.pallas import tpu as pltpu

F32 = jnp.float32
BF16 = jnp.bfloat16

D_MODEL = 2048
N_META = 16
GDN_HEADS = 8
GDN_DIM = 128
GDN_WIDTH = GDN_HEADS * GDN_DIM
CONV_WIDTH = 4
CHUNK = 64
DIFF_HEADS = 8
DIFF_DIM = 64
DIFF_VDIM = 2 * DIFF_DIM
DIFF_WIDTH = DIFF_HEADS * DIFF_VDIM
ROPE_THETA = 10000.0
D_FF = 5632
NORM_EPS = 1e-6
MASK_VALUE = -1e30
LAMBDA_INIT = 0.8 - 0.6 * math.exp(-0.3 * 0)

LANES = 128
META_BLOCK = 512
GATE_LANES = 128
VMEM_LIMIT = 56 * 1024 * 1024


def _pick(n, candidates):
    for c in candidates:
        if n % c == 0:
            return c
    raise ValueError(f"no tile in {candidates} divides {n}")


def _params(sem, vmem=VMEM_LIMIT):
    return pltpu.CompilerParams(dimension_semantics=sem, vmem_limit_bytes=vmem)


def _dot(a, b):
    return jnp.dot(a, b, preferred_element_type=F32)


def _dot_nt(a, b):
    return lax.dot_general(a, b, (((1,), (1,)), ((), ())), preferred_element_type=F32)


def _dot_tn(a, b):
    return lax.dot_general(a, b, (((0,), (0,)), ((), ())), preferred_element_type=F32)


def _softplus(x):
    return jnp.maximum(x, 0.0) + jnp.log1p(jnp.exp(-jnp.abs(x)))


def _silu(x):
    return x * jax.nn.sigmoid(x)


def _silu_tanh(x):
    h = 0.5 * x
    return h + h * jnp.tanh(h)


def _prenorm_gate_kernel(x_ref, mb_ref, gain_ref, wba_ref, alog_ref, dtb_ref, n_ref, gcol_ref, grow_ref):
    h = jnp.where(pl.program_id(0) < pl.num_programs(0) - 1, x_ref[...], mb_ref[...])
    ms = jnp.mean(h * h, axis=-1, keepdims=True)
    n = (h * lax.rsqrt(ms + NORM_EPS) * gain_ref[...]).astype(BF16)
    n_ref[...] = n
    ba = _dot_nt(n, wba_ref[...])
    beta = jax.nn.sigmoid(ba)
    g = -jnp.exp(alog_ref[...]) * _softplus(ba + dtb_ref[...])
    row = lax.broadcasted_iota(jnp.int32, ba.shape, 0) % CHUNK
    gc = g
    for d in (1, 2, 4, 8, 16, 32):
        gc = gc + jnp.where(row >= d, pltpu.roll(gc, d, axis=0), 0.0)
    lane = lax.broadcasted_iota(jnp.int32, ba.shape, 1)
    out = jnp.where(lane < GDN_HEADS, beta, gc)
    gcol_ref[...] = out
    grow_ref[...] = out.T[: 2 * GDN_HEADS]


def _prenorm_gate(x, meta_block, gain, wba, alog, dtb):
    tm = META_BLOCK
    nx = x.shape[0] // tm
    rows = x.shape[0] + tm
    return pl.pallas_call(
        _prenorm_gate_kernel,
        grid=(nx + 1,),
        in_specs=[
            pl.BlockSpec((tm, D_MODEL), lambda i: (jnp.minimum(i, nx - 1), 0)),
            pl.BlockSpec((tm, D_MODEL), lambda i: (0, 0)),
            pl.BlockSpec((1, D_MODEL), lambda i: (0, 0)),
            pl.BlockSpec((GATE_LANES, D_MODEL), lambda i: (0, 0)),
            pl.BlockSpec((1, GATE_LANES), lambda i: (0, 0)),
            pl.BlockSpec((1, GATE_LANES), lambda i: (0, 0)),
        ],
        out_specs=[
            pl.BlockSpec((tm, D_MODEL), lambda i: (i, 0)),
            pl.BlockSpec((tm, GATE_LANES), lambda i: (i, 0)),
            pl.BlockSpec((2 * GDN_HEADS, tm), lambda i: (0, i)),
        ],
        out_shape=[
            jax.ShapeDtypeStruct((rows, D_MODEL), BF16),
            jax.ShapeDtypeStruct((rows, GATE_LANES), F32),
            jax.ShapeDtypeStruct((2 * GDN_HEADS, rows), F32),
        ],
        compiler_params=_params(("parallel",)),
        name="prenorm_gate",
    )(x, meta_block, gain, wba, alog, dtb)


IN_PROJ_TN = 1024
GDN_COL_TILES = 4 * GDN_WIDTH // IN_PROJ_TN
DIFF_ROW0 = 4 * GDN_WIDTH + 2 * GDN_HEADS


def _shift_rows(x, prev, d):
    r8 = lax.broadcasted_iota(jnp.int32, prev.shape, 0)
    shifted = pltpu.roll(x, d, axis=0)
    top = jnp.where(r8 < d, pltpu.roll(prev, d, axis=0), shifted[:8])
    return jnp.concatenate([top, shifted[8:]], axis=0)


def _causal_conv_silu(x, prev, w):
    assert CONV_WIDTH == 4
    w0, w1, w2, w3 = (w[t:t + 1] for t in range(CONV_WIDTH))
    x1 = _shift_rows(x, prev, 1)
    b = x * w1 + x1 * w0
    b_prev = prev * w1 + pltpu.roll(prev, 1, axis=0) * w0
    return _silu_tanh(x * w3 + x1 * w2 + _shift_rows(b, b_prev, 2))


def _in_proj_kernel(a_ref, wg_ref, wt_hbm_ref, cw_ref, o_ref, tail_ref, raw_ref, w16_ref, wd_ref, wd_sem):
    j = pl.program_id(0)
    i = pl.program_id(1)
    tm = a_ref.shape[0]
    tn = w16_ref.shape[1]

    def diff_tile_copy(t):
        return pltpu.make_async_copy(wt_hbm_ref.at[pl.ds(DIFF_ROW0 + t * tn, tn), :], wd_ref, wd_sem)

    @pl.when(i == 0)
    def _():
        tail_ref[...] = jnp.zeros_like(tail_ref)

    @pl.when((i == 0) & (j < GDN_COL_TILES))
    def _():
        w16_ref[...] = wg_ref[...].T.astype(BF16)

    @pl.when((i == 0) & (j >= GDN_COL_TILES))
    def _():
        diff_tile_copy(j - GDN_COL_TILES).wait()
        w16_ref[...] = wd_ref[...].T.astype(BF16)

    @pl.when((i == pl.num_programs(1) - 1) & (j >= GDN_COL_TILES - 1) & (j < pl.num_programs(0) - 1))
    def _():
        diff_tile_copy(j + 1 - GDN_COL_TILES).start()

    def gdn_qkv(l2_scale):
        a = a_ref[...]
        for pair in range(GDN_HEADS // 2):
            cs = slice(pair * 2 * GDN_DIM, (pair + 1) * 2 * GDN_DIM)
            raw_ref[pair] = _dot(a, w16_ref[:, cs])
        for pair in range(GDN_HEADS // 2):
            cs = slice(pair * 2 * GDN_DIM, (pair + 1) * 2 * GDN_DIM)
            raw = raw_ref[pair]
            y = _causal_conv_silu(raw, tail_ref[:, cs], cw_ref[:, cs])
            tail_ref[:, cs] = raw[tm - 8:]
            for half in range(2):
                ls = slice(half * GDN_DIM, (half + 1) * GDN_DIM)
                yh = y[:, ls]
                if l2_scale is not None:
                    yh = yh * (lax.rsqrt(jnp.sum(yh * yh, axis=-1, keepdims=True) + NORM_EPS) * l2_scale)
                o_ref[:, pair * 2 * GDN_DIM + half * GDN_DIM:pair * 2 * GDN_DIM + (half + 1) * GDN_DIM] = yh

    pl.when(j == 0)(functools.partial(gdn_qkv, GDN_DIM ** -0.5))
    pl.when(j == 1)(functools.partial(gdn_qkv, 1.0))
    pl.when(j == 2)(functools.partial(gdn_qkv, None))

    @pl.when(j >= GDN_COL_TILES - 1)
    def _():
        o_ref[...] = _dot(a_ref[...], w16_ref[...])


def _in_proj(n1, wt_all, conv_w):
    m = n1.shape[0]
    tm, tn = META_BLOCK, IN_PROJ_TN
    nm = m // tm
    n = GDN_COL_TILES * tn + 3 * DIFF_WIDTH
    assert wt_all.shape[0] == DIFF_ROW0 + 3 * DIFF_WIDTH and (3 * DIFF_WIDTH) % tn == 0
    seq_order = lambda i: (i + nm - 1) % nm
    return pl.pallas_call(
        _in_proj_kernel,
        grid=(n // tn, nm),
        in_specs=[
            pl.BlockSpec((tm, D_MODEL), lambda j, i: (seq_order(i), 0)),
            pl.BlockSpec((tn, D_MODEL), lambda j, i: (jnp.minimum(j, GDN_COL_TILES - 1), 0)),
            pl.BlockSpec(memory_space=pl.ANY),
            pl.BlockSpec((CONV_WIDTH, tn), lambda j, i: (0, jnp.minimum(j, 2))),
        ],
        out_specs=pl.BlockSpec((tm, tn), lambda j, i: (seq_order(i), j)),
        out_shape=jax.ShapeDtypeStruct((m, n), F32),
        scratch_shapes=[pltpu.VMEM((8, tn), F32), pltpu.VMEM((GDN_HEADS // 2, tm, 2 * GDN_DIM), F32),
                        pltpu.VMEM((D_MODEL, tn), BF16), pltpu.VMEM((tn, D_MODEL), F32),
                        pltpu.SemaphoreType.DMA(())],
        compiler_params=_params(("arbitrary", "arbitrary")),
        name="in_proj",
    )(n1, wt_all, wt_all, conv_w)


GDN_LOCAL_CHUNKS = 4


def _gdn_local_kernel(q_ref, k_ref, v_ref, gcol_ref, grow_ref,
                      w_ref, u_ref, qg_ref, kd_ref, attn_ref, egl_ref):
    q_all = q_ref[...]
    k_all = k_ref[...]
    v_all = v_ref[...]
    gcol = gcol_ref[...]
    grow = grow_ref[...]
    ii = lax.broadcasted_iota(jnp.int32, (CHUNK, CHUNK), 0)
    jj = lax.broadcasted_iota(jnp.int32, (CHUNK, CHUNK), 1)
    units = [(c, h) for c in range(GDN_LOCAL_CHUNKS) for h in range(GDN_HEADS)]
    rows_of = lambda c: slice(c * CHUNK, (c + 1) * CHUNK)
    lanes_of = lambda h: slice(h * GDN_DIM, (h + 1) * GDN_DIM)
    qs, ks, kbs, vbs, egcs, gcs, kqs = [], [], [], [], [], [], []
    for c, h in units:
        rs, ls = rows_of(c), lanes_of(h)
        q = q_all[rs, ls]
        k = k_all[rs, ls]
        beta_c = gcol[rs, h:h + 1]
        gc_c = gcol[rs, GDN_HEADS + h:GDN_HEADS + h + 1]
        kb = k * beta_c
        qs.append(q)
        ks.append(k)
        kbs.append(kb)
        vbs.append(v_all[rs, ls] * beta_c)
        gcs.append(gc_c)
        egcs.append(jnp.exp(gc_c))
        kqs.append(_dot_nt(jnp.concatenate([kb.astype(BF16), q.astype(BF16)], axis=0), k.astype(BF16)))
    lms, attns = [], []
    for (c, h), kq, gc_c in zip(units, kqs, gcs):
        gc_r = grow[GDN_HEADS + h:GDN_HEADS + h + 1, rows_of(c)]
        decay = jnp.exp(jnp.where(ii >= jj, gc_c - gc_r, MASK_VALUE))
        lms.append(jnp.where(ii > jj, kq[:CHUNK] * decay, 0.0))
        attns.append(kq[CHUNK:] * decay)
    xor = ii ^ jj
    eye = jnp.where(ii == jj, 1.0, 0.0)
    xs = [eye - jnp.where(xor == 1, lm, 0.0) for lm in lms]
    level = 1
    while (2 << level) <= CHUNK:
        sel = (xor >> level) == 1
        ys = [_dot(jnp.where(sel, lm, 0.0).astype(BF16), x.astype(BF16)) for lm, x in zip(lms, xs)]
        xs = [x - _dot(x.astype(BF16), y.astype(BF16)) for x, y in zip(xs, ys)]
        level += 1
    uws = [_dot(x.astype(BF16), jnp.concatenate([vb.astype(BF16), (kb * egc).astype(BF16)], axis=1))
           for x, vb, kb, egc in zip(xs, vbs, kbs, egcs)]
    for (c, h), uw, q, k, egc, gc_c, attn in zip(units, uws, qs, ks, egcs, gcs, attns):
        rs, ls = rows_of(c), lanes_of(h)
        u_ref[rs, ls] = uw[:, :GDN_DIM]
        w_ref[rs, ls] = uw[:, GDN_DIM:].astype(BF16)
        qg_ref[rs, ls] = (q * egc).astype(BF16)
        gc_last = gc_c[CHUNK - 1:CHUNK]
        kd_ref[rs, ls] = (k * jnp.exp(gc_last - gc_c)).astype(BF16)
        attn_ref[h, rs, :] = attn.astype(BF16)
        egl_ref[c, h:h + 1, :] = jnp.broadcast_to(jnp.exp(gc_last), (1, GDN_DIM))


def _gdn_local_specs(proj, gcol, grow):
    rows = proj.shape[0]
    rb = GDN_LOCAL_CHUNKS * CHUNK
    blk = lambda col: pl.BlockSpec((rb, GDN_WIDTH), lambda i, col=col: (i, col))
    row_out = lambda dt: jax.ShapeDtypeStruct((rows, GDN_WIDTH), dt)
    in_specs = [
        blk(0), blk(1), blk(2),
        pl.BlockSpec((rb, GATE_LANES), lambda i: (i, 0)),
        pl.BlockSpec((2 * GDN_HEADS, rb), lambda i: (0, i)),
    ]
    out_specs = [
        blk(0), blk(0), blk(0), blk(0),
        pl.BlockSpec((GDN_HEADS, rb, CHUNK), lambda i: (0, i, 0)),
        pl.BlockSpec((GDN_LOCAL_CHUNKS, GDN_HEADS, GDN_DIM), lambda i: (i, 0, 0)),
    ]
    out_shape = [
        row_out(BF16),
        row_out(F32),
        row_out(BF16),
        row_out(BF16),
        jax.ShapeDtypeStruct((GDN_HEADS, rows, CHUNK), BF16),
        jax.ShapeDtypeStruct((rows // CHUNK, GDN_HEADS, GDN_DIM), F32),
    ]
    return (proj, proj, proj, gcol, grow), in_specs, out_specs, out_shape


GDN_STATE_CHUNKS = 2


def _gdn_state_chunks(chunks, w_ref, u_ref, qg_ref, kd_ref, attn_ref, egl_ref, s_ref,
                      z_ref=None, gain_ref=None, o_ref=None):
    heads = range(GDN_HEADS)
    lanes = [slice(h * GDN_DIM, (h + 1) * GDN_DIM) for h in heads]
    ss = [s_ref[h] for h in heads]
    for ck in chunks:
        rw = slice(ck * CHUNK, (ck + 1) * CHUNK)
        rs = [_dot(jnp.concatenate([w_ref[rw, ls], qg_ref[rw, ls]], axis=0), s.astype(BF16))
              for ls, s in zip(lanes, ss)]
        vns = [(u_ref[rw, ls] - r[:CHUNK]).astype(BF16) for ls, r in zip(lanes, rs)]
        if o_ref is not None:
            os_ = [r[CHUNK:] + _dot(attn_ref[h, rw, :], vn) for h, r, vn in zip(heads, rs, vns)]
        ss = [s * egl_ref[ck, h:h + 1, :] + _dot_tn(kd_ref[rw, ls], vn)
              for h, ls, s, vn in zip(heads, lanes, ss, vns)]
        if o_ref is not None:
            for ls, o in zip(lanes, os_):
                on = o * lax.rsqrt(jnp.mean(o * o, axis=-1, keepdims=True) + NORM_EPS) * gain_ref[...]
                o_ref[rw, ls] = (on * _silu(z_ref[rw, ls])).astype(BF16)
    for h, s in zip(heads, ss):
        s_ref[h] = s


KV_TILE = 128
VT_ROWS = DIFF_VDIM + 16
Q_SCALE = DIFF_DIM ** -0.5 * math.log2(math.e)


PREP_ROWS = 2 * KV_TILE


def _attn_prep_kernel(q_ref, k_ref, v_ref, cos_ref, sin_ref, qg_ref, kg_ref, gsum_ref, eye_ref,
                      q2_ref, kr_ref, vt_ref):
    cos = cos_ref[...]
    sin = sin_ref[...]
    gsum = gsum_ref[...]
    eye = eye_ref[...]
    lane = lax.broadcasted_iota(jnp.int32, cos.shape, 1)
    first_half = (lane % DIFF_DIM) < (DIFF_DIM // 2)
    low_map = lane < DIFF_DIM

    def norm_rope(x, gain):
        ms = _dot((x * x).astype(BF16), gsum) * (1.0 / DIFF_DIM)
        xn = x * lax.rsqrt(ms + NORM_EPS) * gain
        rot = jnp.where(first_half, pltpu.roll(xn, LANES - DIFF_DIM // 2, axis=1),
                        pltpu.roll(xn, DIFF_DIM // 2, axis=1))
        return xn * cos + rot * sin

    for h in range(DIFF_HEADS):
        ls = slice(h * DIFF_VDIM, (h + 1) * DIFF_VDIM)
        q = norm_rope(q_ref[:, ls], qg_ref[...]) * Q_SCALE
        q2_ref[0, :, ls] = jnp.where(low_map, q, 0.0).astype(BF16)
        q2_ref[1, :, ls] = jnp.where(low_map, 0.0, q).astype(BF16)
        kr_ref[:, ls] = norm_rope(k_ref[:, ls], kg_ref[...]).astype(BF16)
        for t in range(PREP_ROWS // KV_TILE):
            v = v_ref[t * KV_TILE:(t + 1) * KV_TILE, ls].astype(BF16)
            vt_ref[t, h * VT_ROWS:h * VT_ROWS + DIFF_VDIM, :] = _dot_nt(eye, v).astype(BF16)
            vt_ref[t, h * VT_ROWS + DIFF_VDIM:(h + 1) * VT_ROWS, :] = jnp.ones(
                (VT_ROWS - DIFF_VDIM, KV_TILE), BF16)


def _attn_prep_specs(proj, cos, sin, qgain, kgain, gsum, eye):
    rows = proj.shape[0]
    tm = PREP_ROWS
    col = lambda c: pl.BlockSpec((tm, DIFF_WIDTH), lambda i, c=c: (i, c))
    small = lambda shape: pl.BlockSpec(shape, lambda i: (0,) * len(shape))
    in_specs = [
        col(4), col(5), col(6),
        pl.BlockSpec((tm, LANES), lambda i: (i, 0)),
        pl.BlockSpec((tm, LANES), lambda i: (i, 0)),
        small((1, LANES)), small((1, LANES)), small((LANES, LANES)), small((LANES, LANES)),
    ]
    out_specs = [
        pl.BlockSpec((2, tm, DIFF_WIDTH), lambda i: (0, i, 0)),
        pl.BlockSpec((tm, DIFF_WIDTH), lambda i: (i, 0)),
        pl.BlockSpec((tm // KV_TILE, DIFF_HEADS * VT_ROWS, KV_TILE), lambda i: (i, 0, 0)),
    ]
    out_shape = [
        jax.ShapeDtypeStruct((2, rows, DIFF_WIDTH), BF16),
        jax.ShapeDtypeStruct((rows, DIFF_WIDTH), BF16),
        jax.ShapeDtypeStruct((rows // KV_TILE, DIFF_HEADS * VT_ROWS, KV_TILE), BF16),
    ]
    return (proj, proj, proj, cos, sin, qgain, kgain, gsum, eye), in_specs, out_specs, out_shape


def _gdn_local_attn_prep(gdn_part, prep_part):
    g_ops, g_in, g_out, g_shape = gdn_part
    p_ops, p_in, p_out, p_shape = prep_part
    assert GDN_LOCAL_CHUNKS * CHUNK == PREP_ROWS
    rows = g_ops[0].shape[0]

    def body(*refs):
        g_i, rest = refs[:len(g_in)], refs[len(g_in):]
        p_i, rest = rest[:len(p_in)], rest[len(p_in):]
        g_o, p_o = rest[:len(g_out)], rest[len(g_out):]
        _attn_prep_kernel(*p_i, *p_o)
        _gdn_local_kernel(*g_i, *g_o)

    outs = pl.pallas_call(
        body,
        grid=(rows // PREP_ROWS,),
        in_specs=g_in + p_in,
        out_specs=g_out + p_out,
        out_shape=g_shape + p_shape,
        compiler_params=_params(("parallel",)),
        name="gdn_local_attn_prep",
    )(*g_ops, *p_ops)
    return outs[:len(g_out)], outs[len(g_out):]


ATTN_BLOCK = 1024
ATTN_QSUB = 256

def _diff_attn_kernel(q_ref, k_ref, vt_ref, lam_ref, gain_ref,
                      gw_ref, gu_ref, gqg_ref, gkd_ref, gattn_ref, gegl_ref, gz_ref,
                      mw_ref, mu_ref, mqg_ref, mkd_ref, mattn_ref, megl_ref, ggain_ref,
                      o_ref, og_ref, acc_ref, st_ref, s_ref):
    i = pl.program_id(1)
    step = pl.program_id(0) * pl.num_programs(1) + i

    @pl.when(step == 0)
    def _():
        s_ref[...] = jnp.zeros_like(s_ref)
        _gdn_state_chunks(range(1), mw_ref, mu_ref, mqg_ref, mkd_ref, mattn_ref, megl_ref, s_ref)
    rows = k_ref.shape[0]
    bk = ATTN_QSUB
    nsub = q_ref.shape[1] // ATTN_QSUB
    chains = [(mp, sb) for mp in range(2) for sb in range(nsub)]
    every = list(range(len(chains)))
    qs = [q_ref[mp, sb * ATTN_QSUB:(sb + 1) * ATTN_QSUB, :] for mp, sb in chains]
    kv_tiles = bk // KV_TILE

    def update(sts, vt, ms, which):
        ms = list(ms)
        first = ms[which[0]] is None
        cms = [jnp.max(st, axis=0, keepdims=True) for st in sts]
        m_new = cms if first else [jnp.maximum(ms[c], cm) for c, cm in zip(which, cms)]
        ps = [jnp.exp2(st - mn).astype(BF16) for st, mn in zip(sts, m_new)]
        pvs = [_dot(vt, p) for p in ps]
        for n, c in enumerate(which):
            if first:
                acc_ref[c] = pvs[n]
            else:
                acc_ref[c] = jnp.exp2(ms[c] - m_new[n]) * acc_ref[c] + pvs[n]
            ms[c] = m_new[n]
        return ms

    def store_scores(j, slot, which):
        start = j * bk if isinstance(j, int) else pl.multiple_of(j * bk, bk)
        k_c = k_ref[pl.ds(start, bk), :]
        for c in which:
            st_ref[slot, c] = _dot_nt(k_c, qs[c])

    def values_t(j):
        return jnp.concatenate([vt_ref[j * kv_tiles + t] for t in range(kv_tiles)], axis=1)

    gdn_x = functools.partial(_gdn_state_chunks, w_ref=gw_ref, u_ref=gu_ref, qg_ref=gqg_ref, kd_ref=gkd_ref,
                              attn_ref=gattn_ref, egl_ref=gegl_ref, s_ref=s_ref, z_ref=gz_ref,
                              gain_ref=ggain_ref, o_ref=og_ref)
    gdn_x(range(GDN_STATE_CHUNKS // 2))

    k_meta = k_ref[rows - N_META:rows, :]
    sts = [_dot_nt(k_meta, q) for q in qs]
    store_scores(0, 0, every)
    vt_meta = vt_ref[rows // KV_TILE - 1][:, KV_TILE - N_META:]
    ms = update(sts, vt_meta, [None] * len(chains), every)

    def full_blocks(t, ms):
        for n in range(nsub):
            j = nsub * t + n
            store_scores(j + 1, (n + 1) % 2, every)
            ms = update([st_ref[n % 2, c] for c in every], values_t(j), ms, every)
        return tuple(ms)

    ms = lax.fori_loop(0, i, full_blocks, tuple(ms))
    tri = (lax.broadcasted_iota(jnp.int32, (bk, ATTN_QSUB), 0)
           <= lax.broadcasted_iota(jnp.int32, (bk, ATTN_QSUB), 1))
    for d in range(nsub):
        if d + 1 < nsub:
            store_scores(i * nsub + d + 1, (d + 1) % 2, [c for c in every if chains[c][1] > d])
        which = [c for c in every if chains[c][1] >= d]
        sts = [jnp.where(tri, st_ref[d % 2, c], MASK_VALUE) if chains[c][1] == d else st_ref[d % 2, c]
               for c in which]
        ms = update(sts, values_t(i * nsub + d), ms, which)

    gdn_x(range(GDN_STATE_CHUNKS // 2, GDN_STATE_CHUNKS))

    lp = lam_ref[...]
    lam = (jnp.exp(jnp.sum(lp[0:1] * lp[1:2], axis=-1, keepdims=True))
           - jnp.exp(jnp.sum(lp[2:3] * lp[3:4], axis=-1, keepdims=True)) + LAMBDA_INIT)
    gain = gain_ref[...]
    for sb in range(nsub):
        num1, num2 = acc_ref[sb, :DIFF_VDIM, :], acc_ref[nsub + sb, :DIFF_VDIM, :]
        l1 = acc_ref[sb, DIFF_VDIM:DIFF_VDIM + 1, :]
        l2 = acc_ref[nsub + sb, DIFF_VDIM:DIFF_VDIM + 1, :]
        ot = num1 * (1.0 / l1) - num2 * (lam / l2)
        ot = ot * lax.rsqrt(jnp.mean(ot * ot, axis=0, keepdims=True) + NORM_EPS) * gain
        o_ref[sb * ATTN_QSUB:(sb + 1) * ATTN_QSUB, :] = (ot * (1.0 - LAMBDA_INIT)).T.astype(BF16)


def _diff_attn_gdn_state(q2, kr, vt, lam_params, gain_col, gdn_local_out, proj, gdn_gain, seq):
    rows = kr.shape[0]
    bq = ATTN_BLOCK
    nq = seq // bq
    nchains = 2 * (bq // ATTN_QSUB)
    gb = GDN_STATE_CHUNKS * CHUNK
    last_chunk = rows // CHUNK - 1
    assert seq % bq == 0 and bq % (2 * ATTN_QSUB) == 0 and DIFF_HEADS * nq * gb == seq
    w, u, qg, kd, attn, egl = gdn_local_out
    step = lambda h, i: h * nq + i
    xrows = lambda col: pl.BlockSpec((gb, GDN_WIDTH), lambda h, i, col=col: (step(h, i), col))
    mrows = pl.BlockSpec((CHUNK, GDN_WIDTH), lambda h, i: (last_chunk, 0))
    return pl.pallas_call(
        _diff_attn_kernel,
        grid=(DIFF_HEADS, nq),
        in_specs=[
            pl.BlockSpec((2, bq, DIFF_VDIM), lambda h, i: (0, i, h)),
            pl.BlockSpec((rows, DIFF_VDIM), lambda h, i: (0, h)),
            pl.BlockSpec((rows // KV_TILE, VT_ROWS, KV_TILE), lambda h, i: (0, h, 0)),
            pl.BlockSpec((4, DIFF_DIM), lambda h, i: (0, 0)),
            pl.BlockSpec((DIFF_VDIM, 1), lambda h, i: (0, 0)),
            xrows(0), xrows(0), xrows(0), xrows(0),
            pl.BlockSpec((GDN_HEADS, gb, CHUNK), lambda h, i: (0, step(h, i), 0)),
            pl.BlockSpec((GDN_STATE_CHUNKS, GDN_HEADS, GDN_DIM), lambda h, i: (step(h, i), 0, 0)),
            xrows(3),
            mrows, mrows, mrows, mrows,
            pl.BlockSpec((GDN_HEADS, CHUNK, CHUNK), lambda h, i: (0, last_chunk, 0)),
            pl.BlockSpec((1, GDN_HEADS, GDN_DIM), lambda h, i: (last_chunk, 0, 0)),
            pl.BlockSpec((1, GDN_DIM), lambda h, i: (0, 0)),
        ],
        out_specs=[
            pl.BlockSpec((bq, DIFF_VDIM), lambda h, i: (i, h)),
            pl.BlockSpec((gb, GDN_WIDTH), lambda h, i: (step(h, i), 0)),
        ],
        out_shape=[
            jax.ShapeDtypeStruct((seq, DIFF_WIDTH), BF16),
            jax.ShapeDtypeStruct((seq, GDN_WIDTH), BF16),
        ],
        scratch_shapes=[pltpu.VMEM((nchains, VT_ROWS, ATTN_QSUB), F32),
                        pltpu.VMEM((2, nchains, ATTN_QSUB, ATTN_QSUB), F32),
                        pltpu.VMEM((GDN_HEADS, GDN_DIM, GDN_DIM), F32)],
        compiler_params=_params(("arbitrary", "arbitrary")),
        name="diff_attn_gdn_state",
    )(q2, kr, vt, lam_params, gain_col, w, u, qg, kd, attn, egl, proj, w, u, qg, kd, attn, egl, gdn_gain)


def _out_proj_kernel(mg_ref, md_ref, wg_ref, wd_ref, h_ref, gain_ref, h2_ref, n2_ref):
    h2 = (h_ref[...] + _dot(mg_ref[...], wg_ref[...].astype(BF16))
          + _dot(md_ref[...], wd_ref[...].astype(BF16)))
    h2_ref[...] = h2
    ms = jnp.mean(h2 * h2, axis=-1, keepdims=True)
    n2_ref[...] = (h2 * lax.rsqrt(ms + NORM_EPS) * gain_ref[...]).astype(BF16)


def _out_proj(mix_g, mix_d, w_out, h, gain, seq):
    tm = _pick(seq, (512, 128))
    return pl.pallas_call(
        _out_proj_kernel,
        grid=(seq // tm,),
        in_specs=[
            pl.BlockSpec((tm, GDN_WIDTH), lambda i: (i, 0)),
            pl.BlockSpec((tm, DIFF_WIDTH), lambda i: (i, 0)),
            pl.BlockSpec((GDN_WIDTH, D_MODEL), lambda i: (0, 0), pipeline_mode=pl.Buffered(1)),
            pl.BlockSpec((DIFF_WIDTH, D_MODEL), lambda i: (1, 0), pipeline_mode=pl.Buffered(1)),
            pl.BlockSpec((tm, D_MODEL), lambda i: (i, 0)),
            pl.BlockSpec((1, D_MODEL), lambda i: (0, 0)),
        ],
        out_specs=[
            pl.BlockSpec((tm, D_MODEL), lambda i: (i, 0)),
            pl.BlockSpec((tm, D_MODEL), lambda i: (i, 0)),
        ],
        out_shape=[
            jax.ShapeDtypeStruct((seq, D_MODEL), F32),
            jax.ShapeDtypeStruct((seq, D_MODEL), BF16),
        ],
        compiler_params=_params(("parallel",)),
        name="out_proj",
    )(mix_g, mix_d, w_out, w_out, h, gain)


def _gate_up_kernel(n_ref, wg_ref, wu_ref, wd_ref, a_ref, wd16_ref):
    n = n_ref[...]
    half = wg_ref.shape[1] // 2
    for c in range(2):
        cs = slice(c * half, (c + 1) * half)
        g = _dot(n, wg_ref[:, cs].astype(BF16))
        u = _dot(n, wu_ref[:, cs].astype(BF16))
        a_ref[:, cs] = (_silu(g) * u).astype(BF16)
    wd16_ref[...] = wd_ref[...].astype(BF16)


def _gate_up(n2, w_gu, w_down):
    seq = n2.shape[0]
    tm = _pick(seq, (1024, 128))
    tn = 512
    nt, nm = D_FF // tn, seq // tm
    wd_rows = D_FF // (nt * nm)
    assert wd_rows * nt * nm == D_FF and wd_rows % 16 == 0
    return pl.pallas_call(
        _gate_up_kernel,
        grid=(nt, nm),
        in_specs=[
            pl.BlockSpec((tm, D_MODEL), lambda j, i: (i, 0)),
            pl.BlockSpec((D_MODEL, tn), lambda j, i: (0, j)),
            pl.BlockSpec((D_MODEL, tn), lambda j, i: (0, j + nt)),
            pl.BlockSpec((wd_rows, D_MODEL), lambda j, i: (j * nm + i, 0)),
        ],
        out_specs=[
            pl.BlockSpec((tm, tn), lambda j, i: (i, j)),
            pl.BlockSpec((wd_rows, D_MODEL), lambda j, i: (j * nm + i, 0)),
        ],
        out_shape=[
            jax.ShapeDtypeStruct((seq, D_FF), BF16),
            jax.ShapeDtypeStruct((D_FF, D_MODEL), BF16),
        ],
        compiler_params=_params(("parallel", "parallel")),
        name="ffn_gate_up",
    )(n2, w_gu, w_gu, w_down)


def _down_kernel(a_ref, w_ref, h_ref, o_ref):
    o_ref[...] = h_ref[...] + _dot(a_ref[...], w_ref[...])


def _down(act, w_down16, h2):
    seq = act.shape[0]
    tm = _pick(seq, (512, 128))
    tn = 1024
    return pl.pallas_call(
        _down_kernel,
        grid=(D_MODEL // tn, seq // tm),
        in_specs=[
            pl.BlockSpec((tm, D_FF), lambda j, i: (i, 0)),
            pl.BlockSpec((D_FF, tn), lambda j, i: (0, j)),
            pl.BlockSpec((tm, tn), lambda j, i: (i, j)),
        ],
        out_specs=pl.BlockSpec((tm, tn), lambda j, i: (i, j)),
        out_shape=jax.ShapeDtypeStruct((seq, D_MODEL), F32),
        compiler_params=_params(("parallel", "parallel")),
        name="ffn_down",
    )(act, w_down16, h2)


def _rope_tables(seq):
    half = DIFF_DIM // 2
    pos = jnp.concatenate([jnp.arange(seq) + N_META, jnp.zeros((META_BLOCK - N_META,), jnp.int32),
                           jnp.arange(N_META)]).astype(F32)
    inv_freq = ROPE_THETA ** (-jnp.arange(half, dtype=F32) / half)
    ang = pos[:, None] * inv_freq[None, :]
    cos = jnp.tile(jnp.cos(ang), (1, LANES // half))
    sin = jnp.sin(ang)
    sin = jnp.tile(jnp.concatenate([-sin, sin], axis=1), (1, LANES // DIFF_DIM))
    return cos, sin


def _lane_pad(v, offset):
    return jnp.zeros((1, GATE_LANES), F32).at[0, offset:offset + v.shape[0]].set(v.astype(F32))


def kernel(x, meta_tokens, attn_norm, w_in, conv_w, a_log, dt_bias, gdn_norm, q_norm, k_norm,
           lambda_q1, lambda_k1, lambda_q2, lambda_k2, diff_norm, w_out, ffn_norm, w_gate_up, w_down):
    assert x.shape[0] == 1 and x.shape[2] == D_MODEL
    seq = x.shape[1]
    assert seq % META_BLOCK == 0
    xs = x[0]
    meta_block = jnp.concatenate([jnp.zeros((META_BLOCK - N_META, D_MODEL), xs.dtype),
                                  meta_tokens.astype(xs.dtype)], axis=0)

    wt_in = w_in[0].T
    wt_ba = jnp.pad(wt_in[4 * GDN_WIDTH:DIFF_ROW0], ((0, GATE_LANES - 2 * GDN_HEADS), (0, 0))).astype(BF16)

    n1, gcol, grow = _prenorm_gate(xs, meta_block, attn_norm, wt_ba, _lane_pad(a_log[0], GDN_HEADS),
                                   _lane_pad(dt_bias[0], GDN_HEADS))
    proj = _in_proj(n1, wt_in, conv_w[0])

    cos, sin = _rope_tables(seq)
    tile2 = lambda g: jnp.tile(g.astype(F32), (1, LANES // DIFF_DIM))
    lane = np.arange(LANES)
    gsum = jnp.asarray((lane[:, None] // DIFF_DIM) == (lane[None, :] // DIFF_DIM), BF16)
    eye = jnp.asarray(lane[:, None] == lane[None, :], BF16)
    gdn_local_out, (q2, kr, vt) = _gdn_local_attn_prep(
        _gdn_local_specs(proj, gcol, grow),
        _attn_prep_specs(proj, cos, sin, tile2(q_norm), tile2(k_norm), gsum, eye))
    lam_params = jnp.concatenate([lambda_q1, lambda_k1, lambda_q2, lambda_k2], axis=0).astype(F32)
    mix_d, mix_g = _diff_attn_gdn_state(q2, kr, vt, lam_params, diff_norm.astype(F32).reshape(DIFF_VDIM, 1),
                                        gdn_local_out, proj, gdn_norm, seq)

    h2, n2 = _out_proj(mix_g, mix_d, w_out[0], xs, ffn_norm, seq)
    act, w_down16 = _gate_up(n2, w_gate_up[0], w_down[0])
    out = _down(act, w_down16, h2)
    return out[None]
```

```python
import functools
import math

import jax
import jax.numpy as jnp
import numpy as np
from jax import lax
from jax.experimental import pallas as pl
from jax.experimental.pallas import tpu as pltpu

F32 = jnp.float32
BF16 = jnp.bfloat16

D_MODEL = 2048
N_META = 16
GDN_HEADS = 8
GDN_DIM = 128
GDN_WIDTH = GDN_HEADS * GDN_DIM
CONV_WIDTH = 4
CHUNK = 64
DIFF_HEADS = 8
DIFF_DIM = 64
DIFF_VDIM = 2 * DIFF_DIM
DIFF_WIDTH = DIFF_HEADS * DIFF_VDIM
ROPE_THETA = 10000.0
D_FF = 5632
NORM_EPS = 1e-6
MASK_VALUE = -1e30
LAMBDA_INIT = 0.8 - 0.6 * math.exp(-0.3 * 0)

LANES = 128
META_BLOCK = 512
GATE_LANES = 128
VMEM_LIMIT = 56 * 1024 * 1024


def _pick(n, candidates):
    for c in candidates:
        if n % c == 0:
            return c
    raise ValueError(f"no tile in {candidates} divides {n}")


def _params(sem, vmem=VMEM_LIMIT):
    return pltpu.CompilerParams(dimension_semantics=sem, vmem_limit_bytes=vmem)


def _dot(a, b):
    return jnp.dot(a, b, preferred_element_type=F32)


def _dot_nt(a, b):
    return lax.dot_general(a, b, (((1,), (1,)), ((), ())), preferred_element_type=F32)


def _dot_tn(a, b):
    return lax.dot_general(a, b, (((0,), (0,)), ((), ())), preferred_element_type=F32)


def _softplus(x):
    return jnp.maximum(x, 0.0) + jnp.log1p(jnp.exp(-jnp.abs(x)))


def _silu(x):
    return x * jax.nn.sigmoid(x)


def _silu_tanh(x):
    h = 0.5 * x
    return h + h * jnp.tanh(h)


def _prenorm_gate_kernel(x_ref, mb_ref, gain_ref, wba_ref, alog_ref, dtb_ref, n_ref, gcol_ref, grow_ref):
    h = jnp.where(pl.program_id(0) < pl.num_programs(0) - 1, x_ref[...], mb_ref[...])
    ms = jnp.mean(h * h, axis=-1, keepdims=True)
    n = (h * lax.rsqrt(ms + NORM_EPS) * gain_ref[...]).astype(BF16)
    n_ref[...] = n
    ba = _dot_nt(n, wba_ref[...])
    beta = jax.nn.sigmoid(ba)
    g = -jnp.exp(alog_ref[...]) * _softplus(ba + dtb_ref[...])
    row = lax.broadcasted_iota(jnp.int32, ba.shape, 0) % CHUNK
    gc = g
    for d in (1, 2, 4, 8, 16, 32):
        gc = gc + jnp.where(row >= d, pltpu.roll(gc, d, axis=0), 0.0)
    lane = lax.broadcasted_iota(jnp.int32, ba.shape, 1)
    out = jnp.where(lane < GDN_HEADS, beta, gc)
    gcol_ref[...] = out
    grow_ref[...] = out.T[: 2 * GDN_HEADS]


def _prenorm_gate(x, meta_block, gain, wba, alog, dtb):
    tm = META_BLOCK
    nx = x.shape[0] // tm
    rows = x.shape[0] + tm
    return pl.pallas_call(
        _prenorm_gate_kernel,
        grid=(nx + 1,),
        in_specs=[
            pl.BlockSpec((tm, D_MODEL), lambda i: (jnp.minimum(i, nx - 1), 0)),
            pl.BlockSpec((tm, D_MODEL), lambda i: (0, 0)),
            pl.BlockSpec((1, D_MODEL), lambda i: (0, 0)),
            pl.BlockSpec((GATE_LANES, D_MODEL), lambda i: (0, 0)),
            pl.BlockSpec((1, GATE_LANES), lambda i: (0, 0)),
            pl.BlockSpec((1, GATE_LANES), lambda i: (0, 0)),
        ],
        out_specs=[
            pl.BlockSpec((tm, D_MODEL), lambda i: (i, 0)),
            pl.BlockSpec((tm, GATE_LANES), lambda i: (i, 0)),
            pl.BlockSpec((2 * GDN_HEADS, tm), lambda i: (0, i)),
        ],
        out_shape=[
            jax.ShapeDtypeStruct((rows, D_MODEL), BF16),
            jax.ShapeDtypeStruct((rows, GATE_LANES), F32),
            jax.ShapeDtypeStruct((2 * GDN_HEADS, rows), F32),
        ],
        compiler_params=_params(("parallel",)),
        name="prenorm_gate",
    )(x, meta_block, gain, wba, alog, dtb)


IN_PROJ_TN = 1024
GDN_COL_TILES = 4 * GDN_WIDTH // IN_PROJ_TN
DIFF_ROW0 = 4 * GDN_WIDTH + 2 * GDN_HEADS


def _shift_rows(x, prev, d):
    r8 = lax.broadcasted_iota(jnp.int32, prev.shape, 0)
    shifted = pltpu.roll(x, d, axis=0)
    top = jnp.where(r8 < d, pltpu.roll(prev, d, axis=0), shifted[:8])
    return jnp.concatenate([top, shifted[8:]], axis=0)


def _causal_conv_silu(x, prev, w):
    assert CONV_WIDTH == 4
    w0, w1, w2, w3 = (w[t:t + 1] for t in range(CONV_WIDTH))
    x1 = _shift_rows(x, prev, 1)
    b = x * w1 + x1 * w0
    b_prev = prev * w1 + pltpu.roll(prev, 1, axis=0) * w0
    return _silu_tanh(x * w3 + x1 * w2 + _shift_rows(b, b_prev, 2))


def _in_proj_kernel(a_ref, wg_ref, wt_hbm_ref, cw_ref, o_ref, tail_ref, raw_ref, w16_ref, wd_ref, wd_sem):
    j = pl.program_id(0)
    i = pl.program_id(1)
    tm = a_ref.shape[0]
    tn = w16_ref.shape[1]

    def diff_tile_copy(t):
        return pltpu.make_async_copy(wt_hbm_ref.at[pl.ds(DIFF_ROW0 + t * tn, tn), :], wd_ref, wd_sem)

    @pl.when(i == 0)
    def _():
        tail_ref[...] = jnp.zeros_like(tail_ref)

    @pl.when((i == 0) & (j < GDN_COL_TILES))
    def _():
        w16_ref[...] = wg_ref[...].T.astype(BF16)

    @pl.when((i == 0) & (j >= GDN_COL_TILES))
    def _():
        diff_tile_copy(j - GDN_COL_TILES).wait()
        w16_ref[...] = wd_ref[...].T.astype(BF16)

    @pl.when((i == pl.num_programs(1) - 1) & (j >= GDN_COL_TILES - 1) & (j < pl.num_programs(0) - 1))
    def _():
        diff_tile_copy(j + 1 - GDN_COL_TILES).start()

    def gdn_qkv(l2_scale):
        a = a_ref[...]
        for pair in range(GDN_HEADS // 2):
            cs = slice(pair * 2 * GDN_DIM, (pair + 1) * 2 * GDN_DIM)
            raw_ref[pair] = _dot(a, w16_ref[:, cs])
        for pair in range(GDN_HEADS // 2):
            cs = slice(pair * 2 * GDN_DIM, (pair + 1) * 2 * GDN_DIM)
            raw = raw_ref[pair]
            y = _causal_conv_silu(raw, tail_ref[:, cs], cw_ref[:, cs])
            tail_ref[:, cs] = raw[tm - 8:]
            for half in range(2):
                ls = slice(half * GDN_DIM, (half + 1) * GDN_DIM)
                yh = y[:, ls]
                if l2_scale is not None:
                    yh = yh * (lax.rsqrt(jnp.sum(yh * yh, axis=-1, keepdims=True) + NORM_EPS) * l2_scale)
                o_ref[:, pair * 2 * GDN_DIM + half * GDN_DIM:pair * 2 * GDN_DIM + (half + 1) * GDN_DIM] = yh

    pl.when(j == 0)(functools.partial(gdn_qkv, GDN_DIM ** -0.5))
    pl.when(j == 1)(functools.partial(gdn_qkv, 1.0))
    pl.when(j == 2)(functools.partial(gdn_qkv, None))

    @pl.when(j >= GDN_COL_TILES - 1)
    def _():
        o_ref[...] = _dot(a_ref[...], w16_ref[...])


def _in_proj(n1, wt_all, conv_w):
    m = n1.shape[0]
    tm, tn = META_BLOCK, IN_PROJ_TN
    nm = m // tm
    n = GDN_COL_TILES * tn + 3 * DIFF_WIDTH
    assert wt_all.shape[0] == DIFF_ROW0 + 3 * DIFF_WIDTH and (3 * DIFF_WIDTH) % tn == 0
    seq_order = lambda i: (i + nm - 1) % nm
    return pl.pallas_call(
        _in_proj_kernel,
        grid=(n // tn, nm),
        in_specs=[
            pl.BlockSpec((tm, D_MODEL), lambda j, i: (seq_order(i), 0)),
            pl.BlockSpec((tn, D_MODEL), lambda j, i: (jnp.minimum(j, GDN_COL_TILES - 1), 0)),
            pl.BlockSpec(memory_space=pl.ANY),
            pl.BlockSpec((CONV_WIDTH, tn), lambda j, i: (0, jnp.minimum(j, 2))),
        ],
        out_specs=pl.BlockSpec((tm, tn), lambda j, i: (seq_order(i), j)),
        out_shape=jax.ShapeDtypeStruct((m, n), F32),
        scratch_shapes=[pltpu.VMEM((8, tn), F32), pltpu.VMEM((GDN_HEADS // 2, tm, 2 * GDN_DIM), F32),
                        pltpu.VMEM((D_MODEL, tn), BF16), pltpu.VMEM((tn, D_MODEL), F32),
                        pltpu.SemaphoreType.DMA(())],
        compiler_params=_params(("arbitrary", "arbitrary")),
        name="in_proj",
    )(n1, wt_all, wt_all, conv_w)


GDN_LOCAL_CHUNKS = 4


def _gdn_local_kernel(q_ref, k_ref, v_ref, gcol_ref, grow_ref,
                      w_ref, u_ref, qg_ref, kd_ref, attn_ref, egl_ref):
    q_all = q_ref[...]
    k_all = k_ref[...]
    v_all = v_ref[...]
    gcol = gcol_ref[...]
    grow = grow_ref[...]
    ii = lax.broadcasted_iota(jnp.int32, (CHUNK, CHUNK), 0)
    jj = lax.broadcasted_iota(jnp.int32, (CHUNK, CHUNK), 1)
    units = [(c, h) for c in range(GDN_LOCAL_CHUNKS) for h in range(GDN_HEADS)]
    rows_of = lambda c: slice(c * CHUNK, (c + 1) * CHUNK)
    lanes_of = lambda h: slice(h * GDN_DIM, (h + 1) * GDN_DIM)
    qs, ks, kbs, vbs, egcs, gcs, kqs = [], [], [], [], [], [], []
    for c, h in units:
        rs, ls = rows_of(c), lanes_of(h)
        q = q_all[rs, ls]
        k = k_all[rs, ls]
        beta_c = gcol[rs, h:h + 1]
        gc_c = gcol[rs, GDN_HEADS + h:GDN_HEADS + h + 1]
        kb = k * beta_c
        qs.append(q)
        ks.append(k)
        kbs.append(kb)
        vbs.append(v_all[rs, ls] * beta_c)
        gcs.append(gc_c)
        egcs.append(jnp.exp(gc_c))
        kqs.append(_dot_nt(jnp.concatenate([kb.astype(BF16), q.astype(BF16)], axis=0), k.astype(BF16)))
    lms, attns = [], []
    for (c, h), kq, gc_c in zip(units, kqs, gcs):
        gc_r = grow[GDN_HEADS + h:GDN_HEADS + h + 1, rows_of(c)]
        decay = jnp.exp(jnp.where(ii >= jj, gc_c - gc_r, MASK_VALUE))
        lms.append(jnp.where(ii > jj, kq[:CHUNK] * decay, 0.0))
        attns.append(kq[CHUNK:] * decay)
    xor = ii ^ jj
    eye = jnp.where(ii == jj, 1.0, 0.0)
    xs = [eye - jnp.where(xor == 1, lm, 0.0) for lm in lms]
    level = 1
    while (2 << level) <= CHUNK:
        sel = (xor >> level) == 1
        ys = [_dot(jnp.where(sel, lm, 0.0).astype(BF16), x.astype(BF16)) for lm, x in zip(lms, xs)]
        xs = [x - _dot(x.astype(BF16), y.astype(BF16)) for x, y in zip(xs, ys)]
        level += 1
    uws = [_dot(x.astype(BF16), jnp.concatenate([vb.astype(BF16), (kb * egc).astype(BF16)], axis=1))
           for x, vb, kb, egc in zip(xs, vbs, kbs, egcs)]
    for (c, h), uw, q, k, egc, gc_c, attn in zip(units, uws, qs, ks, egcs, gcs, attns):
        rs, ls = rows_of(c), lanes_of(h)
        u_ref[rs, ls] = uw[:, :GDN_DIM]
        w_ref[rs, ls] = uw[:, GDN_DIM:].astype(BF16)
        qg_ref[rs, ls] = (q * egc).astype(BF16)
        gc_last = gc_c[CHUNK - 1:CHUNK]
        kd_ref[rs, ls] = (k * jnp.exp(gc_last - gc_c)).astype(BF16)
        attn_ref[h, rs, :] = attn.astype(BF16)
        egl_ref[c, h:h + 1, :] = jnp.broadcast_to(jnp.exp(gc_last), (1, GDN_DIM))


def _gdn_local_specs(proj, gcol, grow):
    rows = proj.shape[0]
    rb = GDN_LOCAL_CHUNKS * CHUNK
    blk = lambda col: pl.BlockSpec((rb, GDN_WIDTH), lambda i, col=col: (i, col))
    row_out = lambda dt: jax.ShapeDtypeStruct((rows, GDN_WIDTH), dt)
    in_specs = [
        blk(0), blk(1), blk(2),
        pl.BlockSpec((rb, GATE_LANES), lambda i: (i, 0)),
        pl.BlockSpec((2 * GDN_HEADS, rb), lambda i: (0, i)),
    ]
    out_specs = [
        blk(0), blk(0), blk(0), blk(0),
        pl.BlockSpec((GDN_HEADS, rb, CHUNK), lambda i: (0, i, 0)),
        pl.BlockSpec((GDN_LOCAL_CHUNKS, GDN_HEADS, GDN_DIM), lambda i: (i, 0, 0)),
    ]
    out_shape = [
        row_out(BF16),
        row_out(F32),
        row_out(BF16),
        row_out(BF16),
        jax.ShapeDtypeStruct((GDN_HEADS, rows, CHUNK), BF16),
        jax.ShapeDtypeStruct((rows // CHUNK, GDN_HEADS, GDN_DIM), F32),
    ]
    return (proj, proj, proj, gcol, grow), in_specs, out_specs, out_shape


GDN_STATE_CHUNKS = 2


def _gdn_state_chunks(chunks, w_ref, u_ref, qg_ref, kd_ref, attn_ref, egl_ref, s_ref,
                      z_ref=None, gain_ref=None, o_ref=None):
    heads = range(GDN_HEADS)
    lanes = [slice(h * GDN_DIM, (h + 1) * GDN_DIM) for h in heads]
    ss = [s_ref[h] for h in heads]
    for ck in chunks:
        rw = slice(ck * CHUNK, (ck + 1) * CHUNK)
        rs = [_dot(jnp.concatenate([w_ref[rw, ls], qg_ref[rw, ls]], axis=0), s.astype(BF16))
              for ls, s in zip(lanes, ss)]
        vns = [(u_ref[rw, ls] - r[:CHUNK]).astype(BF16) for ls, r in zip(lanes, rs)]
        if o_ref is not None:
            os_ = [r[CHUNK:] + _dot(attn_ref[h, rw, :], vn) for h, r, vn in zip(heads, rs, vns)]
        ss = [s * egl_ref[ck, h:h + 1, :] + _dot_tn(kd_ref[rw, ls], vn)
              for h, ls, s, vn in zip(heads, lanes, ss, vns)]
        if o_ref is not None:
            for ls, o in zip(lanes, os_):
                on = o * lax.rsqrt(jnp.mean(o * o, axis=-1, keepdims=True) + NORM_EPS) * gain_ref[...]
                o_ref[rw, ls] = (on * _silu(z_ref[rw, ls])).astype(BF16)
    for h, s in zip(heads, ss):
        s_ref[h] = s


KV_TILE = 128
VT_ROWS = DIFF_VDIM + 16
Q_SCALE = DIFF_DIM ** -0.5 * math.log2(math.e)


PREP_ROWS = 2 * KV_TILE


def _attn_prep_kernel(q_ref, k_ref, v_ref, cos_ref, sin_ref, qg_ref, kg_ref, gsum_ref, eye_ref,
                      q2_ref, kr_ref, vt_ref):
    cos = cos_ref[...]
    sin = sin_ref[...]
    gsum = gsum_ref[...]
    eye = eye_ref[...]
    lane = lax.broadcasted_iota(jnp.int32, cos.shape, 1)
    first_half = (lane % DIFF_DIM) < (DIFF_DIM // 2)
    low_map = lane < DIFF_DIM

    def norm_rope(x, gain):
        ms = _dot((x * x).astype(BF16), gsum) * (1.0 / DIFF_DIM)
        xn = x * lax.rsqrt(ms + NORM_EPS) * gain
        rot = jnp.where(first_half, pltpu.roll(xn, LANES - DIFF_DIM // 2, axis=1),
                        pltpu.roll(xn, DIFF_DIM // 2, axis=1))
        return xn * cos + rot * sin

    for h in range(DIFF_HEADS):
        ls = slice(h * DIFF_VDIM, (h + 1) * DIFF_VDIM)
        q = norm_rope(q_ref[:, ls], qg_ref[...]) * Q_SCALE
        q2_ref[0, :, ls] = jnp.where(low_map, q, 0.0).astype(BF16)
        q2_ref[1, :, ls] = jnp.where(low_map, 0.0, q).astype(BF16)
        kr_ref[:, ls] = norm_rope(k_ref[:, ls], kg_ref[...]).astype(BF16)
        for t in range(PREP_ROWS // KV_TILE):
            v = v_ref[t * KV_TILE:(t + 1) * KV_TILE, ls].astype(BF16)
            vt_ref[t, h * VT_ROWS:h * VT_ROWS + DIFF_VDIM, :] = _dot_nt(eye, v).astype(BF16)
            vt_ref[t, h * VT_ROWS + DIFF_VDIM:(h + 1) * VT_ROWS, :] = jnp.ones(
                (VT_ROWS - DIFF_VDIM, KV_TILE), BF16)


def _attn_prep_specs(proj, cos, sin, qgain, kgain, gsum, eye):
    rows = proj.shape[0]
    tm = PREP_ROWS
    col = lambda c: pl.BlockSpec((tm, DIFF_WIDTH), lambda i, c=c: (i, c))
    small = lambda shape: pl.BlockSpec(shape, lambda i: (0,) * len(shape))
    in_specs = [
        col(4), col(5), col(6),
        pl.BlockSpec((tm, LANES), lambda i: (i, 0)),
        pl.BlockSpec((tm, LANES), lambda i: (i, 0)),
        small((1, LANES)), small((1, LANES)), small((LANES, LANES)), small((LANES, LANES)),
    ]
    out_specs = [
        pl.BlockSpec((2, tm, DIFF_WIDTH), lambda i: (0, i, 0)),
        pl.BlockSpec((tm, DIFF_WIDTH), lambda i: (i, 0)),
        pl.BlockSpec((tm // KV_TILE, DIFF_HEADS * VT_ROWS, KV_TILE), lambda i: (i, 0, 0)),
    ]
    out_shape = [
        jax.ShapeDtypeStruct((2, rows, DIFF_WIDTH), BF16),
        jax.ShapeDtypeStruct((rows, DIFF_WIDTH), BF16),
        jax.ShapeDtypeStruct((rows // KV_TILE, DIFF_HEADS * VT_ROWS, KV_TILE), BF16),
    ]
    return (proj, proj, proj, cos, sin, qgain, kgain, gsum, eye), in_specs, out_specs, out_shape


def _gdn_local_attn_prep(gdn_part, prep_part):
    g_ops, g_in, g_out, g_shape = gdn_part
    p_ops, p_in, p_out, p_shape = prep_part
    assert GDN_LOCAL_CHUNKS * CHUNK == PREP_ROWS
    rows = g_ops[0].shape[0]

    def body(*refs):
        g_i, rest = refs[:len(g_in)], refs[len(g_in):]
        p_i, rest = rest[:len(p_in)], rest[len(p_in):]
        g_o, p_o = rest[:len(g_out)], rest[len(g_out):]
        _attn_prep_kernel(*p_i, *p_o)
        _gdn_local_kernel(*g_i, *g_o)

    outs = pl.pallas_call(
        body,
        grid=(rows // PREP_ROWS,),
        in_specs=g_in + p_in,
        out_specs=g_out + p_out,
        out_shape=g_shape + p_shape,
        compiler_params=_params(("parallel",)),
        name="gdn_local_attn_prep",
    )(*g_ops, *p_ops)
    return outs[:len(g_out)], outs[len(g_out):]


ATTN_BLOCK = 1024
ATTN_QSUB = 256

def _diff_attn_kernel(q_ref, k_ref, vt_ref, lam_ref, gain_ref,
                      gw_ref, gu_ref, gqg_ref, gkd_ref, gattn_ref, gegl_ref, gz_ref,
                      mw_ref, mu_ref, mqg_ref, mkd_ref, mattn_ref, megl_ref, ggain_ref,
                      o_ref, og_ref, acc_ref, st_ref, s_ref):
    i = pl.program_id(1)
    step = pl.program_id(0) * pl.num_programs(1) + i

    @pl.when(step == 0)
    def _():
        s_ref[...] = jnp.zeros_like(s_ref)
        _gdn_state_chunks(range(1), mw_ref, mu_ref, mqg_ref, mkd_ref, mattn_ref, megl_ref, s_ref)
    rows = k_ref.shape[0]
    bk = ATTN_QSUB
    nsub = q_ref.shape[1] // ATTN_QSUB
    chains = [(mp, sb) for mp in range(2) for sb in range(nsub)]
    every = list(range(len(chains)))
    qs = [q_ref[mp, sb * ATTN_QSUB:(sb + 1) * ATTN_QSUB, :] for mp, sb in chains]
    kv_tiles = bk // KV_TILE

    def update(sts, vt, ms, which):
        ms = list(ms)
        first = ms[which[0]] is None
        cms = [jnp.max(st, axis=0, keepdims=True) for st in sts]
        m_new = cms if first else [jnp.maximum(ms[c], cm) for c, cm in zip(which, cms)]
        ps = [jnp.exp2(st - mn).astype(BF16) for st, mn in zip(sts, m_new)]
        pvs = [_dot(vt, p) for p in ps]
        for n, c in enumerate(which):
            if first:
                acc_ref[c] = pvs[n]
            else:
                acc_ref[c] = jnp.exp2(ms[c] - m_new[n]) * acc_ref[c] + pvs[n]
            ms[c] = m_new[n]
        return ms

    def store_scores(j, slot, which):
        start = j * bk if isinstance(j, int) else pl.multiple_of(j * bk, bk)
        k_c = k_ref[pl.ds(start, bk), :]
        for c in which:
            st_ref[slot, c] = _dot_nt(k_c, qs[c])

    def values_t(j):
        return jnp.concatenate([vt_ref[j * kv_tiles + t] for t in range(kv_tiles)], axis=1)

    gdn_x = functools.partial(_gdn_state_chunks, w_ref=gw_ref, u_ref=gu_ref, qg_ref=gqg_ref, kd_ref=gkd_ref,
                              attn_ref=gattn_ref, egl_ref=gegl_ref, s_ref=s_ref, z_ref=gz_ref,
                              gain_ref=ggain_ref, o_ref=og_ref)
    gdn_x(range(GDN_STATE_CHUNKS // 2))

    k_meta = k_ref[rows - N_META:rows, :]
    sts = [_dot_nt(k_meta, q) for q in qs]
    store_scores(0, 0, every)
    vt_meta = vt_ref[rows // KV_TILE - 1][:, KV_TILE - N_META:]
    ms = update(sts, vt_meta, [None] * len(chains), every)

    def full_blocks(t, ms):
        for n in range(nsub):
            j = nsub * t + n
            store_scores(j + 1, (n + 1) % 2, every)
            ms = update([st_ref[n % 2, c] for c in every], values_t(j), ms, every)
        return tuple(ms)

    ms = lax.fori_loop(0, i, full_blocks, tuple(ms))
    tri = (lax.broadcasted_iota(jnp.int32, (bk, ATTN_QSUB), 0)
           <= lax.broadcasted_iota(jnp.int32, (bk, ATTN_QSUB), 1))
    for d in range(nsub):
        if d + 1 < nsub:
            store_scores(i * nsub + d + 1, (d + 1) % 2, [c for c in every if chains[c][1] > d])
        which = [c for c in every if chains[c][1] >= d]
        sts = [jnp.where(tri, st_ref[d % 2, c], MASK_VALUE) if chains[c][1] == d else st_ref[d % 2, c]
               for c in which]
        ms = update(sts, values_t(i * nsub + d), ms, which)

    gdn_x(range(GDN_STATE_CHUNKS // 2, GDN_STATE_CHUNKS))

    lp = lam_ref[...]
    lam = (jnp.exp(jnp.sum(lp[0:1] * lp[1:2], axis=-1, keepdims=True))
           - jnp.exp(jnp.sum(lp[2:3] * lp[3:4], axis=-1, keepdims=True)) + LAMBDA_INIT)
    gain = gain_ref[...]
    for sb in range(nsub):
        num1, num2 = acc_ref[sb, :DIFF_VDIM, :], acc_ref[nsub + sb, :DIFF_VDIM, :]
        l1 = acc_ref[sb, DIFF_VDIM:DIFF_VDIM + 1, :]
        l2 = acc_ref[nsub + sb, DIFF_VDIM:DIFF_VDIM + 1, :]
        ot = num1 * (1.0 / l1) - num2 * (lam / l2)
        ot = ot * lax.rsqrt(jnp.mean(ot * ot, axis=0, keepdims=True) + NORM_EPS) * gain
        o_ref[sb * ATTN_QSUB:(sb + 1) * ATTN_QSUB, :] = (ot * (1.0 - LAMBDA_INIT)).T.astype(BF16)


def _diff_attn_gdn_state(q2, kr, vt, lam_params, gain_col, gdn_local_out, proj, gdn_gain, seq):
    rows = kr.shape[0]
    bq = ATTN_BLOCK
    nq = seq // bq
    nchains = 2 * (bq // ATTN_QSUB)
    gb = GDN_STATE_CHUNKS * CHUNK
    last_chunk = rows // CHUNK - 1
    assert seq % bq == 0 and bq % (2 * ATTN_QSUB) == 0 and DIFF_HEADS * nq * gb == seq
    w, u, qg, kd, attn, egl = gdn_local_out
    step = lambda h, i: h * nq + i
    xrows = lambda col: pl.BlockSpec((gb, GDN_WIDTH), lambda h, i, col=col: (step(h, i), col))
    mrows = pl.BlockSpec((CHUNK, GDN_WIDTH), lambda h, i: (last_chunk, 0))
    return pl.pallas_call(
        _diff_attn_kernel,
        grid=(DIFF_HEADS, nq),
        in_specs=[
            pl.BlockSpec((2, bq, DIFF_VDIM), lambda h, i: (0, i, h)),
            pl.BlockSpec((rows, DIFF_VDIM), lambda h, i: (0, h)),
            pl.BlockSpec((rows // KV_TILE, VT_ROWS, KV_TILE), lambda h, i: (0, h, 0)),
            pl.BlockSpec((4, DIFF_DIM), lambda h, i: (0, 0)),
            pl.BlockSpec((DIFF_VDIM, 1), lambda h, i: (0, 0)),
            xrows(0), xrows(0), xrows(0), xrows(0),
            pl.BlockSpec((GDN_HEADS, gb, CHUNK), lambda h, i: (0, step(h, i), 0)),
            pl.BlockSpec((GDN_STATE_CHUNKS, GDN_HEADS, GDN_DIM), lambda h, i: (step(h, i), 0, 0)),
            xrows(3),
            mrows, mrows, mrows, mrows,
            pl.BlockSpec((GDN_HEADS, CHUNK, CHUNK), lambda h, i: (0, last_chunk, 0)),
            pl.BlockSpec((1, GDN_HEADS, GDN_DIM), lambda h, i: (last_chunk, 0, 0)),
            pl.BlockSpec((1, GDN_DIM), lambda h, i: (0, 0)),
        ],
        out_specs=[
            pl.BlockSpec((bq, DIFF_VDIM), lambda h, i: (i, h)),
            pl.BlockSpec((gb, GDN_WIDTH), lambda h, i: (step(h, i), 0)),
        ],
        out_shape=[
            jax.ShapeDtypeStruct((seq, DIFF_WIDTH), BF16),
            jax.ShapeDtypeStruct((seq, GDN_WIDTH), BF16),
        ],
        scratch_shapes=[pltpu.VMEM((nchains, VT_ROWS, ATTN_QSUB), F32),
                        pltpu.VMEM((2, nchains, ATTN_QSUB, ATTN_QSUB), F32),
                        pltpu.VMEM((GDN_HEADS, GDN_DIM, GDN_DIM), F32)],
        compiler_params=_params(("arbitrary", "arbitrary")),
        name="diff_attn_gdn_state",
    )(q2, kr, vt, lam_params, gain_col, w, u, qg, kd, attn, egl, proj, w, u, qg, kd, attn, egl, gdn_gain)


def _out_proj_kernel(mg_ref, md_ref, wg_ref, wd_ref, h_ref, gain_ref, h2_ref, n2_ref):
    h2 = (h_ref[...] + _dot(mg_ref[...], wg_ref[...].astype(BF16))
          + _dot(md_ref[...], wd_ref[...].astype(BF16)))
    h2_ref[...] = h2
    ms = jnp.mean(h2 * h2, axis=-1, keepdims=True)
    n2_ref[...] = (h2 * lax.rsqrt(ms + NORM_EPS) * gain_ref[...]).astype(BF16)


def _out_proj(mix_g, mix_d, w_out, h, gain, seq):
    tm = _pick(seq, (512, 128))
    return pl.pallas_call(
        _out_proj_kernel,
        grid=(seq // tm,),
        in_specs=[
            pl.BlockSpec((tm, GDN_WIDTH), lambda i: (i, 0)),
            pl.BlockSpec((tm, DIFF_WIDTH), lambda i: (i, 0)),
            pl.BlockSpec((GDN_WIDTH, D_MODEL), lambda i: (0, 0), pipeline_mode=pl.Buffered(1)),
            pl.BlockSpec((DIFF_WIDTH, D_MODEL), lambda i: (1, 0), pipeline_mode=pl.Buffered(1)),
            pl.BlockSpec((tm, D_MODEL), lambda i: (i, 0)),
            pl.BlockSpec((1, D_MODEL), lambda i: (0, 0)),
        ],
        out_specs=[
            pl.BlockSpec((tm, D_MODEL), lambda i: (i, 0)),
            pl.BlockSpec((tm, D_MODEL), lambda i: (i, 0)),
        ],
        out_shape=[
            jax.ShapeDtypeStruct((seq, D_MODEL), F32),
            jax.ShapeDtypeStruct((seq, D_MODEL), BF16),
        ],
        compiler_params=_params(("parallel",)),
        name="out_proj",
    )(mix_g, mix_d, w_out, w_out, h, gain)


def _gate_up_kernel(n_ref, wg_ref, wu_ref, wd_ref, a_ref, wd16_ref):
    n = n_ref[...]
    half = wg_ref.shape[1] // 2
    for c in range(2):
        cs = slice(c * half, (c + 1) * half)
        g = _dot(n, wg_ref[:, cs].astype(BF16))
        u = _dot(n, wu_ref[:, cs].astype(BF16))
        a_ref[:, cs] = (_silu(g) * u).astype(BF16)
    wd16_ref[...] = wd_ref[...].astype(BF16)


def _gate_up(n2, w_gu, w_down):
    seq = n2.shape[0]
    tm = _pick(seq, (1024, 128))
    tn = 512
    nt, nm = D_FF // tn, seq // tm
    wd_rows = D_FF // (nt * nm)
    assert wd_rows * nt * nm == D_FF and wd_rows % 16 == 0
    return pl.pallas_call(
        _gate_up_kernel,
        grid=(nt, nm),
        in_specs=[
            pl.BlockSpec((tm, D_MODEL), lambda j, i: (i, 0)),
            pl.BlockSpec((D_MODEL, tn), lambda j, i: (0, j)),
            pl.BlockSpec((D_MODEL, tn), lambda j, i: (0, j + nt)),
            pl.BlockSpec((wd_rows, D_MODEL), lambda j, i: (j * nm + i, 0)),
        ],
        out_specs=[
            pl.BlockSpec((tm, tn), lambda j, i: (i, j)),
            pl.BlockSpec((wd_rows, D_MODEL), lambda j, i: (j * nm + i, 0)),
        ],
        out_shape=[
            jax.ShapeDtypeStruct((seq, D_FF), BF16),
            jax.ShapeDtypeStruct((D_FF, D_MODEL), BF16),
        ],
        compiler_params=_params(("parallel", "parallel")),
        name="ffn_gate_up",
    )(n2, w_gu, w_gu, w_down)


def _down_kernel(a_ref, w_ref, h_ref, o_ref):
    o_ref[...] = h_ref[...] + _dot(a_ref[...], w_ref[...])


def _down(act, w_down16, h2):
    seq = act.shape[0]
    tm = _pick(seq, (512, 128))
    tn = D_MODEL
    return pl.pallas_call(
        _down_kernel,
        grid=(D_MODEL // tn, seq // tm),
        in_specs=[
            pl.BlockSpec((tm, D_FF), lambda j, i: (i, 0)),
            pl.BlockSpec((D_FF, tn), lambda j, i: (0, j), pipeline_mode=pl.Buffered(1)),
            pl.BlockSpec((tm, tn), lambda j, i: (i, j)),
        ],
        out_specs=pl.BlockSpec((tm, tn), lambda j, i: (i, j)),
        out_shape=jax.ShapeDtypeStruct((seq, D_MODEL), F32),
        compiler_params=_params(("parallel", "parallel")),
        name="ffn_down",
    )(act, w_down16, h2)


def _rope_tables(seq):
    half = DIFF_DIM // 2
    pos = jnp.concatenate([jnp.arange(seq) + N_META, jnp.zeros((META_BLOCK - N_META,), jnp.int32),
                           jnp.arange(N_META)]).astype(F32)
    inv_freq = ROPE_THETA ** (-jnp.arange(half, dtype=F32) / half)
    ang = pos[:, None] * inv_freq[None, :]
    cos = jnp.tile(jnp.cos(ang), (1, LANES // half))
    sin = jnp.sin(ang)
    sin = jnp.tile(jnp.concatenate([-sin, sin], axis=1), (1, LANES // DIFF_DIM))
    return cos, sin


def _lane_pad(v, offset):
    return jnp.zeros((1, GATE_LANES), F32).at[0, offset:offset + v.shape[0]].set(v.astype(F32))


def kernel(x, meta_tokens, attn_norm, w_in, conv_w, a_log, dt_bias, gdn_norm, q_norm, k_norm,
           lambda_q1, lambda_k1, lambda_q2, lambda_k2, diff_norm, w_out, ffn_norm, w_gate_up, w_down):
    assert x.shape[0] == 1 and x.shape[2] == D_MODEL
    seq = x.shape[1]
    assert seq % META_BLOCK == 0
    xs = x[0]
    meta_block = jnp.concatenate([jnp.zeros((META_BLOCK - N_META, D_MODEL), xs.dtype),
                                  meta_tokens.astype(xs.dtype)], axis=0)

    wt_in = w_in[0].T
    wt_ba = jnp.pad(wt_in[4 * GDN_WIDTH:DIFF_ROW0], ((0, GATE_LANES - 2 * GDN_HEADS), (0, 0))).astype(BF16)

    n1, gcol, grow = _prenorm_gate(xs, meta_block, attn_norm, wt_ba, _lane_pad(a_log[0], GDN_HEADS),
                                   _lane_pad(dt_bias[0], GDN_HEADS))
    proj = _in_proj(n1, wt_in, conv_w[0])

    cos, sin = _rope_tables(seq)
    tile2 = lambda g: jnp.tile(g.astype(F32), (1, LANES // DIFF_DIM))
    lane = np.arange(LANES)
    gsum = jnp.asarray((lane[:, None] // DIFF_DIM) == (lane[None, :] // DIFF_DIM), BF16)
    eye = jnp.asarray(lane[:, None] == lane[None, :], BF16)
    gdn_local_out, (q2, kr, vt) = _gdn_local_attn_prep(
        _gdn_local_specs(proj, gcol, grow),
        _attn_prep_specs(proj, cos, sin, tile2(q_norm), tile2(k_norm), gsum, eye))
    lam_params = jnp.concatenate([lambda_q1, lambda_k1, lambda_q2, lambda_k2], axis=0).astype(F32)
    mix_d, mix_g = _diff_attn_gdn_state(q2, kr, vt, lam_params, diff_norm.astype(F32).reshape(DIFF_VDIM, 1),
                                        gdn_local_out, proj, gdn_norm, seq)

    h2, n2 = _out_proj(mix_g, mix_d, w_out[0], xs, ffn_norm, seq)
    act, w_down16 = _gate_up(n2, w_gate_up[0], w_down[0])
    out = _down(act, w_down16, h2)
    return out[None]
```

```python
import functools
import math

import jax
import jax.numpy as jnp
import numpy as np
from jax import lax
from jax.experimental import pallas as pl
from jax.experimental.pallas import tpu as pltpu

F32 = jnp.float32
BF16 = jnp.bfloat16

D_MODEL = 2048
N_META = 16
GDN_HEADS = 8
GDN_DIM = 128
GDN_WIDTH = GDN_HEADS * GDN_DIM
CONV_WIDTH = 4
CHUNK = 64
DIFF_HEADS = 8
DIFF_DIM = 64
DIFF_VDIM = 2 * DIFF_DIM
DIFF_WIDTH = DIFF_HEADS * DIFF_VDIM
ROPE_THETA = 10000.0
D_FF = 5632
NORM_EPS = 1e-6
MASK_VALUE = -1e30
LAMBDA_INIT = 0.8 - 0.6 * math.exp(-0.3 * 0)

LANES = 128
META_BLOCK = 512
GATE_LANES = 128
VMEM_LIMIT = 56 * 1024 * 1024

def _pick(n, candidates):
    for c in candidates:
        if n % c == 0:
            return c
    raise ValueError(f"no tile in {candidates} divides {n}")


def _params(sem, vmem=VMEM_LIMIT):
    return pltpu.CompilerParams(dimension_semantics=sem, vmem_limit_bytes=vmem)


def _dot(a, b):
    return jnp.dot(a, b, preferred_element_type=F32)


def _dot_nt(a, b):
    return lax.dot_general(a, b, (((1,), (1,)), ((), ())), preferred_element_type=F32)


def _dot_tn(a, b):
    return lax.dot_general(a, b, (((0,), (0,)), ((), ())), preferred_element_type=F32)


def _softplus(x):
    return jnp.maximum(x, 0.0) + jnp.log1p(jnp.exp(-jnp.abs(x)))


def _silu(x):
    return x * jax.nn.sigmoid(x)


def _silu_tanh(x):
    h = 0.5 * x
    return h + h * jnp.tanh(h)


def _prenorm_gate_kernel(x_ref, mb_ref, gain_ref, wba_ref, alog_ref, dtb_ref, n_ref, gcol_ref, grow_ref):
    h = jnp.where(pl.program_id(0) < pl.num_programs(0) - 1, x_ref[...], mb_ref[...])
    ms = jnp.mean(h * h, axis=-1, keepdims=True)
    n = (h * lax.rsqrt(ms + NORM_EPS) * gain_ref[...]).astype(BF16)
    n_ref[...] = n
    ba = _dot_nt(n, wba_ref[...])
    beta = jax.nn.sigmoid(ba)
    g = -jnp.exp(alog_ref[...]) * _softplus(ba + dtb_ref[...])
    row = lax.broadcasted_iota(jnp.int32, ba.shape, 0) % CHUNK
    gc = g
    for d in (1, 2, 4, 8, 16, 32):
        gc = gc + jnp.where(row >= d, pltpu.roll(gc, d, axis=0), 0.0)
    lane = lax.broadcasted_iota(jnp.int32, ba.shape, 1)
    out = jnp.where(lane < GDN_HEADS, beta, gc)
    gcol_ref[...] = out
    grow_ref[...] = out.T[: 2 * GDN_HEADS]


def _prenorm_gate(x, meta_block, gain, wba, alog, dtb):
    tm = META_BLOCK
    nx = x.shape[0] // tm
    rows = x.shape[0] + tm
    return pl.pallas_call(
        _prenorm_gate_kernel,
        grid=(nx + 1,),
        in_specs=[
            pl.BlockSpec((tm, D_MODEL), lambda i: (jnp.minimum(i, nx - 1), 0)),
            pl.BlockSpec((tm, D_MODEL), lambda i: (0, 0)),
            pl.BlockSpec((1, D_MODEL), lambda i: (0, 0)),
            pl.BlockSpec((GATE_LANES, D_MODEL), lambda i: (0, 0)),
            pl.BlockSpec((1, GATE_LANES), lambda i: (0, 0)),
            pl.BlockSpec((1, GATE_LANES), lambda i: (0, 0)),
        ],
        out_specs=[
            pl.BlockSpec((tm, D_MODEL), lambda i: (i, 0)),
            pl.BlockSpec((tm, GATE_LANES), lambda i: (i, 0)),
            pl.BlockSpec((2 * GDN_HEADS, tm), lambda i: (0, i)),
        ],
        out_shape=[
            jax.ShapeDtypeStruct((rows, D_MODEL), BF16),
            jax.ShapeDtypeStruct((rows, GATE_LANES), F32),
            jax.ShapeDtypeStruct((2 * GDN_HEADS, rows), F32),
        ],
        compiler_params=_params(("parallel",)),
        name="prenorm_gate",
    )(x, meta_block, gain, wba, alog, dtb)


IN_PROJ_TN = 1024
GDN_COL_TILES = 4 * GDN_WIDTH // IN_PROJ_TN
DIFF_ROW0 = 4 * GDN_WIDTH + 2 * GDN_HEADS


def _shift_rows(x, prev, d):
    r8 = lax.broadcasted_iota(jnp.int32, prev.shape, 0)
    shifted = pltpu.roll(x, d, axis=0)
    top = jnp.where(r8 < d, pltpu.roll(prev, d, axis=0), shifted[:8])
    return jnp.concatenate([top, shifted[8:]], axis=0)


def _causal_conv_silu(x, prev, w):
    assert CONV_WIDTH == 4
    w0, w1, w2, w3 = (w[t:t + 1] for t in range(CONV_WIDTH))
    x1 = _shift_rows(x, prev, 1)
    b = x * w1 + x1 * w0
    b_prev = prev * w1 + pltpu.roll(prev, 1, axis=0) * w0
    return _silu_tanh(x * w3 + x1 * w2 + _shift_rows(b, b_prev, 2))


def _in_proj_kernel(a_ref, wg_ref, wt_hbm_ref, cw_ref, o_ref, tail_ref, raw_ref, w16_ref, wd_ref, wd_sem):
    j = pl.program_id(0)
    i = pl.program_id(1)
    tm = a_ref.shape[0]
    tn = w16_ref.shape[1]

    def diff_tile_copy(t):
        return pltpu.make_async_copy(wt_hbm_ref.at[pl.ds(DIFF_ROW0 + t * tn, tn), :], wd_ref, wd_sem)

    @pl.when(i == 0)
    def _():
        tail_ref[...] = jnp.zeros_like(tail_ref)

    @pl.when((i == 0) & (j < GDN_COL_TILES))
    def _():
        w16_ref[...] = wg_ref[...].T.astype(BF16)

    @pl.when((i == 0) & (j >= GDN_COL_TILES))
    def _():
        diff_tile_copy(j - GDN_COL_TILES).wait()
        w16_ref[...] = wd_ref[...].T.astype(BF16)

    @pl.when((i == pl.num_programs(1) - 1) & (j >= GDN_COL_TILES - 1) & (j < pl.num_programs(0) - 1))
    def _():
        diff_tile_copy(j + 1 - GDN_COL_TILES).start()

    def gdn_qkv(l2_scale):
        a = a_ref[...]
        for pair in range(GDN_HEADS // 2):
            cs = slice(pair * 2 * GDN_DIM, (pair + 1) * 2 * GDN_DIM)
            raw_ref[pair] = _dot(a, w16_ref[:, cs])
        for pair in range(GDN_HEADS // 2):
            cs = slice(pair * 2 * GDN_DIM, (pair + 1) * 2 * GDN_DIM)
            raw = raw_ref[pair]
            y = _causal_conv_silu(raw, tail_ref[:, cs], cw_ref[:, cs])
            tail_ref[:, cs] = raw[tm - 8:]
            for half in range(2):
                ls = slice(half * GDN_DIM, (half + 1) * GDN_DIM)
                yh = y[:, ls]
                if l2_scale is not None:
                    yh = yh * (lax.rsqrt(jnp.sum(yh * yh, axis=-1, keepdims=True) + NORM_EPS) * l2_scale)
                o_ref[:, pair * 2 * GDN_DIM + half * GDN_DIM:pair * 2 * GDN_DIM + (half + 1) * GDN_DIM] = yh

    pl.when(j == 0)(functools.partial(gdn_qkv, GDN_DIM ** -0.5))
    pl.when(j == 1)(functools.partial(gdn_qkv, 1.0))
    pl.when(j == 2)(functools.partial(gdn_qkv, None))

    @pl.when(j >= GDN_COL_TILES - 1)
    def _():
        o_ref[...] = _dot(a_ref[...], w16_ref[...])


def _in_proj(n1, wt_all, conv_w):
    m = n1.shape[0]
    tm, tn = META_BLOCK, IN_PROJ_TN
    nm = m // tm
    n = GDN_COL_TILES * tn + 3 * DIFF_WIDTH
    assert wt_all.shape[0] == DIFF_ROW0 + 3 * DIFF_WIDTH and (3 * DIFF_WIDTH) % tn == 0
    seq_order = lambda i: (i + nm - 1) % nm
    return pl.pallas_call(
        _in_proj_kernel,
        grid=(n // tn, nm),
        in_specs=[
            pl.BlockSpec((tm, D_MODEL), lambda j, i: (seq_order(i), 0)),
            pl.BlockSpec((tn, D_MODEL), lambda j, i: (jnp.minimum(j, GDN_COL_TILES - 1), 0)),
            pl.BlockSpec(memory_space=pl.ANY),
            pl.BlockSpec((CONV_WIDTH, tn), lambda j, i: (0, jnp.minimum(j, 2))),
        ],
        out_specs=pl.BlockSpec((tm, tn), lambda j, i: (seq_order(i), j)),
        out_shape=jax.ShapeDtypeStruct((m, n), F32),
        scratch_shapes=[pltpu.VMEM((8, tn), F32), pltpu.VMEM((GDN_HEADS // 2, tm, 2 * GDN_DIM), F32),
                        pltpu.VMEM((D_MODEL, tn), BF16), pltpu.VMEM((tn, D_MODEL), F32),
                        pltpu.SemaphoreType.DMA(())],
        compiler_params=_params(("arbitrary", "arbitrary")),
        name="in_proj",
    )(n1, wt_all, wt_all, conv_w)


GDN_LOCAL_CHUNKS = 4


def _gdn_local_kernel(q_ref, k_ref, v_ref, gcol_ref, grow_ref,
                      w_ref, u_ref, qg_ref, kd_ref, attn_ref, egl_ref):
    q_all = q_ref[...]
    k_all = k_ref[...]
    v_all = v_ref[...]
    gcol = gcol_ref[...]
    grow = grow_ref[...]
    ii = lax.broadcasted_iota(jnp.int32, (CHUNK, CHUNK), 0)
    jj = lax.broadcasted_iota(jnp.int32, (CHUNK, CHUNK), 1)
    units = [(c, h) for c in range(GDN_LOCAL_CHUNKS) for h in range(GDN_HEADS)]
    rows_of = lambda c: slice(c * CHUNK, (c + 1) * CHUNK)
    lanes_of = lambda h: slice(h * GDN_DIM, (h + 1) * GDN_DIM)
    qs, ks, kbs, vbs, egcs, gcs, kqs = [], [], [], [], [], [], []
    for c, h in units:
        rs, ls = rows_of(c), lanes_of(h)
        q = q_all[rs, ls]
        k = k_all[rs, ls]
        beta_c = gcol[rs, h:h + 1]
        gc_c = gcol[rs, GDN_HEADS + h:GDN_HEADS + h + 1]
        kb = k * beta_c
        qs.append(q)
        ks.append(k)
        kbs.append(kb)
        vbs.append(v_all[rs, ls] * beta_c)
        gcs.append(gc_c)
        egcs.append(jnp.exp(gc_c))
        kqs.append(_dot_nt(jnp.concatenate([kb.astype(BF16), q.astype(BF16)], axis=0), k.astype(BF16)))
    lms, attns = [], []
    for (c, h), kq, gc_c in zip(units, kqs, gcs):
        gc_r = grow[GDN_HEADS + h:GDN_HEADS + h + 1, rows_of(c)]
        decay = jnp.exp(jnp.where(ii >= jj, gc_c - gc_r, MASK_VALUE))
        lms.append(jnp.where(ii > jj, kq[:CHUNK] * decay, 0.0))
        attns.append(kq[CHUNK:] * decay)
    xor = ii ^ jj
    eye = jnp.where(ii == jj, 1.0, 0.0)
    xs = [eye - jnp.where(xor == 1, lm, 0.0) for lm in lms]
    level = 1
    while (2 << level) <= CHUNK:
        sel = (xor >> level) == 1
        ys = [_dot(jnp.where(sel, lm, 0.0).astype(BF16), x.astype(BF16)) for lm, x in zip(lms, xs)]
        xs = [x - _dot(x.astype(BF16), y.astype(BF16)) for x, y in zip(xs, ys)]
        level += 1
    uws = [_dot(x.astype(BF16), jnp.concatenate([vb.astype(BF16), (kb * egc).astype(BF16)], axis=1))
           for x, vb, kb, egc in zip(xs, vbs, kbs, egcs)]
    for (c, h), uw, q, k, egc, gc_c, attn in zip(units, uws, qs, ks, egcs, gcs, attns):
        rs, ls = rows_of(c), lanes_of(h)
        u_ref[rs, ls] = uw[:, :GDN_DIM]
        w_ref[rs, ls] = uw[:, GDN_DIM:].astype(BF16)
        qg_ref[rs, ls] = (q * egc).astype(BF16)
        gc_last = gc_c[CHUNK - 1:CHUNK]
        kd_ref[rs, ls] = (k * jnp.exp(gc_last - gc_c)).astype(BF16)
        attn_ref[h, rs, :] = attn.astype(BF16)
        egl_ref[c, h:h + 1, :] = jnp.broadcast_to(jnp.exp(gc_last), (1, GDN_DIM))


def _gdn_local_specs(proj, gcol, grow):
    rows = proj.shape[0]
    rb = GDN_LOCAL_CHUNKS * CHUNK
    blk = lambda col: pl.BlockSpec((rb, GDN_WIDTH), lambda i, col=col: (i, col))
    row_out = lambda dt: jax.ShapeDtypeStruct((rows, GDN_WIDTH), dt)
    in_specs = [
        blk(0), blk(1), blk(2),
        pl.BlockSpec((rb, GATE_LANES), lambda i: (i, 0)),
        pl.BlockSpec((2 * GDN_HEADS, rb), lambda i: (0, i)),
    ]
    out_specs = [
        blk(0), blk(0), blk(0), blk(0),
        pl.BlockSpec((GDN_HEADS, rb, CHUNK), lambda i: (0, i, 0)),
        pl.BlockSpec((GDN_LOCAL_CHUNKS, GDN_HEADS, GDN_DIM), lambda i: (i, 0, 0)),
    ]
    out_shape = [
        row_out(BF16),
        row_out(F32),
        row_out(BF16),
        row_out(BF16),
        jax.ShapeDtypeStruct((GDN_HEADS, rows, CHUNK), BF16),
        jax.ShapeDtypeStruct((rows // CHUNK, GDN_HEADS, GDN_DIM), F32),
    ]
    return (proj, proj, proj, gcol, grow), in_specs, out_specs, out_shape


GDN_STATE_CHUNKS = 2


def _gdn_state_chunks(chunks, w_ref, u_ref, qg_ref, kd_ref, attn_ref, egl_ref, s_ref,
                      z_ref=None, gain_ref=None, o_ref=None):
    heads = range(GDN_HEADS)
    lanes = [slice(h * GDN_DIM, (h + 1) * GDN_DIM) for h in heads]
    ss = [s_ref[h] for h in heads]
    for ck in chunks:
        rw = slice(ck * CHUNK, (ck + 1) * CHUNK)
        rs = [_dot(jnp.concatenate([w_ref[rw, ls], qg_ref[rw, ls]], axis=0), s.astype(BF16))
              for ls, s in zip(lanes, ss)]
        vns = [(u_ref[rw, ls] - r[:CHUNK]).astype(BF16) for ls, r in zip(lanes, rs)]
        if o_ref is not None:
            os_ = [r[CHUNK:] + _dot(attn_ref[h, rw, :], vn) for h, r, vn in zip(heads, rs, vns)]
        ss = [s * egl_ref[ck, h:h + 1, :] + _dot_tn(kd_ref[rw, ls], vn)
              for h, ls, s, vn in zip(heads, lanes, ss, vns)]
        if o_ref is not None:
            for ls, o in zip(lanes, os_):
                on = o * lax.rsqrt(jnp.mean(o * o, axis=-1, keepdims=True) + NORM_EPS) * gain_ref[...]
                o_ref[rw, ls] = (on * _silu(z_ref[rw, ls])).astype(BF16)
    for h, s in zip(heads, ss):
        s_ref[h] = s


KV_TILE = 128
VT_ROWS = DIFF_VDIM + 16
Q_SCALE = DIFF_DIM ** -0.5 * math.log2(math.e)


PREP_ROWS = 2 * KV_TILE


def _attn_prep_kernel(q_ref, k_ref, v_ref, cos_ref, sin_ref, qg_ref, kg_ref, gsum_ref, eye_ref,
                      q2_ref, kr_ref, vt_ref):
    cos = cos_ref[...]
    sin = sin_ref[...]
    gsum = gsum_ref[...]
    eye = eye_ref[...]
    lane = lax.broadcasted_iota(jnp.int32, cos.shape, 1)
    first_half = (lane % DIFF_DIM) < (DIFF_DIM // 2)
    low_map = lane < DIFF_DIM

    def norm_rope(x, gain):
        ms = _dot((x * x).astype(BF16), gsum) * (1.0 / DIFF_DIM)
        xn = x * lax.rsqrt(ms + NORM_EPS) * gain
        rot = jnp.where(first_half, pltpu.roll(xn, LANES - DIFF_DIM // 2, axis=1),
                        pltpu.roll(xn, DIFF_DIM // 2, axis=1))
        return xn * cos + rot * sin

    for h in range(DIFF_HEADS):
        ls = slice(h * DIFF_VDIM, (h + 1) * DIFF_VDIM)
        q = norm_rope(q_ref[:, ls], qg_ref[...]) * Q_SCALE
        q2_ref[0, :, ls] = jnp.where(low_map, q, 0.0).astype(BF16)
        q2_ref[1, :, ls] = jnp.where(low_map, 0.0, q).astype(BF16)
        kr_ref[:, ls] = norm_rope(k_ref[:, ls], kg_ref[...]).astype(BF16)
        for t in range(PREP_ROWS // KV_TILE):
            v = v_ref[t * KV_TILE:(t + 1) * KV_TILE, ls].astype(BF16)
            vt_ref[t, h * VT_ROWS:h * VT_ROWS + DIFF_VDIM, :] = _dot_nt(eye, v).astype(BF16)
            vt_ref[t, h * VT_ROWS + DIFF_VDIM:(h + 1) * VT_ROWS, :] = jnp.ones(
                (VT_ROWS - DIFF_VDIM, KV_TILE), BF16)


def _attn_prep_specs(proj, cos, sin, qgain, kgain, gsum, eye):
    rows = proj.shape[0]
    tm = PREP_ROWS
    col = lambda c: pl.BlockSpec((tm, DIFF_WIDTH), lambda i, c=c: (i, c))
    small = lambda shape: pl.BlockSpec(shape, lambda i: (0,) * len(shape))
    in_specs = [
        col(4), col(5), col(6),
        pl.BlockSpec((tm, LANES), lambda i: (i, 0)),
        pl.BlockSpec((tm, LANES), lambda i: (i, 0)),
        small((1, LANES)), small((1, LANES)), small((LANES, LANES)), small((LANES, LANES)),
    ]
    out_specs = [
        pl.BlockSpec((2, tm, DIFF_WIDTH), lambda i: (0, i, 0)),
        pl.BlockSpec((tm, DIFF_WIDTH), lambda i: (i, 0)),
        pl.BlockSpec((tm // KV_TILE, DIFF_HEADS * VT_ROWS, KV_TILE), lambda i: (i, 0, 0)),
    ]
    out_shape = [
        jax.ShapeDtypeStruct((2, rows, DIFF_WIDTH), BF16),
        jax.ShapeDtypeStruct((rows, DIFF_WIDTH), BF16),
        jax.ShapeDtypeStruct((rows // KV_TILE, DIFF_HEADS * VT_ROWS, KV_TILE), BF16),
    ]
    return (proj, proj, proj, cos, sin, qgain, kgain, gsum, eye), in_specs, out_specs, out_shape


def _gdn_local_attn_prep(gdn_part, prep_part):
    g_ops, g_in, g_out, g_shape = gdn_part
    p_ops, p_in, p_out, p_shape = prep_part
    assert GDN_LOCAL_CHUNKS * CHUNK == PREP_ROWS
    rows = g_ops[0].shape[0]

    def body(*refs):
        g_i, rest = refs[:len(g_in)], refs[len(g_in):]
        p_i, rest = rest[:len(p_in)], rest[len(p_in):]
        g_o, p_o = rest[:len(g_out)], rest[len(g_out):]
        _attn_prep_kernel(*p_i, *p_o)
        _gdn_local_kernel(*g_i, *g_o)

    outs = pl.pallas_call(
        body,
        grid=(rows // PREP_ROWS,),
        in_specs=g_in + p_in,
        out_specs=g_out + p_out,
        out_shape=g_shape + p_shape,
        compiler_params=_params(("parallel",)),
        name="gdn_local_attn_prep",
    )(*g_ops, *p_ops)
    return outs[:len(g_out)], outs[len(g_out):]


ATTN_BLOCK = 1024
ATTN_QSUB = 256

def _diff_attn_kernel(q_ref, k_ref, vt_ref, lam_ref, gain_ref,
                      gw_ref, gu_ref, gqg_ref, gkd_ref, gattn_ref, gegl_ref, gz_ref,
                      mw_ref, mu_ref, mqg_ref, mkd_ref, mattn_ref, megl_ref, ggain_ref,
                      o_ref, og_ref, acc_ref, st_ref, s_ref):
    i = pl.program_id(1)
    step = pl.program_id(0) * pl.num_programs(1) + i

    @pl.when(step == 0)
    def _():
        s_ref[...] = jnp.zeros_like(s_ref)
        _gdn_state_chunks(range(1), mw_ref, mu_ref, mqg_ref, mkd_ref, mattn_ref, megl_ref, s_ref)
    rows = k_ref.shape[0]
    bk = ATTN_QSUB
    nsub = q_ref.shape[1] // ATTN_QSUB
    chains = [(mp, sb) for mp in range(2) for sb in range(nsub)]
    every = list(range(len(chains)))
    qs = [q_ref[mp, sb * ATTN_QSUB:(sb + 1) * ATTN_QSUB, :] for mp, sb in chains]
    kv_tiles = bk // KV_TILE

    def update(sts, vt, ms, which):
        ms = list(ms)
        first = ms[which[0]] is None
        cms = [jnp.max(st, axis=0, keepdims=True) for st in sts]
        m_new = cms if first else [jnp.maximum(ms[c], cm) for c, cm in zip(which, cms)]
        ps = [jnp.exp2(st - mn).astype(BF16) for st, mn in zip(sts, m_new)]
        pvs = [_dot(vt, p) for p in ps]
        for n, c in enumerate(which):
            if first:
                acc_ref[c] = pvs[n]
            else:
                acc_ref[c] = jnp.exp2(ms[c] - m_new[n]) * acc_ref[c] + pvs[n]
            ms[c] = m_new[n]
        return ms

    def store_scores(j, slot, which):
        start = j * bk if isinstance(j, int) else pl.multiple_of(j * bk, bk)
        k_c = k_ref[pl.ds(start, bk), :]
        for c in which:
            st_ref[slot, c] = _dot_nt(k_c, qs[c])

    def values_t(j):
        return jnp.concatenate([vt_ref[j * kv_tiles + t] for t in range(kv_tiles)], axis=1)

    gdn_x = functools.partial(_gdn_state_chunks, w_ref=gw_ref, u_ref=gu_ref, qg_ref=gqg_ref, kd_ref=gkd_ref,
                              attn_ref=gattn_ref, egl_ref=gegl_ref, s_ref=s_ref, z_ref=gz_ref,
                              gain_ref=ggain_ref, o_ref=og_ref)
    gdn_x(range(GDN_STATE_CHUNKS // 2))

    k_meta = k_ref[rows - N_META:rows, :]
    sts = [_dot_nt(k_meta, q) for q in qs]
    store_scores(0, 0, every)
    vt_meta = vt_ref[rows // KV_TILE - 1][:, KV_TILE - N_META:]
    ms = update(sts, vt_meta, [None] * len(chains), every)

    def full_blocks(t, ms):
        for n in range(nsub):
            j = nsub * t + n
            store_scores(j + 1, (n + 1) % 2, every)
            ms = update([st_ref[n % 2, c] for c in every], values_t(j), ms, every)
        return tuple(ms)

    ms = lax.fori_loop(0, i, full_blocks, tuple(ms))
    tri = (lax.broadcasted_iota(jnp.int32, (bk, ATTN_QSUB), 0)
           <= lax.broadcasted_iota(jnp.int32, (bk, ATTN_QSUB), 1))
    for d in range(nsub):
        if d + 1 < nsub:
            store_scores(i * nsub + d + 1, (d + 1) % 2, [c for c in every if chains[c][1] > d])
        which = [c for c in every if chains[c][1] >= d]
        sts = [jnp.where(tri, st_ref[d % 2, c], MASK_VALUE) if chains[c][1] == d else st_ref[d % 2, c]
               for c in which]
        ms = update(sts, values_t(i * nsub + d), ms, which)

    gdn_x(range(GDN_STATE_CHUNKS // 2, GDN_STATE_CHUNKS))

    lp = lam_ref[...]
    lam = (jnp.exp(jnp.sum(lp[0:1] * lp[1:2], axis=-1, keepdims=True))
           - jnp.exp(jnp.sum(lp[2:3] * lp[3:4], axis=-1, keepdims=True)) + LAMBDA_INIT)
    gain = gain_ref[...]
    for sb in range(nsub):
        num1, num2 = acc_ref[sb, :DIFF_VDIM, :], acc_ref[nsub + sb, :DIFF_VDIM, :]
        l1 = acc_ref[sb, DIFF_VDIM:DIFF_VDIM + 1, :]
        l2 = acc_ref[nsub + sb, DIFF_VDIM:DIFF_VDIM + 1, :]
        ot = num1 * (1.0 / l1) - num2 * (lam / l2)
        ot = ot * lax.rsqrt(jnp.mean(ot * ot, axis=0, keepdims=True) + NORM_EPS) * gain
        o_ref[sb * ATTN_QSUB:(sb + 1) * ATTN_QSUB, :] = (ot * (1.0 - LAMBDA_INIT)).T.astype(BF16)


def _diff_attn_gdn_state(q2, kr, vt, lam_params, gain_col, gdn_local_out, proj, gdn_gain, seq):
    rows = kr.shape[0]
    bq = ATTN_BLOCK
    nq = seq // bq
    nchains = 2 * (bq // ATTN_QSUB)
    gb = GDN_STATE_CHUNKS * CHUNK
    last_chunk = rows // CHUNK - 1
    assert seq % bq == 0 and bq % (2 * ATTN_QSUB) == 0 and DIFF_HEADS * nq * gb == seq
    w, u, qg, kd, attn, egl = gdn_local_out
    step = lambda h, i: h * nq + i
    xrows = lambda col: pl.BlockSpec((gb, GDN_WIDTH), lambda h, i, col=col: (step(h, i), col))
    mrows = pl.BlockSpec((CHUNK, GDN_WIDTH), lambda h, i: (last_chunk, 0))
    return pl.pallas_call(
        _diff_attn_kernel,
        grid=(DIFF_HEADS, nq),
        in_specs=[
            pl.BlockSpec((2, bq, DIFF_VDIM), lambda h, i: (0, i, h)),
            pl.BlockSpec((rows, DIFF_VDIM), lambda h, i: (0, h)),
            pl.BlockSpec((rows // KV_TILE, VT_ROWS, KV_TILE), lambda h, i: (0, h, 0)),
            pl.BlockSpec((4, DIFF_DIM), lambda h, i: (0, 0)),
            pl.BlockSpec((DIFF_VDIM, 1), lambda h, i: (0, 0)),
            xrows(0), xrows(0), xrows(0), xrows(0),
            pl.BlockSpec((GDN_HEADS, gb, CHUNK), lambda h, i: (0, step(h, i), 0)),
            pl.BlockSpec((GDN_STATE_CHUNKS, GDN_HEADS, GDN_DIM), lambda h, i: (step(h, i), 0, 0)),
            xrows(3),
            mrows, mrows, mrows, mrows,
            pl.BlockSpec((GDN_HEADS, CHUNK, CHUNK), lambda h, i: (0, last_chunk, 0)),
            pl.BlockSpec((1, GDN_HEADS, GDN_DIM), lambda h, i: (last_chunk, 0, 0)),
            pl.BlockSpec((1, GDN_DIM), lambda h, i: (0, 0)),
        ],
        out_specs=[
            pl.BlockSpec((bq, DIFF_VDIM), lambda h, i: (i, h)),
            pl.BlockSpec((gb, GDN_WIDTH), lambda h, i: (step(h, i), 0)),
        ],
        out_shape=[
            jax.ShapeDtypeStruct((seq, DIFF_WIDTH), BF16),
            jax.ShapeDtypeStruct((seq, GDN_WIDTH), BF16),
        ],
        scratch_shapes=[pltpu.VMEM((nchains, VT_ROWS, ATTN_QSUB), F32),
                        pltpu.VMEM((2, nchains, ATTN_QSUB, ATTN_QSUB), F32),
                        pltpu.VMEM((GDN_HEADS, GDN_DIM, GDN_DIM), F32)],
        compiler_params=_params(("arbitrary", "arbitrary")),
        name="diff_attn_gdn_state",
    )(q2, kr, vt, lam_params, gain_col, w, u, qg, kd, attn, egl, proj, w, u, qg, kd, attn, egl, gdn_gain)


RESIDUAL_BUFFERS = 3


def _out_proj_kernel(mg_ref, md_ref, wg_ref, wd_ref, h_hbm_ref, gain_ref, h2_ref, n2_ref, hbuf_ref, hsem):
    i = pl.program_id(0)
    n = pl.num_programs(0)
    tm = h2_ref.shape[0]

    def residual_copy(t):
        slot = t % RESIDUAL_BUFFERS
        return pltpu.make_async_copy(h_hbm_ref.at[pl.ds(pl.multiple_of(t * tm, tm), tm), :],
                                     hbuf_ref.at[slot], hsem.at[slot])

    @pl.when(i == 0)
    def _():
        for t in range(RESIDUAL_BUFFERS - 1):
            residual_copy(t).start()

    @pl.when(i + RESIDUAL_BUFFERS - 1 < n)
    def _():
        residual_copy(i + RESIDUAL_BUFFERS - 1).start()

    residual_copy(i).wait()
    h2 = (hbuf_ref[i % RESIDUAL_BUFFERS] + _dot(mg_ref[...], wg_ref[...].astype(BF16))
          + _dot(md_ref[...], wd_ref[...].astype(BF16)))
    h2_ref[...] = h2
    ms = jnp.mean(h2 * h2, axis=-1, keepdims=True)
    n2_ref[...] = (h2 * lax.rsqrt(ms + NORM_EPS) * gain_ref[...]).astype(BF16)


def _out_proj(mix_g, mix_d, w_out, h, gain, seq):
    tm = _pick(seq, (512, 128))
    return pl.pallas_call(
        _out_proj_kernel,
        grid=(seq // tm,),
        in_specs=[
            pl.BlockSpec((tm, GDN_WIDTH), lambda i: (i, 0)),
            pl.BlockSpec((tm, DIFF_WIDTH), lambda i: (i, 0)),
            pl.BlockSpec((GDN_WIDTH, D_MODEL), lambda i: (0, 0), pipeline_mode=pl.Buffered(1)),
            pl.BlockSpec((DIFF_WIDTH, D_MODEL), lambda i: (1, 0), pipeline_mode=pl.Buffered(1)),
            pl.BlockSpec(memory_space=pl.ANY),
            pl.BlockSpec((1, D_MODEL), lambda i: (0, 0)),
        ],
        out_specs=[
            pl.BlockSpec((tm, D_MODEL), lambda i: (i, 0)),
            pl.BlockSpec((tm, D_MODEL), lambda i: (i, 0)),
        ],
        out_shape=[
            jax.ShapeDtypeStruct((seq, D_MODEL), F32),
            jax.ShapeDtypeStruct((seq, D_MODEL), BF16),
        ],
        scratch_shapes=[pltpu.VMEM((RESIDUAL_BUFFERS, tm, D_MODEL), F32),
                        pltpu.SemaphoreType.DMA((RESIDUAL_BUFFERS,))],
        compiler_params=_params(("arbitrary",)),
        name="out_proj",
    )(mix_g, mix_d, w_out, w_out, h, gain)


def _gate_up_kernel(n_ref, wg_ref, wu_ref, wd_ref, a_ref, wd16_ref):
    n = n_ref[...]
    half = wg_ref.shape[1] // 2
    for c in range(2):
        cs = slice(c * half, (c + 1) * half)
        g = _dot(n, wg_ref[:, cs].astype(BF16))
        u = _dot(n, wu_ref[:, cs].astype(BF16))
        a_ref[:, cs] = (_silu(g) * u).astype(BF16)
    wd16_ref[...] = wd_ref[...].astype(BF16)


def _gate_up(n2, w_gu, w_down):
    seq = n2.shape[0]
    tm = _pick(seq, (1024, 128))
    tn = 512
    nt, nm = D_FF // tn, seq // tm
    wd_rows = D_FF // (nt * nm)
    assert wd_rows * nt * nm == D_FF and wd_rows % 16 == 0
    return pl.pallas_call(
        _gate_up_kernel,
        grid=(nt, nm),
        in_specs=[
            pl.BlockSpec((tm, D_MODEL), lambda j, i: (i, 0)),
            pl.BlockSpec((D_MODEL, tn), lambda j, i: (0, j)),
            pl.BlockSpec((D_MODEL, tn), lambda j, i: (0, j + nt)),
            pl.BlockSpec((wd_rows, D_MODEL), lambda j, i: (j * nm + i, 0)),
        ],
        out_specs=[
            pl.BlockSpec((tm, tn), lambda j, i: (i, j)),
            pl.BlockSpec((wd_rows, D_MODEL), lambda j, i: (j * nm + i, 0)),
        ],
        out_shape=[
            jax.ShapeDtypeStruct((seq, D_FF), BF16),
            jax.ShapeDtypeStruct((D_FF, D_MODEL), BF16),
        ],
        compiler_params=_params(("parallel", "parallel")),
        name="ffn_gate_up",
    )(n2, w_gu, w_gu, w_down)


def _down_kernel(a_ref, w_ref, h_ref, o_ref):
    o_ref[...] = h_ref[...] + _dot(a_ref[...], w_ref[...])


def _down(act, w_down16, h2):
    seq = act.shape[0]
    tm = _pick(seq, (512, 128))
    tn = D_MODEL
    return pl.pallas_call(
        _down_kernel,
        grid=(D_MODEL // tn, seq // tm),
        in_specs=[
            pl.BlockSpec((tm, D_FF), lambda j, i: (i, 0)),
            pl.BlockSpec((D_FF, tn), lambda j, i: (0, j), pipeline_mode=pl.Buffered(1)),
            pl.BlockSpec((tm, tn), lambda j, i: (i, j)),
        ],
        out_specs=pl.BlockSpec((tm, tn), lambda j, i: (i, j)),
        out_shape=jax.ShapeDtypeStruct((seq, D_MODEL), F32),
        compiler_params=_params(("parallel", "parallel")),
        name="ffn_down",
    )(act, w_down16, h2)


def _rope_tables(seq):
    half = DIFF_DIM // 2
    pos = jnp.concatenate([jnp.arange(seq) + N_META, jnp.zeros((META_BLOCK - N_META,), jnp.int32),
                           jnp.arange(N_META)]).astype(F32)
    inv_freq = ROPE_THETA ** (-jnp.arange(half, dtype=F32) / half)
    ang = pos[:, None] * inv_freq[None, :]
    cos = jnp.tile(jnp.cos(ang), (1, LANES // half))
    sin = jnp.sin(ang)
    sin = jnp.tile(jnp.concatenate([-sin, sin], axis=1), (1, LANES // DIFF_DIM))
    return cos, sin


def _lane_pad(v, offset):
    return jnp.zeros((1, GATE_LANES), F32).at[0, offset:offset + v.shape[0]].set(v.astype(F32))


def kernel(x, meta_tokens, attn_norm, w_in, conv_w, a_log, dt_bias, gdn_norm, q_norm, k_norm,
           lambda_q1, lambda_k1, lambda_q2, lambda_k2, diff_norm, w_out, ffn_norm, w_gate_up, w_down):
    assert x.shape[0] == 1 and x.shape[2] == D_MODEL
    seq = x.shape[1]
    assert seq % META_BLOCK == 0
    xs = x[0]
    meta_block = jnp.concatenate([jnp.zeros((META_BLOCK - N_META, D_MODEL), xs.dtype),
                                  meta_tokens.astype(xs.dtype)], axis=0)

    wt_in = w_in[0].T
    wt_ba = jnp.pad(wt_in[4 * GDN_WIDTH:DIFF_ROW0], ((0, GATE_LANES - 2 * GDN_HEADS), (0, 0))).astype(BF16)

    n1, gcol, grow = _prenorm_gate(xs, meta_block, attn_norm, wt_ba, _lane_pad(a_log[0], GDN_HEADS),
                                   _lane_pad(dt_bias[0], GDN_HEADS))
    proj = _in_proj(n1, wt_in, conv_w[0])

    cos, sin = _rope_tables(seq)
    tile2 = lambda g: jnp.tile(g.astype(F32), (1, LANES // DIFF_DIM))
    lane = np.arange(LANES)
    gsum = jnp.asarray((lane[:, None] // DIFF_DIM) == (lane[None, :] // DIFF_DIM), BF16)
    eye = jnp.asarray(lane[:, None] == lane[None, :], BF16)
    gdn_local_out, (q2, kr, vt) = _gdn_local_attn_prep(
        _gdn_local_specs(proj, gcol, grow),
        _attn_prep_specs(proj, cos, sin, tile2(q_norm), tile2(k_norm), gsum, eye))
    lam_params = jnp.concatenate([lambda_q1, lambda_k1, lambda_q2, lambda_k2], axis=0).astype(F32)
    mix_d, mix_g = _diff_attn_gdn_state(q2, kr, vt, lam_params, diff_norm.astype(F32).reshape(DIFF_VDIM, 1),
                                        gdn_local_out, proj, gdn_norm, seq)

    h2, n2 = _out_proj(mix_g, mix_d, w_out[0], xs, ffn_norm, seq)
    act, w_down16 = _gate_up(n2, w_gate_up[0], w_down[0])
    out = _down(act, w_down16, h2)
    return out[None]
```
